```python
import math
import jax
import jax.numpy as jnp
from jax import lax
import numpy as np

D_MODEL = 1024
BATCH = 8
SEQ = 8192
DEPTH = 4

DN_HEADS = 8
DN_DK = 64
DN_DV = 64
DN_CHUNK = 64
CONV_W = 4
DN_QKV = DN_HEADS * (2 * DN_DK + DN_DV)
NSA_HEADS = 8
NSA_GROUPS = 2
NSA_HPG = NSA_HEADS // NSA_GROUPS
NSA_DH = 64
CMP_LEN = 32
CMP_STRIDE = 16
SEL_LEN = 64
SEL_TOP = 16
WINDOW = 512
NSA_Q_BLOCK = 64
ROPE_THETA = 10000.0
FFN_DIM = 3584
N_EXPERTS = 8
TOP_K = 2
EPS = 1e-6
NEG = -1e30
FORCE = 1e6

SPLIT_SIZES = (
    DN_QKV,
    DN_HEADS * DN_DV,
    DN_HEADS,
    DN_HEADS,
    NSA_HEADS * NSA_DH,
    NSA_GROUPS * NSA_DH,
    NSA_GROUPS * NSA_DH,
    NSA_GROUPS * NSA_DH,
    NSA_GROUPS * NSA_DH,
    NSA_GROUPS * NSA_DH,
    NSA_GROUPS * NSA_DH,
    3 * NSA_HEADS,
    2 * D_MODEL,
)
P_IN = sum(SPLIT_SIZES)

kernel_name = 'hybrid_deltanet_nsa_moe_adaln_trunk'


def rms_norm(x, w):
    xf = x.astype(jnp.float32)
    y = xf * lax.rsqrt(jnp.mean(xf * xf, -1, keepdims=True) + EPS)
    return (y * w.astype(jnp.float32)).astype(x.dtype)


def l2_norm(x):
    xf = x.astype(jnp.float32)
    return (xf * lax.rsqrt(jnp.sum(xf * xf, -1, keepdims=True) + EPS)).astype(x.dtype)


def rope_angles(pos):
    inv = 1.0 / (ROPE_THETA ** (jnp.arange(0, NSA_DH, 2, dtype=jnp.float32) / NSA_DH))
    ang = pos.astype(jnp.float32)[..., None] * inv
    return jnp.cos(ang)[:, :, None, :], jnp.sin(ang)[:, :, None, :]


def apply_rope(x, cos, sin):
    xf = x.astype(jnp.float32)
    x1, x2 = jnp.split(xf, 2, -1)
    return jnp.concatenate([x1 * cos - x2 * sin, x2 * cos + x1 * sin], -1).astype(x.dtype)


def masked_softmax(s, mask):
    s = jnp.where(mask, s.astype(jnp.float32), NEG)
    m = jnp.max(s, -1, keepdims=True)
    e = jnp.where(mask, jnp.exp(s - m), 0.0)
    return e / jnp.maximum(jnp.sum(e, -1, keepdims=True), 1e-30)


def causal_conv_silu(x, w):
    ch = x.shape[-1]
    y = lax.conv_general_dilated(x, w[:, None, :].astype(x.dtype), window_strides=(1,),
                                 padding=[(CONV_W - 1, 0)],
                                 dimension_numbers=('NWC', 'WIO', 'NWC'),
                                 feature_group_count=ch)
    return jax.nn.silu(y)


def gated_delta_rule(q, k, v, beta, g):
    b_, s_, h_, dk = q.shape
    dv = v.shape[-1]
    n, c = s_ // DN_CHUNK, DN_CHUNK
    f32 = jnp.float32
    qc = q.astype(f32).reshape(b_, n, c, h_, dk).transpose(1, 0, 3, 2, 4)
    kc = k.astype(f32).reshape(b_, n, c, h_, dk).transpose(1, 0, 3, 2, 4)
    vc = v.astype(f32).reshape(b_, n, c, h_, dv).transpose(1, 0, 3, 2, 4)
    bc = beta.astype(f32).reshape(b_, n, c, h_).transpose(1, 0, 3, 2)
    gc = jnp.cumsum(g.astype(f32).reshape(b_, n, c, h_).transpose(1, 0, 3, 2), axis=-1)
    tril = jnp.tril(jnp.ones((c, c), bool))
    strict = jnp.tril(jnp.ones((c, c), bool), -1)
    diff = gc[..., :, None] - gc[..., None, :]
    decay = jnp.where(tril, jnp.exp(jnp.where(tril, diff, 0.0)), 0.0)
    kb = kc * bc[..., None]
    lower = jnp.where(strict, jnp.einsum('nbhcd,nbhsd->nbhcs', kb, kc) * decay, 0.0)
    eye = jnp.eye(c, dtype=f32)
    tinv = lax.linalg.triangular_solve(eye + lower, jnp.broadcast_to(eye, lower.shape),
                                       left_side=True, lower=True, unit_diagonal=True)
    u = tinv @ (vc * bc[..., None])
    w = tinv @ (kb * jnp.exp(gc)[..., None])
    a_intra = jnp.where(tril, jnp.einsum('nbhcd,nbhsd->nbhcs', qc, kc) * decay, 0.0)
    q_dec = qc * jnp.exp(gc)[..., None]
    k_dec = kc * jnp.exp(gc[..., -1:] - gc)[..., None]
    g_last = jnp.exp(gc[..., -1])

    def step(state, xs):
        u_n, w_n, a_n, qd_n, kd_n, gl_n = xs
        v_new = u_n - w_n @ state
        o_n = qd_n @ state + a_n @ v_new
        state = state * gl_n[..., None, None] + jnp.swapaxes(kd_n, -1, -2) @ v_new
        return state, o_n

    s0 = jnp.zeros((b_, h_, dk, dv), f32)
    _, o = lax.scan(step, s0, (u, w, a_intra, q_dec, k_dec, g_last))
    return o.transpose(1, 0, 3, 2, 4).reshape(b_, s_, h_, dv).astype(q.dtype)


def compress_blocks(x, pos_emb, w1, w2):
    b_, s_ = x.shape[:2]
    n_cmp = (s_ - CMP_LEN) // CMP_STRIDE + 1
    idx = jnp.arange(n_cmp)[:, None] * CMP_STRIDE + jnp.arange(CMP_LEN)[None, :]
    blk = x[:, idx] + pos_emb[None, None, :, None, :]
    blk = blk.transpose(0, 1, 3, 2, 4).reshape(b_, n_cmp, NSA_GROUPS, CMP_LEN * NSA_DH)
    return jax.nn.silu(blk @ w1) @ w2


def nsa_core(q, k_c, v_c, k_s, v_s, k_w, v_w, gates):
    b_, s_ = q.shape[:2]
    n_cmp = k_c.shape[1]
    n_sel = s_ // SEL_LEN
    k_top = min(SEL_TOP, n_sel)
    qbk = NSA_Q_BLOCK
    scale = NSA_DH ** -0.5
    cmp_start = jnp.arange(n_cmp) * CMP_STRIDE
    cmp_end = cmp_start + CMP_LEN - 1
    blk_start = jnp.arange(n_sel) * SEL_LEN
    overlap = ((cmp_start[:, None] < blk_start[None, :] + SEL_LEN) &
               (cmp_start[:, None] + CMP_LEN > blk_start[None, :])).astype(jnp.float32)
    ks_blocks = k_s.reshape(b_, n_sel, SEL_LEN, NSA_GROUPS, NSA_DH).transpose(0, 3, 1, 2, 4)
    vs_blocks = v_s.reshape(b_, n_sel, SEL_LEN, NSA_GROUPS, NSA_DH).transpose(0, 3, 1, 2, 4)
    kw_pad = jnp.pad(k_w, ((0, 0), (WINDOW, 0), (0, 0), (0, 0)))
    vw_pad = jnp.pad(v_w, ((0, 0), (WINDOW, 0), (0, 0), (0, 0)))
    b_ix = jnp.arange(b_)[:, None, None, None]
    g_ix = jnp.arange(NSA_GROUPS)[None, :, None, None]
    jb = jnp.arange(n_sel)

    def block(s0):
        t = s0 + jnp.arange(qbk)
        qb = lax.dynamic_slice_in_dim(q, s0, qbk, 1).reshape(b_, qbk, NSA_GROUPS, NSA_HPG, NSA_DH) * scale
        s_c = jnp.einsum('bqghd,bcgd->bghqc', qb, k_c)
        p_c = masked_softmax(s_c, cmp_end[None, :] <= t[:, None])
        o_c = jnp.einsum('bghqc,bcgd->bqghd', p_c.astype(v_c.dtype), v_c)
        imp = jnp.einsum('bghqc,cj->bgqj', p_c, overlap)
        cur = t // SEL_LEN
        forced = (jb[None] == 0) | (jb[None] == cur[:, None]) | (jb[None] == cur[:, None] - 1)
        causal = blk_start[None] <= t[:, None]
        imp = jnp.where(forced, FORCE, jnp.where(causal, imp, -FORCE))
        _, sel = lax.top_k(imp, k_top)
        k_sel = ks_blocks[b_ix, g_ix, sel]
        v_sel = vs_blocks[b_ix, g_ix, sel]
        tok = sel[..., None] * SEL_LEN + jnp.arange(SEL_LEN)
        m_s = (tok <= t[:, None, None]).reshape(b_, NSA_GROUPS, 1, qbk, k_top * SEL_LEN)
        s_s = jnp.einsum('bqghd,bgqkld->bghqkl', qb, k_sel).reshape(
            b_, NSA_GROUPS, NSA_HPG, qbk, k_top * SEL_LEN)
        p_s = masked_softmax(s_s, m_s).reshape(b_, NSA_GROUPS, NSA_HPG, qbk, k_top, SEL_LEN)
        o_s = jnp.einsum('bghqkl,bgqkld->bqghd', p_s.astype(v_sel.dtype), v_sel)
        kw = lax.dynamic_slice_in_dim(kw_pad, s0, qbk + WINDOW, 1)
        vw = lax.dynamic_slice_in_dim(vw_pad, s0, qbk + WINDOW, 1)
        u = s0 - WINDOW + jnp.arange(qbk + WINDOW)
        dist = t[:, None] - u[None, :]
        m_w = (dist >= 0) & (dist < WINDOW) & (u[None, :] >= 0)
        s_w = jnp.einsum('bqghd,bkgd->bghqk', qb, kw)
        p_w = masked_softmax(s_w, m_w)
        o_w = jnp.einsum('bghqk,bkgd->bqghd', p_w.astype(vw.dtype), vw)
        gb = lax.dynamic_slice_in_dim(gates, s0, qbk, 1).reshape(b_, qbk, NSA_GROUPS, NSA_HPG, 3)
        o = gb[..., 0:1] * o_c + gb[..., 1:2] * o_s + gb[..., 2:3] * o_w
        return o.reshape(b_, qbk, NSA_HEADS * NSA_DH)

    out = lax.map(block, jnp.arange(s_ // qbk) * qbk)
    return out.transpose(1, 0, 2, 3).reshape(b_, s_, NSA_HEADS * NSA_DH)


def swiglu(h, w1, w3, w2):
    return (jax.nn.silu(h @ w1) * (h @ w3)) @ w2


def moe_ffn(h, w_router, w1, w3, w2):
    logits = (h @ w_router).astype(jnp.float32)
    top_val, top_idx = lax.top_k(logits, TOP_K)
    top_w = jax.nn.softmax(top_val, -1)
    combine = jnp.sum(jax.nn.one_hot(top_idx, N_EXPERTS, dtype=jnp.float32) * top_w[..., None], -2)
    out = jnp.zeros_like(h)
    for e in range(N_EXPERTS):
        out = out + combine[..., e:e + 1].astype(h.dtype) * swiglu(h, w1[e], w3[e], w2[e])
    return out


def setup_inputs(seed: int = 0) -> dict:
    key = jax.random.key(seed)
    keys = iter(jax.random.split(key, 40))

    def nrm(shape, scale):
        return jax.random.normal(next(keys), shape, jnp.float32) * scale

    L, D = DEPTH, D_MODEL
    n_dense, n_moe = (DEPTH + 1) // 2, DEPTH // 2
    x = nrm((BATCH, SEQ, D), 1.0)
    c = nrm((BATCH, D), 1.0)
    positions = (jnp.arange(SEQ, dtype=jnp.int32)[None, :] +
                 jax.random.randint(next(keys), (BATCH, 1), 0, 1024, dtype=jnp.int32))
    w_ada = nrm((L, D, 6 * D), 0.5 * D ** -0.5)
    b_ada = nrm((L, 6 * D), 0.01)
    norm_mix = 1.0 + nrm((L, D), 0.05)
    norm_ffn = 1.0 + nrm((L, D), 0.05)
    w_in = nrm((L, D, P_IN), D ** -0.5)
    conv_w = nrm((L, CONV_W, DN_QKV), CONV_W ** -0.5)
    a_log = jnp.log(jax.random.uniform(next(keys), (L, DN_HEADS), jnp.float32, 1.0, 16.0))
    dt = jnp.exp(jax.random.uniform(next(keys), (L, DN_HEADS), jnp.float32,
                                    math.log(1e-3), math.log(1e-1)))
    dt_bias = dt + jnp.log(-jnp.expm1(-dt))
    dn_norm = 1.0 + nrm((L, DN_DV), 0.05)
    cmp_pos = nrm((L, 2, CMP_LEN, NSA_DH), 0.1)
    w_cmp1 = nrm((L, 2, CMP_LEN * NSA_DH, NSA_DH), (CMP_LEN * NSA_DH) ** -0.5)
    w_cmp2 = nrm((L, 2, NSA_DH, NSA_DH), NSA_DH ** -0.5)
    q_norm = 1.0 + nrm((L, NSA_DH), 0.05)
    k_norm = 1.0 + nrm((L, 3, NSA_DH), 0.05)
    w_oa = nrm((L, DN_HEADS * DN_DV, D), (DN_HEADS * DN_DV) ** -0.5)
    w_ob = nrm((L, NSA_HEADS * NSA_DH, D), (NSA_HEADS * NSA_DH) ** -0.5)
    w_out = nrm((L, D, D), D ** -0.5)
    w1_dense = nrm((n_dense, D, FFN_DIM), D ** -0.5)
    w3_dense = nrm((n_dense, D, FFN_DIM), D ** -0.5)
    w2_dense = nrm((n_dense, FFN_DIM, D), FFN_DIM ** -0.5)
    w_router = nrm((n_moe, D, N_EXPERTS), D ** -0.5)
    w1_moe = nrm((n_moe, N_EXPERTS, D, FFN_DIM), D ** -0.5)
    w3_moe = nrm((n_moe, N_EXPERTS, D, FFN_DIM), D ** -0.5)
    w2_moe = nrm((n_moe, N_EXPERTS, FFN_DIM, D), FFN_DIM ** -0.5)
    return {'x': x, 'c': c, 'positions': positions, 'w_ada': w_ada, 'b_ada': b_ada,
            'norm_mix': norm_mix, 'norm_ffn': norm_ffn, 'w_in': w_in, 'conv_w': conv_w,
            'a_log': a_log, 'dt_bias': dt_bias, 'dn_norm': dn_norm, 'cmp_pos': cmp_pos,
            'w_cmp1': w_cmp1, 'w_cmp2': w_cmp2, 'q_norm': q_norm, 'k_norm': k_norm,
            'w_oa': w_oa, 'w_ob': w_ob, 'w_out': w_out, 'w1_dense': w1_dense,
            'w3_dense': w3_dense, 'w2_dense': w2_dense, 'w_router': w_router,
            'w1_moe': w1_moe, 'w3_moe': w3_moe, 'w2_moe': w2_moe}


def reference(x, c, positions, w_ada, b_ada, norm_mix, norm_ffn, w_in, conv_w, a_log, dt_bias,
              dn_norm, cmp_pos, w_cmp1, w_cmp2, q_norm, k_norm, w_oa, w_ob, w_out,
              w1_dense, w3_dense, w2_dense, w_router, w1_moe, w3_moe, w2_moe):
    b_, s_, _ = x.shape
    gq, dh = NSA_GROUPS, NSA_DH
    cos, sin = rope_angles(positions)
    n_cmp = (s_ - CMP_LEN) // CMP_STRIDE + 1
    cmp_end = jnp.arange(n_cmp) * CMP_STRIDE + CMP_LEN - 1
    cos_c, sin_c = rope_angles(positions[:, cmp_end])
    split_at = [int(i) for i in np.cumsum(SPLIT_SIZES)[:-1]]
    c_act = jax.nn.silu(c)
    for l in range(DEPTH):
        mod = c_act @ w_ada[l] + b_ada[l]
        sh_m, sc_m, g_m, sh_f, sc_f, g_f = [m[:, None, :] for m in jnp.split(mod, 6, -1)]
        h = rms_norm(x, norm_mix[l]) * (1.0 + sc_m) + sh_m
        proj = h @ w_in[l]
        (dn_qkv, dn_z, dn_b, dn_a, n_q, c_k, c_v, s_k, s_v, w_k, w_v, n_g, m_g) = \
            jnp.split(proj, split_at, -1)
        qkv = causal_conv_silu(dn_qkv, conv_w[l])
        d_q, d_k, d_v = jnp.split(qkv, [DN_HEADS * DN_DK, 2 * DN_HEADS * DN_DK], -1)
        d_q = l2_norm(d_q.reshape(b_, s_, DN_HEADS, DN_DK)) * (DN_DK ** -0.5)
        d_k = l2_norm(d_k.reshape(b_, s_, DN_HEADS, DN_DK))
        d_v = d_v.reshape(b_, s_, DN_HEADS, DN_DV)
        beta = jax.nn.sigmoid(dn_b)
        g_log = -jnp.exp(a_log[l].astype(jnp.float32)) * jax.nn.softplus(
            dn_a.astype(jnp.float32) + dt_bias[l].astype(jnp.float32))
        o_a = gated_delta_rule(d_q, d_k, d_v, beta, g_log)
        o_a = rms_norm(o_a, dn_norm[l]) * jax.nn.silu(dn_z.reshape(b_, s_, DN_HEADS, DN_DV))
        y_a = o_a.reshape(b_, s_, DN_HEADS * DN_DV) @ w_oa[l]
        n_q = apply_rope(rms_norm(n_q.reshape(b_, s_, NSA_HEADS, dh), q_norm[l]), cos, sin)
        k_c = compress_blocks(c_k.reshape(b_, s_, gq, dh), cmp_pos[l, 0], w_cmp1[l, 0], w_cmp2[l, 0])
        k_c = apply_rope(rms_norm(k_c, k_norm[l, 0]), cos_c, sin_c)
        v_c = compress_blocks(c_v.reshape(b_, s_, gq, dh), cmp_pos[l, 1], w_cmp1[l, 1], w_cmp2[l, 1])
        s_k = apply_rope(rms_norm(s_k.reshape(b_, s_, gq, dh), k_norm[l, 1]), cos, sin)
        w_k = apply_rope(rms_norm(w_k.reshape(b_, s_, gq, dh), k_norm[l, 2]), cos, sin)
        gates = jax.nn.sigmoid(n_g.reshape(b_, s_, NSA_HEADS, 3))
        o_b = nsa_core(n_q, k_c, v_c, s_k, s_v.reshape(b_, s_, gq, dh),
                       w_k, w_v.reshape(b_, s_, gq, dh), gates)
        y_b = o_b @ w_ob[l]
        gate_a, gate_b = jnp.split(jax.nn.sigmoid(m_g), 2, -1)
        y = (gate_a * y_a + gate_b * y_b) @ w_out[l]
        x = x + g_m * y
        h = rms_norm(x, norm_ffn[l]) * (1.0 + sc_f) + sh_f
        if l % 2 == 0:
            f = swiglu(h, w1_dense[l // 2], w3_dense[l // 2], w2_dense[l // 2])
        else:
            f = moe_ffn(h, w_router[l // 2], w1_moe[l // 2], w3_moe[l // 2], w2_moe[l // 2])
        x = x + g_f * f
    return x
```

```python
import functools

import jax
import jax.numpy as jnp
import numpy as np
from jax import lax
from jax.experimental import pallas as pl
from jax.experimental.pallas import tpu as pltpu

F32 = jnp.float32
BF16 = jnp.bfloat16

DN_HEADS = 8
DN_DK = 64
DN_DV = 64
DN_CHUNK = 64
CONV_W = 4
DN_QKV = DN_HEADS * (2 * DN_DK + DN_DV)
NSA_HEADS = 8
NSA_GROUPS = 2
NSA_HPG = NSA_HEADS // NSA_GROUPS
NSA_DH = 64
CMP_LEN = 32
CMP_STRIDE = 16
SEL_LEN = 64
SEL_TOP = 16
WINDOW = 512
ROPE_THETA = 10000.0
N_EXPERTS = 8
EPS = 1e-6
NEG = -1e30
FORCE = 1e6

LANES = 128
VMEM_LIMIT = 56 * 1024 * 1024


def _sigmoid(x):
    return 1.0 / (1.0 + jnp.exp(-x))


def _silu(x):
    return x * _sigmoid(x)


def _softplus(x):
    return jnp.maximum(x, 0.0) + jnp.log(1.0 + jnp.exp(-jnp.abs(x)))


def _dot(a, b):
    return jnp.dot(a, b, preferred_element_type=F32)


def _dot_nt(a, b):
    return lax.dot_general(a, b, (((1,), (1,)), ((), ())), preferred_element_type=F32)


def _dot_tn(a, b):
    return lax.dot_general(a, b, (((0,), (0,)), ((), ())), preferred_element_type=F32)


def _norm_mod(x, nw, sc, sh):
    y = x * lax.rsqrt(jnp.mean(x * x, -1, keepdims=True) + EPS) * nw
    return y * (1.0 + sc) + sh


def _params(sem):
    return pltpu.CompilerParams(dimension_semantics=sem, vmem_limit_bytes=VMEM_LIMIT)


def _mod_body(c_ref, w_ref, b_ref, o_ref):
    c = c_ref[...]
    o_ref[0] = _dot(_silu(c).astype(BF16), w_ref[0].astype(BF16)) + b_ref[0]


def _ada_mod(c, w_ada, b_ada):
    n_layers, d, n = w_ada.shape
    b = c.shape[0]
    tn = n // 4
    return pl.pallas_call(
        _mod_body,
        grid=(n_layers, n // tn),
        in_specs=[pl.BlockSpec((b, d), lambda l, j: (0, 0)),
                  pl.BlockSpec((1, d, tn), lambda l, j: (l, 0, j)),
                  pl.BlockSpec((1, 1, tn), lambda l, j: (l, 0, j))],
        out_specs=pl.BlockSpec((1, b, tn), lambda l, j: (l, 0, j)),
        out_shape=jax.ShapeDtypeStruct((n_layers, b, n), F32),
        compiler_params=_params(("parallel", "parallel")),
        name="ada_mod",
    )(c, w_ada, b_ada.reshape(n_layers, 1, n))


def _nm_mm_body(x_ref, nw_ref, sc_ref, sh_ref, w_ref, o_ref, h_ref):
    @pl.when(pl.program_id(2) == 0)
    def _():
        h_ref[...] = _norm_mod(x_ref[0], nw_ref[...], sc_ref[0], sh_ref[0]).astype(BF16)

    o_ref[0] = _dot(h_ref[...], w_ref[...]).astype(o_ref.dtype)


def _norm_mod_matmul(x, nw, sc, sh, w, out_dtype, tm, tn):
    b, s, d = x.shape
    n = w.shape[1]
    return pl.pallas_call(
        _nm_mm_body,
        grid=(b, s // tm, n // tn),
        in_specs=[pl.BlockSpec((1, tm, d), lambda bi, i, j: (bi, i, 0)),
                  pl.BlockSpec((1, d), lambda bi, i, j: (0, 0)),
                  pl.BlockSpec((1, 1, d), lambda bi, i, j: (bi, 0, 0)),
                  pl.BlockSpec((1, 1, d), lambda bi, i, j: (bi, 0, 0)),
                  pl.BlockSpec((d, tn), lambda bi, i, j: (0, j))],
        out_specs=pl.BlockSpec((1, tm, tn), lambda bi, i, j: (bi, i, j)),
        out_shape=jax.ShapeDtypeStruct((b, s, n), out_dtype),
        scratch_shapes=[pltpu.VMEM((tm, d), BF16)],
        compiler_params=_params(("parallel", "parallel", "arbitrary")),
        name="in_proj",
    )(x, nw, sc, sh, w)


def _unit_lower_inverse(lower, eye):
    m = -lower
    p = eye + m
    for _ in range(5):
        m = _dot(m, m)
        p = p + _dot(m, p)
    return p


def _dn_body(x_ref, sm_ref, at_ref, cw_ref, alog_ref, dtb_ref, alogt_ref, dtbt_ref, dnw_ref, o_ref,
             buf_ref, act_ref, gcn_ref, beta_ref, gct_ref, state_ref, *, ts):
    nc = ts // DN_CHUNK
    c64 = DN_CHUNK

    @pl.when(pl.program_id(1) == 0)
    def _():
        buf_ref[0:8, :] = jnp.zeros((8, DN_QKV), F32)
        state_ref[...] = jnp.zeros_like(state_ref)

    for sl in range(DN_QKV // LANES):
        cols = slice(sl * LANES, (sl + 1) * LANES)
        buf_ref[8:ts + 8, cols] = x_ref[0, :, cols].astype(F32)
        y = cw_ref[0:1, cols] * buf_ref[5:5 + ts, cols]
        for j in range(1, CONV_W):
            y = y + cw_ref[j:j + 1, cols] * buf_ref[5 + j:5 + j + ts, cols]
        buf_ref[0:8, cols] = buf_ref[ts:ts + 8, cols]
        act_ref[:, :, cols] = _silu(y).reshape(nc, c64, LANES)

    sm = sm_ref[0]
    beta_ref[...] = _sigmoid(sm).reshape(nc, c64, LANES)
    g = -jnp.exp(alog_ref[...]) * _softplus(sm + dtb_ref[...])
    row = lax.broadcasted_iota(jnp.int32, (ts, LANES), 0) & (c64 - 1)
    for sft in (1, 2, 4, 8, 16, 32):
        g = g + jnp.where(row >= sft, pltpu.roll(g, sft, 0), 0.0)
    gcn_ref[...] = g.reshape(nc, c64, LANES)
    gt = -jnp.exp(alogt_ref[...]) * _softplus(at_ref[0] + dtbt_ref[...])
    lane = lax.broadcasted_iota(jnp.int32, (DN_HEADS, ts), 1) & (c64 - 1)
    for sft in (1, 2, 4, 8, 16, 32):
        gt = gt + jnp.where(lane >= sft, pltpu.roll(gt, sft, 1), 0.0)
    for c in range(nc):
        gct_ref[c] = gt[:, c * c64:(c + 1) * c64]

    ri = lax.broadcasted_iota(jnp.int32, (c64, c64), 0)
    ci = lax.broadcasted_iota(jnp.int32, (c64, c64), 1)
    tril = ri >= ci
    strict = ri > ci
    eye = jnp.where(ri == ci, 1.0, 0.0).astype(F32)
    dnw = dnw_ref[...]

    def chunk(c, carry):
        gcn = gcn_ref[c]
        bet = beta_ref[c]
        gct = gct_ref[c]
        r0 = pl.multiple_of(c * c64, c64)
        for h in range(DN_HEADS):
            q = act_ref[c, :, h * DN_DK:(h + 1) * DN_DK]
            k = act_ref[c, :, (DN_HEADS + h) * DN_DK:(DN_HEADS + h + 1) * DN_DK]
            v = act_ref[c, :, 2 * DN_HEADS * DN_DK + h * DN_DV:2 * DN_HEADS * DN_DK + (h + 1) * DN_DV]
            q = q * lax.rsqrt(jnp.sum(q * q, -1, keepdims=True) + EPS) * (DN_DK ** -0.5)
            k = k * lax.rsqrt(jnp.sum(k * k, -1, keepdims=True) + EPS)
            bcol = bet[:, h:h + 1]
            gcol = gcn[:, DN_HEADS + h:DN_HEADS + h + 1]
            grow = gct[h:h + 1, :]
            decay = jnp.where(tril, jnp.exp(jnp.where(tril, gcol - grow, 0.0)), 0.0)
            eg = jnp.exp(gcol)
            glast = gcol[c64 - 1:c64, :]
            ekd = jnp.exp(glast - gcol)
            egl = jnp.exp(glast)
            kb = k * bcol
            k16 = k.astype(BF16)
            kk = _dot_nt(kb.astype(BF16), k16)
            qk = _dot_nt(q.astype(BF16), k16)
            lower = jnp.where(strict, kk * decay, 0.0)
            a_intra = jnp.where(tril, qk * decay, 0.0)
            tinv = _unit_lower_inverse(lower, eye).astype(BF16)
            u = _dot(tinv, (v * bcol).astype(BF16))
            w = _dot(tinv, (kb * eg).astype(BF16))
            st = state_ref[h]
            st16 = st.astype(BF16)
            v_new = u - _dot(w.astype(BF16), st16)
            vn16 = v_new.astype(BF16)
            o = _dot((q * eg).astype(BF16), st16) + _dot(a_intra.astype(BF16), vn16)
            state_ref[h] = st * egl + _dot_tn((k * ekd).astype(BF16), vn16)
            on = o * lax.rsqrt(jnp.mean(o * o, -1, keepdims=True) + EPS) * dnw
            z = x_ref[0, pl.ds(r0, c64), DN_QKV + h * DN_DV:DN_QKV + (h + 1) * DN_DV].astype(F32)
            o_ref[0, pl.ds(r0, c64), h * DN_DV:(h + 1) * DN_DV] = (on * _silu(z)).astype(o_ref.dtype)
        return carry

    lax.fori_loop(0, nc, chunk, 0)


def _deltanet(proj, small, a_t, conv_w, a_log, dt_bias, dn_norm, ts):
    b, s, _ = proj.shape
    wdn = DN_QKV + DN_HEADS * DN_DV
    pad = jnp.zeros((LANES - 2 * DN_HEADS,), F32)
    alog_row = jnp.concatenate([jnp.zeros((DN_HEADS,), F32), a_log, pad]).reshape(1, LANES)
    dtb_row = jnp.concatenate([jnp.zeros((DN_HEADS,), F32), dt_bias, pad]).reshape(1, LANES)
    nc = ts // DN_CHUNK
    return pl.pallas_call(
        functools.partial(_dn_body, ts=ts),
        grid=(b, s // ts),
        in_specs=[pl.BlockSpec((1, ts, wdn), lambda bi, i: (bi, i, 0)),
                  pl.BlockSpec((1, ts, LANES), lambda bi, i: (bi, i, 0)),
                  pl.BlockSpec((1, DN_HEADS, ts), lambda bi, i: (bi, 0, i)),
                  pl.BlockSpec((CONV_W, DN_QKV), lambda bi, i: (0, 0)),
                  pl.BlockSpec((1, LANES), lambda bi, i: (0, 0)),
                  pl.BlockSpec((1, LANES), lambda bi, i: (0, 0)),
                  pl.BlockSpec((DN_HEADS, 1), lambda bi, i: (0, 0)),
                  pl.BlockSpec((DN_HEADS, 1), lambda bi, i: (0, 0)),
                  pl.BlockSpec((1, DN_DV), lambda bi, i: (0, 0))],
        out_specs=pl.BlockSpec((1, ts, DN_HEADS * DN_DV), lambda bi, i: (bi, i, 0)),
        out_shape=jax.ShapeDtypeStruct((b, s, DN_HEADS * DN_DV), BF16),
        scratch_shapes=[pltpu.VMEM((ts + 8, DN_QKV), F32),
                        pltpu.VMEM((nc, DN_CHUNK, DN_QKV), F32),
                        pltpu.VMEM((nc, DN_CHUNK, LANES), F32),
                        pltpu.VMEM((nc, DN_CHUNK, LANES), F32),
                        pltpu.VMEM((nc, DN_HEADS, DN_CHUNK), F32),
                        pltpu.VMEM((DN_HEADS, DN_DK, DN_DV), F32)],
        compiler_params=_params(("parallel", "arbitrary")),
        name="deltanet",
    )(proj, small, a_t, conv_w, alog_row, dtb_row, a_log.reshape(DN_HEADS, 1),
      dt_bias.reshape(DN_HEADS, 1), dn_norm.reshape(1, DN_DV))


def _seg_ones():
    r = lax.broadcasted_iota(jnp.int32, (LANES, LANES), 0) // NSA_DH
    c = lax.broadcasted_iota(jnp.int32, (LANES, LANES), 1) // NSA_DH
    return jnp.where(r == c, 1.0, 0.0).astype(F32)


def _norm_rope(x, w, cos_f, sin_s, seg):
    ms = _dot(x * x, seg) * (1.0 / NSA_DH)
    y = x * lax.rsqrt(ms + EPS) * w
    half = NSA_DH // 2
    lane = lax.broadcasted_iota(jnp.int32, y.shape, 1) & (NSA_DH - 1)
    partner = jnp.where(lane < half, pltpu.roll(y, LANES - half, 1), pltpu.roll(y, half, 1))
    return y * cos_f + partner * sin_s


def _nsa_prep_body(q_ref, ck_ref, cv_ref, sk_ref, sv_ref, wk_ref, wv_ref, cos_ref, sin_ref,
                   qw_ref, skw_ref, wkw_ref,
                   qo_ref, sko_ref, svo_ref, wko_ref, wvo_ref, cko_ref, cvo_ref):
    seg = _seg_ones()
    cos_f = cos_ref[0]
    sin_s = sin_ref[0]
    for sl in range(NSA_HEADS * NSA_DH // LANES):
        x = q_ref[0, :, sl * LANES:(sl + 1) * LANES].astype(F32)
        y = (_norm_rope(x, qw_ref[...], cos_f, sin_s, seg) * (NSA_DH ** -0.5)).astype(BF16)
        for half in range(2):
            h = 2 * sl + half
            qo_ref[0, h // NSA_HPG, h % NSA_HPG] = y[:, half * NSA_DH:(half + 1) * NSA_DH]
    sk = _norm_rope(sk_ref[0].astype(F32), skw_ref[...], cos_f, sin_s, seg).astype(BF16)
    wk = _norm_rope(wk_ref[0].astype(F32), wkw_ref[...], cos_f, sin_s, seg).astype(BF16)
    sv = sv_ref[0]
    wv = wv_ref[0]
    for g in range(NSA_GROUPS):
        cols = slice(g * NSA_DH, (g + 1) * NSA_DH)
        sko_ref[0, g] = sk[:, cols]
        wko_ref[0, g] = wk[:, cols]
        svo_ref[0, g] = sv[:, cols]
        wvo_ref[0, g] = wv[:, cols]
    cko_ref[0] = ck_ref[0]
    cvo_ref[0] = cv_ref[0]


def _nsa_prep(proj, col0, cos_f, sin_s, q_norm, k_norm_s, k_norm_w, ts):
    b, s, _ = proj.shape
    qw = NSA_HEADS * NSA_DH
    qblk = col0 // qw
    k0 = (col0 + qw) // LANES

    def kspec(i):
        return pl.BlockSpec((1, ts, LANES), lambda bi, t, i=i: (bi, t, k0 + i))

    tile2 = lambda w: jnp.tile(w.reshape(1, NSA_DH), (1, LANES // NSA_DH))
    gshape = jax.ShapeDtypeStruct((b, NSA_GROUPS, s, NSA_DH), BF16)
    gspec = pl.BlockSpec((1, NSA_GROUPS, ts, NSA_DH), lambda bi, t: (bi, 0, t, 0))
    cspec = pl.BlockSpec((1, ts, LANES), lambda bi, t: (bi, t, 0))
    wspec = pl.BlockSpec((1, LANES), lambda bi, t: (0, 0))
    return pl.pallas_call(
        _nsa_prep_body,
        grid=(b, s // ts),
        in_specs=[pl.BlockSpec((1, ts, qw), lambda bi, t: (bi, t, qblk)),
                  kspec(0), kspec(1), kspec(2), kspec(3), kspec(4), kspec(5),
                  cspec, cspec, wspec, wspec, wspec],
        out_specs=[pl.BlockSpec((1, NSA_GROUPS, NSA_HPG, ts, NSA_DH), lambda bi, t: (bi, 0, 0, t, 0)),
                   gspec, gspec, gspec, gspec, cspec, cspec],
        out_shape=[jax.ShapeDtypeStruct((b, NSA_GROUPS, NSA_HPG, s, NSA_DH), BF16),
                   gshape, gshape, gshape, gshape,
                   jax.ShapeDtypeStruct((b, s, LANES), BF16),
                   jax.ShapeDtypeStruct((b, s, LANES), BF16)],
        compiler_params=_params(("parallel", "parallel")),
        name="nsa_prep",
    )(proj, proj, proj, proj, proj, proj, proj, cos_f, sin_s,
      tile2(q_norm), tile2(k_norm_s), tile2(k_norm_w))


def _compress_body(ck_ref, cv_ref, pos_ref, w1a_ref, w1b_ref, w2_ref, kw_ref, cos_ref, sin_ref,
                   ko_ref, vo_ref):
    n = ck_ref.shape[1]
    outs = []
    for which, x_ref in enumerate((ck_ref, cv_ref)):
        x = x_ref[0].astype(F32)
        lo = _dot((x + pos_ref[which, 0:1, :]).astype(BF16), w1a_ref[which])
        hi = _dot((x + pos_ref[which, 1:2, :]).astype(BF16), w1b_ref[which])
        h1 = _silu(lo + pltpu.roll(hi, n - 1, 0))
        outs.append(_dot(h1.astype(BF16), w2_ref[which]))
    kc = _norm_rope(outs[0], kw_ref[...], cos_ref[0], sin_ref[0], _seg_ones()).astype(BF16)
    vc = outs[1].astype(BF16)
    for g in range(NSA_GROUPS):
        ko_ref[0, g] = kc[:, g * NSA_DH:(g + 1) * NSA_DH]
        vo_ref[0, g] = vc[:, g * NSA_DH:(g + 1) * NSA_DH]


def _compress(ck, cv, cmp_pos, w_cmp1, w_cmp2, k_norm_c, cos_c, sin_c):
    b, s, _ = ck.shape
    n = s // CMP_STRIDE
    width = CMP_STRIDE * LANES
    per_row = CMP_LEN // CMP_STRIDE
    eye_g = jnp.eye(NSA_GROUPS, dtype=F32)
    w1 = w_cmp1.reshape(2, per_row, CMP_STRIDE, NSA_DH, NSA_DH)
    w1 = jnp.einsum('khldo,gG->khlgdGo', w1, eye_g).reshape(2, per_row, width, LANES).astype(BF16)
    w2 = jnp.einsum('kdo,gG->kgdGo', w_cmp2, eye_g).reshape(2, LANES, LANES).astype(BF16)
    pos = jnp.broadcast_to(cmp_pos.reshape(2, per_row, CMP_STRIDE, 1, NSA_DH),
                           (2, per_row, CMP_STRIDE, NSA_GROUPS, NSA_DH)).reshape(2, per_row, width)
    kw = jnp.tile(k_norm_c.reshape(1, NSA_DH), (1, NSA_GROUPS))
    full = lambda shp: pl.BlockSpec(shp, lambda bi: (0,) * len(shp))
    bspec = pl.BlockSpec((1, n, width), lambda bi: (bi, 0, 0))
    tspec = pl.BlockSpec((1, n, LANES), lambda bi: (bi, 0, 0))
    ospec = pl.BlockSpec((1, NSA_GROUPS, n, NSA_DH), lambda bi: (bi, 0, 0, 0))
    oshape = jax.ShapeDtypeStruct((b, NSA_GROUPS, n, NSA_DH), BF16)
    return pl.pallas_call(
        _compress_body,
        grid=(b,),
        in_specs=[bspec, bspec, full((2, per_row, width)), full((2, width, LANES)),
                  full((2, width, LANES)), full((2, LANES, LANES)), full((1, LANES)), tspec, tspec],
        out_specs=[ospec, ospec],
        out_shape=[oshape, oshape],
        compiler_params=_params(("parallel",)),
        name="nsa_compress",
    )(ck.reshape(b, n, width), cv.reshape(b, n, width), pos, w1[:, 0], w1[:, 1], w2, kw, cos_c, sin_c)


def _masked_softmax(s, mask):
    s = jnp.where(mask, s, NEG)
    m = jnp.max(s, -1, keepdims=True)
    e = jnp.where(mask, jnp.exp(s - m), 0.0)
    return e / jnp.maximum(jnp.sum(e, -1, keepdims=True), 1e-30)


def _nsa_body(q_ref, kc_ref, vc_ref, ks_ref, vs_ref, kw_ref, vw_ref, gt_ref, o_ref,
              m_ref, l_ref, acc_ref, *, tq, tk, n_sel):
    hp = NSA_HPG
    rows = hp * tq
    t0 = pl.program_id(2) * tq
    q = q_ref[0, 0].reshape(rows, NSA_DH)
    t_q = t0 + lax.broadcasted_iota(jnp.int32, (tq, 1), 0)

    kc = kc_ref[0, 0]
    n_cmp = kc.shape[0]
    s_c = _dot_nt(q, kc).reshape(hp, tq, n_cmp)
    cmp_start = lax.broadcasted_iota(jnp.int32, (1, n_cmp), 1) * CMP_STRIDE
    p_c = _masked_softmax(s_c, (cmp_start + (CMP_LEN - 1) <= t_q)[None])
    p_c16 = p_c.astype(BF16).reshape(rows, n_cmp)
    o_c = _dot(p_c16, vc_ref[0, 0]).reshape(hp, tq, NSA_DH)

    nb = ((n_sel + LANES - 1) // LANES) * LANES
    cs = lax.broadcasted_iota(jnp.int32, (n_cmp, nb), 0) * CMP_STRIDE
    bs = lax.broadcasted_iota(jnp.int32, (n_cmp, nb), 1) * SEL_LEN
    overlap = jnp.where((cs < bs + SEL_LEN) & (cs + CMP_LEN > bs), 1.0, 0.0).astype(BF16)
    imp = jnp.sum(_dot(p_c16, overlap).reshape(hp, tq, nb), axis=0)
    jb = lax.broadcasted_iota(jnp.int32, (1, nb), 1)
    cur = t_q >> 6
    forced = (jb == 0) | (jb == cur) | (jb == cur - 1)
    imp = jnp.where(forced, FORCE, jnp.where(jb * SEL_LEN <= t_q, imp, -FORCE))
    imp = jnp.where(jb < n_sel, imp, -jnp.inf)
    jbf = jb.astype(F32)
    sel = jnp.zeros((tq, nb), F32)
    for _ in range(min(SEL_TOP, n_sel)):
        mx = jnp.max(imp, -1, keepdims=True)
        first = jnp.min(jnp.where(imp == mx, jbf, float(nb)), -1, keepdims=True)
        hit = jbf == first
        sel = jnp.where(hit, 1.0, sel)
        imp = jnp.where(hit, -jnp.inf, imp)
    sel16 = sel.astype(BF16)

    m_ref[...] = jnp.full(m_ref.shape, NEG, F32)
    l_ref[...] = jnp.zeros(l_ref.shape, F32)
    acc_ref[...] = jnp.zeros(acc_ref.shape, F32)
    bpt = tk // SEL_LEN

    def key_tile(kt, carry):
        k0 = pl.multiple_of(kt * tk, tk)
        s = _dot_nt(q, ks_ref[0, 0, pl.ds(k0, tk), :]).reshape(hp, tq, tk)
        blk = kt * bpt + (lax.broadcasted_iota(jnp.int32, (nb, tk), 1) >> 6)
        expand = jnp.where(lax.broadcasted_iota(jnp.int32, (nb, tk), 0) == blk, 1.0, 0.0).astype(BF16)
        tok = k0 + lax.broadcasted_iota(jnp.int32, (1, tk), 1)
        mask = ((_dot(sel16, expand) > 0.5) & (tok <= t_q))[None]
        s = jnp.where(mask, s, NEG)
        m_prev = m_ref[...]
        m_new = jnp.maximum(m_prev, jnp.max(s, -1, keepdims=True))
        alpha = jnp.exp(m_prev - m_new)
        e = jnp.where(mask, jnp.exp(s - m_new), 0.0)
        l_ref[...] = alpha * l_ref[...] + jnp.sum(e, -1, keepdims=True)
        pv = _dot(e.astype(BF16).reshape(rows, tk), vs_ref[0, 0, pl.ds(k0, tk), :])
        acc_ref[...] = alpha * acc_ref[...] + pv.reshape(hp, tq, NSA_DH)
        m_ref[...] = m_new
        return carry

    lax.fori_loop(0, (t0 + tq + tk - 1) // tk, key_tile, 0)
    o_s = acc_ref[...] / jnp.maximum(l_ref[...], 1e-30)

    wl = WINDOW + tq
    w0 = pl.multiple_of(jnp.maximum(t0 - WINDOW, 0), tq)
    s_w = _dot_nt(q, kw_ref[0, 0, pl.ds(w0, wl), :]).reshape(hp, tq, wl)
    dist = t_q - (w0 + lax.broadcasted_iota(jnp.int32, (1, wl), 1))
    p_w = _masked_softmax(s_w, ((dist >= 0) & (dist < WINDOW))[None])
    o_w = _dot(p_w.astype(BF16).reshape(rows, wl), vw_ref[0, 0, pl.ds(w0, wl), :]).reshape(hp, tq, NSA_DH)

    gates = _sigmoid(gt_ref[0, 0])
    for j in range(hp):
        o = (gates[:, 3 * j:3 * j + 1] * o_c[j] + gates[:, 3 * j + 1:3 * j + 2] * o_s[j]
             + gates[:, 3 * j + 2:3 * j + 3] * o_w[j])
        o_ref[0, :, j * NSA_DH:(j + 1) * NSA_DH] = o.astype(o_ref.dtype)


def _nsa_attention(q, kc, vc, ks, vs, kw, vw, gates, tq, tk):
    b, g, hp, s, dh = q.shape
    n_cmp = kc.shape[2]
    assert s >= WINDOW + tq and WINDOW % tq == 0 and s % tk == 0 and tk % SEL_LEN == 0
    seq_spec = pl.BlockSpec((1, 1, s, dh), lambda bi, gi, i: (bi, gi, 0, 0))
    cmp_spec = pl.BlockSpec((1, 1, n_cmp, dh), lambda bi, gi, i: (bi, gi, 0, 0))
    return pl.pallas_call(
        functools.partial(_nsa_body, tq=tq, tk=tk, n_sel=s // SEL_LEN),
        grid=(b, g, s // tq),
        in_specs=[pl.BlockSpec((1, 1, hp, tq, dh), lambda bi, gi, i: (bi, gi, 0, i, 0)),
                  cmp_spec, cmp_spec, seq_spec, seq_spec, seq_spec, seq_spec,
                  pl.BlockSpec((1, 1, tq, 3 * hp), lambda bi, gi, i: (bi, gi, i, 0))],
        out_specs=pl.BlockSpec((1, tq, hp * dh), lambda bi, gi, i: (bi, i, gi)),
        out_shape=jax.ShapeDtypeStruct((b, s, g * hp * dh), BF16),
        scratch_shapes=[pltpu.VMEM((hp, tq, 1), F32), pltpu.VMEM((hp, tq, 1), F32),
                        pltpu.VMEM((hp, tq, dh), F32)],
        compiler_params=_params(("parallel", "parallel", "arbitrary")),
        name="nsa_attention",
    )(q, kc, vc, ks, vs, kw, vw, gates)


def _mix_out_body(x_ref, oa_ref, ob_ref, mg_ref, gm_ref, woa_ref, wob_ref, wout_ref, o_ref):
    d = x_ref.shape[2]
    y_a = _dot(oa_ref[0], woa_ref[...])
    y_b = _dot(ob_ref[0], wob_ref[...])
    merged = (_sigmoid(mg_ref[0, :, 0:d].astype(F32)) * y_a
              + _sigmoid(mg_ref[0, :, d:2 * d].astype(F32)) * y_b)
    o_ref[0] = x_ref[0] + gm_ref[0] * _dot(merged.astype(BF16), wout_ref[...])


def _mix_out(x, o_a, o_b, proj, mg_blk, g_m, w_oa, w_ob, w_out, tm):
    b, s, d = x.shape
    full = lambda a: pl.BlockSpec(a.shape, lambda bi, i: (0, 0))
    return pl.pallas_call(
        _mix_out_body,
        grid=(b, s // tm),
        in_specs=[pl.BlockSpec((1, tm, d), lambda bi, i: (bi, i, 0)),
                  pl.BlockSpec((1, tm, o_a.shape[2]), lambda bi, i: (bi, i, 0)),
                  pl.BlockSpec((1, tm, o_b.shape[2]), lambda bi, i: (bi, i, 0)),
                  pl.BlockSpec((1, tm, 2 * d), lambda bi, i: (bi, i, mg_blk)),
                  pl.BlockSpec((1, 1, d), lambda bi, i: (bi, 0, 0)),
                  full(w_oa), full(w_ob), full(w_out)],
        out_specs=pl.BlockSpec((1, tm, d), lambda bi, i: (bi, i, 0)),
        out_shape=jax.ShapeDtypeStruct(x.shape, F32),
        compiler_params=_params(("parallel", "parallel")),
        name="mix_out",
    )(x, o_a, o_b, proj, g_m, w_oa, w_ob, w_out)


def _ffn_body(x_ref, nw_ref, sc_ref, sh_ref, gf_ref, w1_ref, w3_ref, w2_ref, o_ref, h_ref, acc_ref):
    f = pl.program_id(2)

    @pl.when(f == 0)
    def _():
        h_ref[...] = _norm_mod(x_ref[0], nw_ref[...], sc_ref[0], sh_ref[0]).astype(BF16)
        acc_ref[...] = jnp.zeros_like(acc_ref)

    h = h_ref[...]
    t = _silu(_dot(h, w1_ref[...])) * _dot(h, w3_ref[...])
    acc_ref[...] += _dot(t.astype(BF16), w2_ref[...])

    @pl.when(f == pl.num_programs(2) - 1)
    def _():
        o_ref[0] = x_ref[0] + gf_ref[0] * acc_ref[...]


def _dense_ffn(x, nw, sc, sh, g_f, w1, w3, w2, tm, tf):
    b, s, d = x.shape
    ff = w1.shape[1]
    vec = pl.BlockSpec((1, 1, d), lambda bi, i, f: (bi, 0, 0))
    return pl.pallas_call(
        _ffn_body,
        grid=(b, s // tm, ff // tf),
        in_specs=[pl.BlockSpec((1, tm, d), lambda bi, i, f: (bi, i, 0)),
                  pl.BlockSpec((1, d), lambda bi, i, f: (0, 0)), vec, vec, vec,
                  pl.BlockSpec((d, tf), lambda bi, i, f: (0, f)),
                  pl.BlockSpec((d, tf), lambda bi, i, f: (0, f)),
                  pl.BlockSpec((tf, d), lambda bi, i, f: (f, 0))],
        out_specs=pl.BlockSpec((1, tm, d), lambda bi, i, f: (bi, i, 0)),
        out_shape=jax.ShapeDtypeStruct(x.shape, F32),
        scratch_shapes=[pltpu.VMEM((tm, d), BF16), pltpu.VMEM((tm, d), F32)],
        compiler_params=_params(("parallel", "parallel", "arbitrary")),
        name="dense_ffn",
    )(x, nw, sc, sh, g_f, w1, w3, w2)


def _moe_body(x_ref, nw_ref, sc_ref, sh_ref, gf_ref, wr_ref, w1_ref, w3_ref, w2_ref, o_ref,
              h_ref, acc_ref, comb_ref):
    e = pl.program_id(2)
    f = pl.program_id(3)
    lane = lax.broadcasted_iota(jnp.int32, comb_ref.shape, 1)

    @pl.when((e == 0) & (f == 0))
    def _():
        h16 = _norm_mod(x_ref[0], nw_ref[...], sc_ref[0], sh_ref[0]).astype(BF16)
        h_ref[...] = h16
        acc_ref[...] = jnp.zeros_like(acc_ref)
        lanef = lane.astype(F32)
        logits = jnp.where(lane < N_EXPERTS, _dot(h16, wr_ref[...]), -jnp.inf)
        m1 = jnp.max(logits, -1, keepdims=True)
        i1 = jnp.min(jnp.where(logits == m1, lanef, float(LANES)), -1, keepdims=True)
        rest = jnp.where(lanef == i1, -jnp.inf, logits)
        m2 = jnp.max(rest, -1, keepdims=True)
        i2 = jnp.min(jnp.where(rest == m2, lanef, float(LANES)), -1, keepdims=True)
        e2 = jnp.exp(m2 - m1)
        comb_ref[...] = (jnp.where(lanef == i1, 1.0 / (1.0 + e2), 0.0)
                         + jnp.where(lanef == i2, e2 / (1.0 + e2), 0.0))

    cw = jnp.sum(jnp.where(lane == e, comb_ref[...], 0.0), -1, keepdims=True)
    h = h_ref[...]
    t = _silu(_dot(h, w1_ref[0])) * _dot(h, w3_ref[0]) * cw
    acc_ref[...] += _dot(t.astype(BF16), w2_ref[0])

    @pl.when((e == pl.num_programs(2) - 1) & (f == pl.num_programs(3) - 1))
    def _():
        o_ref[0] = x_ref[0] + gf_ref[0] * acc_ref[...]


def _moe_ffn(x, nw, sc, sh, g_f, w_router, w1, w3, w2, tm, tf):
    b, s, d = x.shape
    n_e, _, ff = w1.shape
    wr = jnp.pad(w_router, ((0, 0), (0, LANES - n_e))).astype(BF16)
    vec = pl.BlockSpec((1, 1, d), lambda bi, i, e, f: (bi, 0, 0))
    return pl.pallas_call(
        _moe_body,
        grid=(b, s // tm, n_e, ff // tf),
        in_specs=[pl.BlockSpec((1, tm, d), lambda bi, i, e, f: (bi, i, 0)),
                  pl.BlockSpec((1, d), lambda bi, i, e, f: (0, 0)), vec, vec, vec,
                  pl.BlockSpec((d, LANES), lambda bi, i, e, f: (0, 0)),
                  pl.BlockSpec((1, d, tf), lambda bi, i, e, f: (e, 0, f)),
                  pl.BlockSpec((1, d, tf), lambda bi, i, e, f: (e, 0, f)),
                  pl.BlockSpec((1, tf, d), lambda bi, i, e, f: (e, f, 0))],
        out_specs=pl.BlockSpec((1, tm, d), lambda bi, i, e, f: (bi, i, 0)),
        out_shape=jax.ShapeDtypeStruct(x.shape, F32),
        scratch_shapes=[pltpu.VMEM((tm, d), BF16), pltpu.VMEM((tm, d), F32),
                        pltpu.VMEM((tm, LANES), F32)],
        compiler_params=_params(("parallel", "parallel", "arbitrary", "arbitrary")),
        name="moe_ffn",
    )(x, nw, sc, sh, g_f, wr, w1, w3, w2)


def _rope_tables(pos):
    inv = 1.0 / (ROPE_THETA ** (jnp.arange(0, NSA_DH, 2, dtype=F32) / NSA_DH))
    ang = pos.astype(F32)[..., None] * inv
    cos, sin = jnp.cos(ang), jnp.sin(ang)
    reps = LANES // NSA_DH
    return (jnp.tile(jnp.concatenate([cos, cos], -1), (1, 1, reps)),
            jnp.tile(jnp.concatenate([-sin, sin], -1), (1, 1, reps)))


_SPLITS = (DN_QKV, DN_HEADS * DN_DV, DN_HEADS, DN_HEADS, NSA_HEADS * NSA_DH) + (NSA_GROUPS * NSA_DH,) * 6
_OFF = np.concatenate([[0], np.cumsum(_SPLITS)])
_OFF_NG = int(_OFF[-1])
_OFF_MG = _OFF_NG + 3 * NSA_HEADS


def kernel(x, c, positions, w_ada, b_ada, norm_mix, norm_ffn, w_in, conv_w, a_log, dt_bias, dn_norm, cmp_pos, w_cmp1, w_cmp2, q_norm, k_norm, w_oa, w_ob, w_out, w1_dense, w3_dense, w2_dense, w_router, w1_moe, w3_moe, w2_moe):
    b, s, d = x.shape
    depth = w_in.shape[0]
    wdn = DN_QKV + DN_HEADS * DN_DV
    n_small = 2 * DN_HEADS + 3 * NSA_HEADS

    cos_f, sin_s = _rope_tables(positions)
    n_cmp_pad = s // CMP_STRIDE
    cmp_end = jnp.minimum(jnp.arange(n_cmp_pad) * CMP_STRIDE + CMP_LEN - 1, s - 1)
    cos_c, sin_c = _rope_tables(positions[:, cmp_end])

    mod = _ada_mod(c, w_ada, b_ada)

    off_nq = int(_OFF[4])
    w_main = jnp.concatenate([w_in[:, :, 0:wdn], w_in[:, :, _OFF_MG:_OFF_MG + 2 * d],
                              w_in[:, :, off_nq:_OFF_NG]], -1).astype(BF16)
    w_small = jnp.concatenate([w_in[:, :, wdn:wdn + 2 * DN_HEADS], w_in[:, :, _OFF_NG:_OFF_MG],
                               jnp.zeros((depth, d, LANES - n_small), F32)], -1).astype(BF16)
    nsa_col0 = wdn + 2 * d
    n_main = w_main.shape[2]

    w_oa16, w_ob16, w_out16 = w_oa.astype(BF16), w_ob.astype(BF16), w_out.astype(BF16)
    w1d, w3d, w2d = w1_dense.astype(BF16), w3_dense.astype(BF16), w2_dense.astype(BF16)
    w1m, w3m, w2m = w1_moe.astype(BF16), w3_moe.astype(BF16), w2_moe.astype(BF16)

    for l in range(depth):
        sh_m, sc_m, g_m, sh_f, sc_f, g_f = [m.reshape(b, 1, d) for m in jnp.split(mod[l], 6, -1)]
        nw_m = norm_mix[l].reshape(1, d)
        proj = _norm_mod_matmul(x, nw_m, sc_m, sh_m, w_main[l], BF16, tm=1024, tn=n_main // 3)
        small = _norm_mod_matmul(x, nw_m, sc_m, sh_m, w_small[l], F32, tm=1024, tn=LANES)
        a_t = jnp.swapaxes(small[:, :, DN_HEADS:2 * DN_HEADS], 1, 2)
        o_a = _deltanet(proj, small, a_t, conv_w[l], a_log[l], dt_bias[l], dn_norm[l], ts=512)
        qn, ksn, vs, kwn, vw, ck, cv = _nsa_prep(proj, nsa_col0, cos_f, sin_s, q_norm[l],
                                                 k_norm[l, 1], k_norm[l, 2], ts=512)
        kc, vc = _compress(ck, cv, cmp_pos[l], w_cmp1[l], w_cmp2[l], k_norm[l, 0], cos_c, sin_c)
        gates = small[:, :, 2 * DN_HEADS:n_small].reshape(b, s, NSA_GROUPS, 3 * NSA_HPG)
        o_b = _nsa_attention(qn, kc, vc, ksn, vs, kwn, vw, jnp.swapaxes(gates, 1, 2), tq=128, tk=512)
        x = _mix_out(x, o_a, o_b, proj, wdn // (2 * d), g_m, w_oa16[l], w_ob16[l], w_out16[l], tm=512)
        nw_f = norm_ffn[l].reshape(1, d)
        if l % 2 == 0:
            x = _dense_ffn(x, nw_f, sc_f, sh_f, g_f, w1d[l // 2], w3d[l // 2], w2d[l // 2], tm=1024, tf=512)
        else:
            x = _moe_ffn(x, nw_f, sc_f, sh_f, g_f, w_router[l // 2], w1m[l // 2], w3m[l // 2],
                         w2m[l // 2], tm=1024, tf=512)
    return x
```

```python
import functools

import jax
import jax.numpy as jnp
import numpy as np
from jax import lax
from jax.experimental import pallas as pl
from jax.experimental.pallas import tpu as pltpu

F32 = jnp.float32
BF16 = jnp.bfloat16

DN_HEADS = 8
DN_DK = 64
DN_DV = 64
DN_CHUNK = 64
CONV_W = 4
DN_QKV = DN_HEADS * (2 * DN_DK + DN_DV)
NSA_HEADS = 8
NSA_GROUPS = 2
NSA_HPG = NSA_HEADS // NSA_GROUPS
NSA_DH = 64
CMP_LEN = 32
CMP_STRIDE = 16
SEL_LEN = 64
SEL_SHIFT = 6
SEL_TOP = 16
WINDOW = 512
ROPE_THETA = 10000.0
N_EXPERTS = 8
EPS = 1e-6
NEG = -1e30
FORCE = 1e6
SEL_BIAS = 1e30

LANES = 128
VMEM_LIMIT = 56 * 1024 * 1024


def _sigmoid(x):
    return 1.0 / (1.0 + jnp.exp(-x))


def _silu(x):
    return x * _sigmoid(x)


def _softplus(x):
    return jnp.maximum(x, 0.0) + jnp.log(1.0 + jnp.exp(-jnp.abs(x)))


def _dot(a, b):
    return jnp.dot(a, b, preferred_element_type=F32)


def _dot_nt(a, b):
    return lax.dot_general(a, b, (((1,), (1,)), ((), ())), preferred_element_type=F32)


def _dot_tn(a, b):
    return lax.dot_general(a, b, (((0,), (0,)), ((), ())), preferred_element_type=F32)


def _norm_mod(x, nw, sc, sh):
    y = x * lax.rsqrt(jnp.mean(x * x, -1, keepdims=True) + EPS) * nw
    return y * (1.0 + sc) + sh


def _params(sem):
    return pltpu.CompilerParams(dimension_semantics=sem, vmem_limit_bytes=VMEM_LIMIT)


def _mod_body(c_ref, w_ref, b_ref, o_ref):
    c = c_ref[...]
    o_ref[0] = _dot(_silu(c).astype(BF16), w_ref[0].astype(BF16)) + b_ref[0]


def _ada_mod(c, w_ada, b_ada):
    n_layers, d, n = w_ada.shape
    b = c.shape[0]
    tn = n // 4
    return pl.pallas_call(
        _mod_body,
        grid=(n_layers, n // tn),
        in_specs=[pl.BlockSpec((b, d), lambda l, j: (0, 0)),
                  pl.BlockSpec((1, d, tn), lambda l, j: (l, 0, j)),
                  pl.BlockSpec((1, 1, tn), lambda l, j: (l, 0, j))],
        out_specs=pl.BlockSpec((1, b, tn), lambda l, j: (l, 0, j)),
        out_shape=jax.ShapeDtypeStruct((n_layers, b, n), F32),
        compiler_params=_params(("parallel", "parallel")),
        name="ada_mod",
    )(c, w_ada, b_ada.reshape(n_layers, 1, n))


def _nm_mm_body(x_ref, nw_ref, sc_ref, sh_ref, w_ref, o_ref, h_ref):
    @pl.when(pl.program_id(2) == 0)
    def _():
        h_ref[...] = _norm_mod(x_ref[0], nw_ref[...], sc_ref[0], sh_ref[0]).astype(BF16)

    o_ref[0] = _dot(h_ref[...], w_ref[...]).astype(o_ref.dtype)


def _norm_mod_matmul(x, nw, sc, sh, w, out_dtype, tm, tn):
    b, s, d = x.shape
    n = w.shape[1]
    return pl.pallas_call(
        _nm_mm_body,
        grid=(b, s // tm, n // tn),
        in_specs=[pl.BlockSpec((1, tm, d), lambda bi, i, j: (bi, i, 0)),
                  pl.BlockSpec((1, d), lambda bi, i, j: (0, 0)),
                  pl.BlockSpec((1, 1, d), lambda bi, i, j: (bi, 0, 0)),
                  pl.BlockSpec((1, 1, d), lambda bi, i, j: (bi, 0, 0)),
                  pl.BlockSpec((d, tn), lambda bi, i, j: (0, j))],
        out_specs=pl.BlockSpec((1, tm, tn), lambda bi, i, j: (bi, i, j)),
        out_shape=jax.ShapeDtypeStruct((b, s, n), out_dtype),
        scratch_shapes=[pltpu.VMEM((tm, d), BF16)],
        compiler_params=_params(("parallel", "parallel", "arbitrary")),
        name="in_proj",
    )(x, nw, sc, sh, w)


def _dn_body(x_ref, sm_ref, at_ref, cw_ref, alog_ref, dtb_ref, alogt_ref, dtbt_ref, dnw_ref, o_ref,
             buf_ref, act_ref, gcn_ref, beta_ref, gct_ref, state_ref, *, ts):
    nc = ts // DN_CHUNK
    c64 = DN_CHUNK

    @pl.when(pl.program_id(1) == 0)
    def _():
        buf_ref[0:8, :] = jnp.zeros((8, DN_QKV), F32)
        state_ref[...] = jnp.zeros_like(state_ref)

    for sl in range(DN_QKV // LANES):
        cols = slice(sl * LANES, (sl + 1) * LANES)
        buf_ref[8:ts + 8, cols] = x_ref[0, :, cols].astype(F32)
        y = cw_ref[0:1, cols] * buf_ref[5:5 + ts, cols]
        for j in range(1, CONV_W):
            y = y + cw_ref[j:j + 1, cols] * buf_ref[5 + j:5 + j + ts, cols]
        buf_ref[0:8, cols] = buf_ref[ts:ts + 8, cols]
        act_ref[:, :, cols] = _silu(y).reshape(nc, c64, LANES)

    sm = sm_ref[0]
    beta_ref[...] = _sigmoid(sm).reshape(nc, c64, LANES)
    g = -jnp.exp(alog_ref[...]) * _softplus(sm + dtb_ref[...])
    row = lax.broadcasted_iota(jnp.int32, (ts, LANES), 0) & (c64 - 1)
    for sft in (1, 2, 4, 8, 16, 32):
        g = g + jnp.where(row >= sft, pltpu.roll(g, sft, 0), 0.0)
    gcn_ref[...] = g.reshape(nc, c64, LANES)
    gt = -jnp.exp(alogt_ref[...]) * _softplus(at_ref[0] + dtbt_ref[...])
    lane = lax.broadcasted_iota(jnp.int32, (DN_HEADS, ts), 1) & (c64 - 1)
    for sft in (1, 2, 4, 8, 16, 32):
        gt = gt + jnp.where(lane >= sft, pltpu.roll(gt, sft, 1), 0.0)
    for c in range(nc):
        gct_ref[c] = gt[:, c * c64:(c + 1) * c64]

    ri = lax.broadcasted_iota(jnp.int32, (c64, c64), 0)
    ci = lax.broadcasted_iota(jnp.int32, (c64, c64), 1)
    tril = ri >= ci
    strict = ri > ci
    eye = jnp.where(ri == ci, 1.0, 0.0).astype(F32)
    dnw = dnw_ref[...]

    def chunk(c, carry):
        gcn = gcn_ref[c]
        bet = beta_ref[c]
        gct = gct_ref[c]
        r0 = pl.multiple_of(c * c64, c64)
        hs = range(DN_HEADS)
        q = [act_ref[c, :, h * DN_DK:(h + 1) * DN_DK] for h in hs]
        k = [act_ref[c, :, (DN_HEADS + h) * DN_DK:(DN_HEADS + h + 1) * DN_DK] for h in hs]
        v = [act_ref[c, :, 2 * DN_HEADS * DN_DK + h * DN_DV:2 * DN_HEADS * DN_DK + (h + 1) * DN_DV]
             for h in hs]
        q = [x * lax.rsqrt(jnp.sum(x * x, -1, keepdims=True) + EPS) * (DN_DK ** -0.5) for x in q]
        k = [x * lax.rsqrt(jnp.sum(x * x, -1, keepdims=True) + EPS) for x in k]
        bcol = [bet[:, h:h + 1] for h in hs]
        gcol = [gcn[:, DN_HEADS + h:DN_HEADS + h + 1] for h in hs]
        decay = [jnp.where(tril, jnp.exp(jnp.where(tril, gcol[h] - gct[h:h + 1, :], 0.0)), 0.0) for h in hs]
        eg = [jnp.exp(x) for x in gcol]
        glast = [x[c64 - 1:c64, :] for x in gcol]
        kb = [k[h] * bcol[h] for h in hs]
        k16 = [x.astype(BF16) for x in k]
        kk = [_dot_nt(kb[h].astype(BF16), k16[h]) for h in hs]
        qk = [_dot_nt(q[h].astype(BF16), k16[h]) for h in hs]
        a16 = [jnp.where(tril, qk[h] * decay[h], 0.0).astype(BF16) for h in hs]
        m = [jnp.where(strict, -(kk[h] * decay[h]), 0.0) for h in hs]
        p = [eye + x for x in m]
        for _ in range(5):
            m = [_dot(x, x) for x in m]
            p = [p[h] + _dot(m[h], p[h]) for h in hs]
        tinv = [x.astype(BF16) for x in p]
        u = [_dot(tinv[h], (v[h] * bcol[h]).astype(BF16)) for h in hs]
        w = [_dot(tinv[h], (kb[h] * eg[h]).astype(BF16)) for h in hs]
        st = [state_ref[h] for h in hs]
        st16 = [x.astype(BF16) for x in st]
        ws = [_dot(w[h].astype(BF16), st16[h]) for h in hs]
        vn16 = [(u[h] - ws[h]).astype(BF16) for h in hs]
        qs = [_dot((q[h] * eg[h]).astype(BF16), st16[h]) for h in hs]
        av = [_dot(a16[h], vn16[h]) for h in hs]
        kv = [_dot_tn((k[h] * jnp.exp(glast[h] - gcol[h])).astype(BF16), vn16[h]) for h in hs]
        for h in hs:
            state_ref[h] = st[h] * jnp.exp(glast[h]) + kv[h]
            o = qs[h] + av[h]
            on = o * lax.rsqrt(jnp.mean(o * o, -1, keepdims=True) + EPS) * dnw
            z = x_ref[0, pl.ds(r0, c64), DN_QKV + h * DN_DV:DN_QKV + (h + 1) * DN_DV].astype(F32)
            o_ref[0, pl.ds(r0, c64), h * DN_DV:(h + 1) * DN_DV] = (on * _silu(z)).astype(o_ref.dtype)
        return carry

    lax.fori_loop(0, nc, chunk, 0)


def _deltanet(proj, small, a_t, conv_w, a_log, dt_bias, dn_norm, ts):
    b, s, _ = proj.shape
    wdn = DN_QKV + DN_HEADS * DN_DV
    pad = jnp.zeros((LANES - 2 * DN_HEADS,), F32)
    alog_row = jnp.concatenate([jnp.zeros((DN_HEADS,), F32), a_log, pad]).reshape(1, LANES)
    dtb_row = jnp.concatenate([jnp.zeros((DN_HEADS,), F32), dt_bias, pad]).reshape(1, LANES)
    nc = ts // DN_CHUNK
    return pl.pallas_call(
        functools.partial(_dn_body, ts=ts),
        grid=(b, s // ts),
        in_specs=[pl.BlockSpec((1, ts, wdn), lambda bi, i: (bi, i, 0)),
                  pl.BlockSpec((1, ts, LANES), lambda bi, i: (bi, i, 0)),
                  pl.BlockSpec((1, DN_HEADS, ts), lambda bi, i: (bi, 0, i)),
                  pl.BlockSpec((CONV_W, DN_QKV), lambda bi, i: (0, 0)),
                  pl.BlockSpec((1, LANES), lambda bi, i: (0, 0)),
                  pl.BlockSpec((1, LANES), lambda bi, i: (0, 0)),
                  pl.BlockSpec((DN_HEADS, 1), lambda bi, i: (0, 0)),
                  pl.BlockSpec((DN_HEADS, 1), lambda bi, i: (0, 0)),
                  pl.BlockSpec((1, DN_DV), lambda bi, i: (0, 0))],
        out_specs=pl.BlockSpec((1, ts, DN_HEADS * DN_DV), lambda bi, i: (bi, i, 0)),
        out_shape=jax.ShapeDtypeStruct((b, s, DN_HEADS * DN_DV), BF16),
        scratch_shapes=[pltpu.VMEM((ts + 8, DN_QKV), F32),
                        pltpu.VMEM((nc, DN_CHUNK, DN_QKV), F32),
                        pltpu.VMEM((nc, DN_CHUNK, LANES), F32),
                        pltpu.VMEM((nc, DN_CHUNK, LANES), F32),
                        pltpu.VMEM((nc, DN_HEADS, DN_CHUNK), F32),
                        pltpu.VMEM((DN_HEADS, DN_DK, DN_DV), F32)],
        compiler_params=_params(("parallel", "arbitrary")),
        name="deltanet",
    )(proj, small, a_t, conv_w, alog_row, dtb_row, a_log.reshape(DN_HEADS, 1),
      dt_bias.reshape(DN_HEADS, 1), dn_norm.reshape(1, DN_DV))


def _seg_ones():
    r = lax.broadcasted_iota(jnp.int32, (LANES, LANES), 0) // NSA_DH
    c = lax.broadcasted_iota(jnp.int32, (LANES, LANES), 1) // NSA_DH
    return jnp.where(r == c, 1.0, 0.0).astype(F32)


def _norm_rope(x, w, cos_f, sin_s, seg):
    ms = _dot(x * x, seg) * (1.0 / NSA_DH)
    y = x * lax.rsqrt(ms + EPS) * w
    half = NSA_DH // 2
    lane = lax.broadcasted_iota(jnp.int32, y.shape, 1) & (NSA_DH - 1)
    partner = jnp.where(lane < half, pltpu.roll(y, LANES - half, 1), pltpu.roll(y, half, 1))
    return y * cos_f + partner * sin_s


def _nsa_prep_body(q_ref, ck_ref, cv_ref, sk_ref, sv_ref, wk_ref, wv_ref, cos_ref, sin_ref,
                   qw_ref, skw_ref, wkw_ref,
                   qo_ref, sko_ref, svo_ref, wko_ref, wvo_ref, cko_ref, cvo_ref):
    seg = _seg_ones()
    cos_f = cos_ref[0]
    sin_s = sin_ref[0]
    for sl in range(NSA_HEADS * NSA_DH // LANES):
        x = q_ref[0, :, sl * LANES:(sl + 1) * LANES].astype(F32)
        y = (_norm_rope(x, qw_ref[...], cos_f, sin_s, seg) * (NSA_DH ** -0.5)).astype(BF16)
        for half in range(2):
            h = 2 * sl + half
            qo_ref[0, h // NSA_HPG, h % NSA_HPG, :, 0:NSA_DH] = y[:, half * NSA_DH:(half + 1) * NSA_DH]
            qo_ref[0, h // NSA_HPG, h % NSA_HPG, :, NSA_DH:LANES] = jnp.zeros(
                (y.shape[0], LANES - NSA_DH), BF16)
    sk = _norm_rope(sk_ref[0].astype(F32), skw_ref[...], cos_f, sin_s, seg).astype(BF16)
    wk = _norm_rope(wk_ref[0].astype(F32), wkw_ref[...], cos_f, sin_s, seg).astype(BF16)
    sv = sv_ref[0]
    wv = wv_ref[0]
    ts = sk.shape[0]
    blk = (pl.program_id(1) * ts + lax.broadcasted_iota(jnp.int32, (ts, LANES), 0)) >> SEL_SHIFT
    onehot = jnp.where(lax.broadcasted_iota(jnp.int32, (ts, LANES), 1) == blk, SEL_BIAS, 0.0).astype(BF16)
    for g in range(NSA_GROUPS):
        cols = slice(g * NSA_DH, (g + 1) * NSA_DH)
        sko_ref[0, g, :, 0:LANES] = onehot
        sko_ref[0, g, :, LANES:LANES + NSA_DH] = sk[:, cols]
        sko_ref[0, g, :, LANES + NSA_DH:2 * LANES] = jnp.zeros((ts, LANES - NSA_DH), BF16)
        wko_ref[0, g] = wk[:, cols]
        svo_ref[0, g] = sv[:, cols]
        wvo_ref[0, g] = wv[:, cols]
    cko_ref[0] = ck_ref[0]
    cvo_ref[0] = cv_ref[0]


def _nsa_prep(proj, col0, cos_f, sin_s, q_norm, k_norm_s, k_norm_w, ts):
    b, s, _ = proj.shape
    assert s // SEL_LEN <= LANES
    qw = NSA_HEADS * NSA_DH
    qblk = col0 // qw
    k0 = (col0 + qw) // LANES

    def kspec(i):
        return pl.BlockSpec((1, ts, LANES), lambda bi, t, i=i: (bi, t, k0 + i))

    tile2 = lambda w: jnp.tile(w.reshape(1, NSA_DH), (1, LANES // NSA_DH))
    gshape = jax.ShapeDtypeStruct((b, NSA_GROUPS, s, NSA_DH), BF16)
    gspec = pl.BlockSpec((1, NSA_GROUPS, ts, NSA_DH), lambda bi, t: (bi, 0, t, 0))
    ashape = jax.ShapeDtypeStruct((b, NSA_GROUPS, s, 2 * LANES), BF16)
    aspec = pl.BlockSpec((1, NSA_GROUPS, ts, 2 * LANES), lambda bi, t: (bi, 0, t, 0))
    cspec = pl.BlockSpec((1, ts, LANES), lambda bi, t: (bi, t, 0))
    wspec = pl.BlockSpec((1, LANES), lambda bi, t: (0, 0))
    return pl.pallas_call(
        _nsa_prep_body,
        grid=(b, s // ts),
        in_specs=[pl.BlockSpec((1, ts, qw), lambda bi, t: (bi, t, qblk)),
                  kspec(0), kspec(1), kspec(2), kspec(3), kspec(4), kspec(5),
                  cspec, cspec, wspec, wspec, wspec],
        out_specs=[pl.BlockSpec((1, NSA_GROUPS, NSA_HPG, ts, LANES), lambda bi, t: (bi, 0, 0, t, 0)),
                   aspec, gspec, gspec, gspec, cspec, cspec],
        out_shape=[jax.ShapeDtypeStruct((b, NSA_GROUPS, NSA_HPG, s, LANES), BF16),
                   ashape, gshape, gshape, gshape,
                   jax.ShapeDtypeStruct((b, s, LANES), BF16),
                   jax.ShapeDtypeStruct((b, s, LANES), BF16)],
        compiler_params=_params(("parallel", "parallel")),
        name="nsa_prep",
    )(proj, proj, proj, proj, proj, proj, proj, cos_f, sin_s,
      tile2(q_norm), tile2(k_norm_s), tile2(k_norm_w))


def _compress_body(ck_ref, cv_ref, pos_ref, w1a_ref, w1b_ref, w2_ref, kw_ref, cos_ref, sin_ref,
                   ko_ref, vo_ref):
    n = ck_ref.shape[1]
    outs = []
    for which, x_ref in enumerate((ck_ref, cv_ref)):
        x = x_ref[0].astype(F32)
        lo = _dot((x + pos_ref[which, 0:1, :]).astype(BF16), w1a_ref[which])
        hi = _dot((x + pos_ref[which, 1:2, :]).astype(BF16), w1b_ref[which])
        h1 = _silu(lo + pltpu.roll(hi, n - 1, 0))
        outs.append(_dot(h1.astype(BF16), w2_ref[which]))
    kc = _norm_rope(outs[0], kw_ref[...], cos_ref[0], sin_ref[0], _seg_ones()).astype(BF16)
    vc = outs[1].astype(BF16)
    for g in range(NSA_GROUPS):
        ko_ref[0, g] = kc[:, g * NSA_DH:(g + 1) * NSA_DH]
        vo_ref[0, g] = vc[:, g * NSA_DH:(g + 1) * NSA_DH]


def _compress(ck, cv, cmp_pos, w_cmp1, w_cmp2, k_norm_c, cos_c, sin_c):
    b, s, _ = ck.shape
    n = s // CMP_STRIDE
    width = CMP_STRIDE * LANES
    per_row = CMP_LEN // CMP_STRIDE
    eye_g = jnp.eye(NSA_GROUPS, dtype=F32)
    w1 = w_cmp1.reshape(2, per_row, CMP_STRIDE, NSA_DH, NSA_DH)
    w1 = jnp.einsum('khldo,gG->khlgdGo', w1, eye_g).reshape(2, per_row, width, LANES).astype(BF16)
    w2 = jnp.einsum('kdo,gG->kgdGo', w_cmp2, eye_g).reshape(2, LANES, LANES).astype(BF16)
    pos = jnp.broadcast_to(cmp_pos.reshape(2, per_row, CMP_STRIDE, 1, NSA_DH),
                           (2, per_row, CMP_STRIDE, NSA_GROUPS, NSA_DH)).reshape(2, per_row, width)
    kw = jnp.tile(k_norm_c.reshape(1, NSA_DH), (1, NSA_GROUPS))
    full = lambda shp: pl.BlockSpec(shp, lambda bi: (0,) * len(shp))
    bspec = pl.BlockSpec((1, n, width), lambda bi: (bi, 0, 0))
    tspec = pl.BlockSpec((1, n, LANES), lambda bi: (bi, 0, 0))
    ospec = pl.BlockSpec((1, NSA_GROUPS, n, NSA_DH), lambda bi: (bi, 0, 0, 0))
    oshape = jax.ShapeDtypeStruct((b, NSA_GROUPS, n, NSA_DH), BF16)
    return pl.pallas_call(
        _compress_body,
        grid=(b,),
        in_specs=[bspec, bspec, full((2, per_row, width)), full((2, width, LANES)),
                  full((2, width, LANES)), full((2, LANES, LANES)), full((1, LANES)), tspec, tspec],
        out_specs=[ospec, ospec],
        out_shape=[oshape, oshape],
        compiler_params=_params(("parallel",)),
        name="nsa_compress",
    )(ck.reshape(b, n, width), cv.reshape(b, n, width), pos, w1[:, 0], w1[:, 1], w2, kw, cos_c, sin_c)


def _nsa_body(q_ref, kc_ref, vc_ref, ksa_ref, vs_ref, kw_ref, vw_ref, gt_ref, o_ref,
              qa_ref, m_ref, lp_ref, acc_ref, *, tq, tk, n_sel):
    hp = NSA_HPG
    hs = range(hp)
    t0 = pl.program_id(2) * tq
    q = [q_ref[0, 0, j][:, 0:NSA_DH] for j in hs]
    t_q = t0 + lax.broadcasted_iota(jnp.int32, (tq, 1), 0)

    kc = kc_ref[0, 0]
    vc = vc_ref[0, 0]
    n_cmp = kc.shape[0]
    cmp_end = lax.broadcasted_iota(jnp.int32, (1, n_cmp), 1) * CMP_STRIDE + (CMP_LEN - 1)
    bias_c = jnp.where(cmp_end <= t_q, 0.0, NEG)
    valid_c = jnp.where(t_q >= CMP_LEN - 1, 1.0, 0.0)
    cs = lax.broadcasted_iota(jnp.int32, (LANES, n_cmp), 1) * CMP_STRIDE
    bs = lax.broadcasted_iota(jnp.int32, (LANES, n_cmp), 0) * SEL_LEN
    overlap_t = jnp.where((cs < bs + SEL_LEN) & (cs + CMP_LEN > bs), 1.0, 0.0).astype(BF16)
    s_c = [_dot_nt(q[j], kc) for j in hs]
    o_c = []
    imp = None
    for j in hs:
        sc = s_c[j] + bias_c
        e_c = jnp.exp(sc - jnp.max(sc, -1, keepdims=True))
        p16 = (e_c * (valid_c / jnp.sum(e_c, -1, keepdims=True))).astype(BF16)
        o_c.append(_dot(p16, vc))
        part = _dot_nt(overlap_t, p16)
        imp = part if imp is None else imp + part

    jb = lax.broadcasted_iota(jnp.int32, (LANES, 1), 0)
    t_row = t0 + lax.broadcasted_iota(jnp.int32, (1, tq), 1)
    cur = t_row >> SEL_SHIFT
    forced = (jb == 0) | (jb == cur) | (jb == cur - 1)
    imp = jnp.where(forced, FORCE, jnp.where(jb * SEL_LEN <= t_row, imp, -FORCE))
    imp = jnp.where(jb < n_sel, imp, -jnp.inf)
    jbf = jb.astype(F32)
    sel_t = jnp.zeros((LANES, tq), F32)
    for _ in range(min(SEL_TOP, n_sel)):
        mx = jnp.max(imp, 0, keepdims=True)
        first = jnp.min(jnp.where(imp == mx, jbf, float(LANES)), 0, keepdims=True)
        hit = jbf == first
        sel_t = jnp.where(hit, 1.0, sel_t)
        imp = jnp.where(hit, -jnp.inf, imp)
    ri = lax.broadcasted_iota(jnp.int32, (LANES, LANES), 0)
    ci = lax.broadcasted_iota(jnp.int32, (LANES, LANES), 1)
    sel = _dot_tn(sel_t.astype(BF16), jnp.where(ri == ci, 1.0, 0.0).astype(BF16))
    selm1 = (sel - 1.0).astype(BF16)
    for j in hs:
        qa_ref[j, :, 0:LANES] = selm1
        qa_ref[j, :, LANES:2 * LANES] = q_ref[0, 0, j]

    m_ref[...] = jnp.full(m_ref.shape, NEG, F32)
    lp_ref[...] = jnp.zeros(lp_ref.shape, F32)
    acc_ref[...] = jnp.zeros(acc_ref.shape, F32)

    def key_tile(kt, diagonal):
        k0 = pl.multiple_of(kt * tk, tk)
        k_aug = ksa_ref[0, 0, pl.ds(k0, tk), :]
        v = vs_ref[0, 0, pl.ds(k0, tk), :]
        s = [_dot_nt(qa_ref[j], k_aug) for j in hs]
        if diagonal:
            tok = k0 + lax.broadcasted_iota(jnp.int32, (1, tk), 1)
            bias = jnp.where(tok <= t_q, 0.0, NEG)
        for j in hs:
            sj = s[j] + bias if diagonal else s[j]
            m_prev = m_ref[j]
            m_new = jnp.maximum(m_prev, jnp.max(sj, -1, keepdims=True))
            alpha = jnp.exp(m_prev - m_new)
            e = jnp.exp(sj - m_new)
            part = e[:, 0:LANES]
            for i in range(1, tk // LANES):
                part = part + e[:, i * LANES:(i + 1) * LANES]
            lp_ref[j] = alpha * lp_ref[j] + part
            acc_ref[j] = alpha * acc_ref[j] + _dot(e.astype(BF16), v)
            m_ref[j] = m_new

    k_diag = t0 // tk

    def below_diagonal(kt, carry):
        key_tile(kt, False)
        return carry

    lax.fori_loop(0, k_diag, below_diagonal, 0)
    key_tile(k_diag, True)
    o_s = [acc_ref[j] / jnp.maximum(jnp.sum(lp_ref[j], -1, keepdims=True), 1e-30) for j in hs]

    wl = WINDOW + tq
    w0 = pl.multiple_of(jnp.maximum(t0 - WINDOW, 0), tq)
    dist = t_q - (w0 + lax.broadcasted_iota(jnp.int32, (1, wl), 1))
    bias_w = jnp.where((dist >= 0) & (dist < WINDOW), 0.0, NEG)
    k_w = kw_ref[0, 0, pl.ds(w0, wl), :]
    v_w = vw_ref[0, 0, pl.ds(w0, wl), :]
    s_w = [_dot_nt(q[j], k_w) for j in hs]
    o_w = []
    for j in hs:
        sw = s_w[j] + bias_w
        e_w = jnp.exp(sw - jnp.max(sw, -1, keepdims=True))
        o_w.append(_dot(e_w.astype(BF16), v_w) / jnp.sum(e_w, -1, keepdims=True))

    gates = _sigmoid(gt_ref[0, 0])
    for j in range(hp):
        o = (gates[:, 3 * j:3 * j + 1] * o_c[j] + gates[:, 3 * j + 1:3 * j + 2] * o_s[j]
             + gates[:, 3 * j + 2:3 * j + 3] * o_w[j])
        o_ref[0, :, j * NSA_DH:(j + 1) * NSA_DH] = o.astype(o_ref.dtype)


def _nsa_attention(q, kc, vc, ksa, vs, kw, vw, gates, tq, tk):
    b, g, hp, s, _ = q.shape
    dh = NSA_DH
    n_cmp = kc.shape[2]
    assert s >= WINDOW + tq and WINDOW % tq == 0 and s % tk == 0 and tk % tq == 0 and tk % LANES == 0
    seq_spec = pl.BlockSpec((1, 1, s, dh), lambda bi, gi, i: (bi, gi, 0, 0))
    cmp_spec = pl.BlockSpec((1, 1, n_cmp, dh), lambda bi, gi, i: (bi, gi, 0, 0))
    return pl.pallas_call(
        functools.partial(_nsa_body, tq=tq, tk=tk, n_sel=s // SEL_LEN),
        grid=(b, g, s // tq),
        in_specs=[pl.BlockSpec((1, 1, hp, tq, LANES), lambda bi, gi, i: (bi, gi, 0, i, 0)),
                  cmp_spec, cmp_spec,
                  pl.BlockSpec((1, 1, s, 2 * LANES), lambda bi, gi, i: (bi, gi, 0, 0)),
                  seq_spec, seq_spec, seq_spec,
                  pl.BlockSpec((1, 1, tq, 3 * hp), lambda bi, gi, i: (bi, gi, i, 0))],
        out_specs=pl.BlockSpec((1, tq, hp * dh), lambda bi, gi, i: (bi, i, gi)),
        out_shape=jax.ShapeDtypeStruct((b, s, g * hp * dh), BF16),
        scratch_shapes=[pltpu.VMEM((hp, tq, 2 * LANES), BF16), pltpu.VMEM((hp, tq, 1), F32),
                        pltpu.VMEM((hp, tq, LANES), F32), pltpu.VMEM((hp, tq, dh), F32)],
        compiler_params=_params(("parallel", "parallel", "arbitrary")),
        name="nsa_attention",
    )(q, kc, vc, ksa, vs, kw, vw, gates)


def _mix_out_body(x_ref, oa_ref, ob_ref, mg_ref, gm_ref, woa_ref, wob_ref, wout_ref, o_ref):
    d = x_ref.shape[2]
    y_a = _dot(oa_ref[0], woa_ref[...])
    y_b = _dot(ob_ref[0], wob_ref[...])
    merged = (_sigmoid(mg_ref[0, :, 0:d].astype(F32)) * y_a
              + _sigmoid(mg_ref[0, :, d:2 * d].astype(F32)) * y_b)
    o_ref[0] = x_ref[0] + gm_ref[0] * _dot(merged.astype(BF16), wout_ref[...])


def _mix_out(x, o_a, o_b, proj, mg_blk, g_m, w_oa, w_ob, w_out, tm):
    b, s, d = x.shape
    full = lambda a: pl.BlockSpec(a.shape, lambda bi, i: (0, 0))
    return pl.pallas_call(
        _mix_out_body,
        grid=(b, s // tm),
        in_specs=[pl.BlockSpec((1, tm, d), lambda bi, i: (bi, i, 0)),
                  pl.BlockSpec((1, tm, o_a.shape[2]), lambda bi, i: (bi, i, 0)),
                  pl.BlockSpec((1, tm, o_b.shape[2]), lambda bi, i: (bi, i, 0)),
                  pl.BlockSpec((1, tm, 2 * d), lambda bi, i: (bi, i, mg_blk)),
                  pl.BlockSpec((1, 1, d), lambda bi, i: (bi, 0, 0)),
                  full(w_oa), full(w_ob), full(w_out)],
        out_specs=pl.BlockSpec((1, tm, d), lambda bi, i: (bi, i, 0)),
        out_shape=jax.ShapeDtypeStruct(x.shape, F32),
        compiler_params=_params(("parallel", "parallel")),
        name="mix_out",
    )(x, o_a, o_b, proj, g_m, w_oa, w_ob, w_out)


def _ffn_body(x_ref, nw_ref, sc_ref, sh_ref, gf_ref, w1_ref, w3_ref, w2_ref, o_ref, h_ref, acc_ref):
    f = pl.program_id(2)

    @pl.when(f == 0)
    def _():
        h_ref[...] = _norm_mod(x_ref[0], nw_ref[...], sc_ref[0], sh_ref[0]).astype(BF16)
        acc_ref[...] = jnp.zeros_like(acc_ref)

    h = h_ref[...]
    t = _silu(_dot(h, w1_ref[...])) * _dot(h, w3_ref[...])
    acc_ref[...] += _dot(t.astype(BF16), w2_ref[...])

    @pl.when(f == pl.num_programs(2) - 1)
    def _():
        o_ref[0] = x_ref[0] + gf_ref[0] * acc_ref[...]


def _dense_ffn(x, nw, sc, sh, g_f, w1, w3, w2, tm, tf):
    b, s, d = x.shape
    ff = w1.shape[1]
    vec = pl.BlockSpec((1, 1, d), lambda bi, i, f: (bi, 0, 0))
    return pl.pallas_call(
        _ffn_body,
        grid=(b, s // tm, ff // tf),
        in_specs=[pl.BlockSpec((1, tm, d), lambda bi, i, f: (bi, i, 0)),
                  pl.BlockSpec((1, d), lambda bi, i, f: (0, 0)), vec, vec, vec,
                  pl.BlockSpec((d, tf), lambda bi, i, f: (0, f)),
                  pl.BlockSpec((d, tf), lambda bi, i, f: (0, f)),
                  pl.BlockSpec((tf, d), lambda bi, i, f: (f, 0))],
        out_specs=pl.BlockSpec((1, tm, d), lambda bi, i, f: (bi, i, 0)),
        out_shape=jax.ShapeDtypeStruct(x.shape, F32),
        scratch_shapes=[pltpu.VMEM((tm, d), BF16), pltpu.VMEM((tm, d), F32)],
        compiler_params=_params(("parallel", "parallel", "arbitrary")),
        name="dense_ffn",
    )(x, nw, sc, sh, g_f, w1, w3, w2)


def _moe_body(x_ref, nw_ref, sc_ref, sh_ref, gf_ref, wr_ref, w1_ref, w3_ref, w2_ref, o_ref,
              h_ref, acc_ref, comb_ref):
    e = pl.program_id(2)
    f = pl.program_id(3)
    lane = lax.broadcasted_iota(jnp.int32, comb_ref.shape, 1)

    @pl.when((e == 0) & (f == 0))
    def _():
        h16 = _norm_mod(x_ref[0], nw_ref[...], sc_ref[0], sh_ref[0]).astype(BF16)
        h_ref[...] = h16
        acc_ref[...] = jnp.zeros_like(acc_ref)
        lanef = lane.astype(F32)
        logits = jnp.where(lane < N_EXPERTS, _dot(h16, wr_ref[...]), -jnp.inf)
        m1 = jnp.max(logits, -1, keepdims=True)
        i1 = jnp.min(jnp.where(logits == m1, lanef, float(LANES)), -1, keepdims=True)
        rest = jnp.where(lanef == i1, -jnp.inf, logits)
        m2 = jnp.max(rest, -1, keepdims=True)
        i2 = jnp.min(jnp.where(rest == m2, lanef, float(LANES)), -1, keepdims=True)
        e2 = jnp.exp(m2 - m1)
        comb_ref[...] = (jnp.where(lanef == i1, 1.0 / (1.0 + e2), 0.0)
                         + jnp.where(lanef == i2, e2 / (1.0 + e2), 0.0))

    cw = jnp.sum(jnp.where(lane == e, comb_ref[...], 0.0), -1, keepdims=True)
    h = h_ref[...]
    t = _silu(_dot(h, w1_ref[0])) * _dot(h, w3_ref[0]) * cw
    acc_ref[...] += _dot(t.astype(BF16), w2_ref[0])

    @pl.when((e == pl.num_programs(2) - 1) & (f == pl.num_programs(3) - 1))
    def _():
        o_ref[0] = x_ref[0] + gf_ref[0] * acc_ref[...]


def _moe_ffn(x, nw, sc, sh, g_f, w_router, w1, w3, w2, tm, tf):
    b, s, d = x.shape
    n_e, _, ff = w1.shape
    wr = jnp.pad(w_router, ((0, 0), (0, LANES - n_e))).astype(BF16)
    vec = pl.BlockSpec((1, 1, d), lambda bi, i, e, f: (bi, 0, 0))
    return pl.pallas_call(
        _moe_body,
        grid=(b, s // tm, n_e, ff // tf),
        in_specs=[pl.BlockSpec((1, tm, d), lambda bi, i, e, f: (bi, i, 0)),
                  pl.BlockSpec((1, d), lambda bi, i, e, f: (0, 0)), vec, vec, vec,
                  pl.BlockSpec((d, LANES), lambda bi, i, e, f: (0, 0)),
                  pl.BlockSpec((1, d, tf), lambda bi, i, e, f: (e, 0, f)),
                  pl.BlockSpec((1, d, tf), lambda bi, i, e, f: (e, 0, f)),
                  pl.BlockSpec((1, tf, d), lambda bi, i, e, f: (e, f, 0))],
        out_specs=pl.BlockSpec((1, tm, d), lambda bi, i, e, f: (bi, i, 0)),
        out_shape=jax.ShapeDtypeStruct(x.shape, F32),
        scratch_shapes=[pltpu.VMEM((tm, d), BF16), pltpu.VMEM((tm, d), F32),
                        pltpu.VMEM((tm, LANES), F32)],
        compiler_params=_params(("parallel", "parallel", "arbitrary", "arbitrary")),
        name="moe_ffn",
    )(x, nw, sc, sh, g_f, wr, w1, w3, w2)


def _rope_tables(pos):
    inv = 1.0 / (ROPE_THETA ** (jnp.arange(0, NSA_DH, 2, dtype=F32) / NSA_DH))
    ang = pos.astype(F32)[..., None] * inv
    cos, sin = jnp.cos(ang), jnp.sin(ang)
    reps = LANES // NSA_DH
    return (jnp.tile(jnp.concatenate([cos, cos], -1), (1, 1, reps)),
            jnp.tile(jnp.concatenate([-sin, sin], -1), (1, 1, reps)))


_SPLITS = (DN_QKV, DN_HEADS * DN_DV, DN_HEADS, DN_HEADS, NSA_HEADS * NSA_DH) + (NSA_GROUPS * NSA_DH,) * 6
_OFF = np.concatenate([[0], np.cumsum(_SPLITS)])
_OFF_NG = int(_OFF[-1])
_OFF_MG = _OFF_NG + 3 * NSA_HEADS


def kernel(x, c, positions, w_ada, b_ada, norm_mix, norm_ffn, w_in, conv_w, a_log, dt_bias, dn_norm, cmp_pos, w_cmp1, w_cmp2, q_norm, k_norm, w_oa, w_ob, w_out, w1_dense, w3_dense, w2_dense, w_router, w1_moe, w3_moe, w2_moe):
    b, s, d = x.shape
    depth = w_in.shape[0]
    wdn = DN_QKV + DN_HEADS * DN_DV
    n_small = 2 * DN_HEADS + 3 * NSA_HEADS

    cos_f, sin_s = _rope_tables(positions)
    n_cmp_pad = s // CMP_STRIDE
    cmp_end = jnp.minimum(jnp.arange(n_cmp_pad) * CMP_STRIDE + CMP_LEN - 1, s - 1)
    cos_c, sin_c = _rope_tables(positions[:, cmp_end])

    mod = _ada_mod(c, w_ada, b_ada)

    off_nq = int(_OFF[4])
    w_main = jnp.concatenate([w_in[:, :, 0:wdn], w_in[:, :, _OFF_MG:_OFF_MG + 2 * d],
                              w_in[:, :, off_nq:_OFF_NG]], -1).astype(BF16)
    w_small = jnp.concatenate([w_in[:, :, wdn:wdn + 2 * DN_HEADS], w_in[:, :, _OFF_NG:_OFF_MG],
                               jnp.zeros((depth, d, LANES - n_small), F32)], -1).astype(BF16)
    nsa_col0 = wdn + 2 * d
    n_main = w_main.shape[2]

    w_oa16, w_ob16, w_out16 = w_oa.astype(BF16), w_ob.astype(BF16), w_out.astype(BF16)
    w1d, w3d, w2d = w1_dense.astype(BF16), w3_dense.astype(BF16), w2_dense.astype(BF16)
    w1m, w3m, w2m = w1_moe.astype(BF16), w3_moe.astype(BF16), w2_moe.astype(BF16)

    for l in range(depth):
        sh_m, sc_m, g_m, sh_f, sc_f, g_f = [m.reshape(b, 1, d) for m in jnp.split(mod[l], 6, -1)]
        nw_m = norm_mix[l].reshape(1, d)
        proj = _norm_mod_matmul(x, nw_m, sc_m, sh_m, w_main[l], BF16, tm=1024, tn=n_main // 3)
        small = _norm_mod_matmul(x, nw_m, sc_m, sh_m, w_small[l], F32, tm=1024, tn=LANES)
        a_t = jnp.swapaxes(small[:, :, DN_HEADS:2 * DN_HEADS], 1, 2)
        o_a = _deltanet(proj, small, a_t, conv_w[l], a_log[l], dt_bias[l], dn_norm[l], ts=512)
        qn, ksn, vs, kwn, vw, ck, cv = _nsa_prep(proj, nsa_col0, cos_f, sin_s, q_norm[l],
                                                 k_norm[l, 1], k_norm[l, 2], ts=512)
        kc, vc = _compress(ck, cv, cmp_pos[l], w_cmp1[l], w_cmp2[l], k_norm[l, 0], cos_c, sin_c)
        gates = small[:, :, 2 * DN_HEADS:n_small].reshape(b, s, NSA_GROUPS, 3 * NSA_HPG)
        o_b = _nsa_attention(qn, kc, vc, ksn, vs, kwn, vw, jnp.swapaxes(gates, 1, 2), tq=256, tk=1024)
        x = _mix_out(x, o_a, o_b, proj, wdn // (2 * d), g_m, w_oa16[l], w_ob16[l], w_out16[l], tm=512)
        nw_f = norm_ffn[l].reshape(1, d)
        if l % 2 == 0:
            x = _dense_ffn(x, nw_f, sc_f, sh_f, g_f, w1d[l // 2], w3d[l // 2], w2d[l // 2], tm=1024, tf=512)
        else:
            x = _moe_ffn(x, nw_f, sc_f, sh_f, g_f, w_router[l // 2], w1m[l // 2], w3m[l // 2],
                         w2m[l // 2], tm=1024, tf=512)
    return x
```

```python
import functools

import jax
import jax.numpy as jnp
import numpy as np
from jax import lax
from jax.experimental import pallas as pl
from jax.experimental.pallas import tpu as pltpu

F32 = jnp.float32
BF16 = jnp.bfloat16

DN_HEADS = 8
DN_DK = 64
DN_DV = 64
DN_CHUNK = 64
CONV_W = 4
DN_QKV = DN_HEADS * (2 * DN_DK + DN_DV)
NSA_HEADS = 8
NSA_GROUPS = 2
NSA_HPG = NSA_HEADS // NSA_GROUPS
NSA_DH = 64
CMP_LEN = 32
CMP_STRIDE = 16
SEL_LEN = 64
SEL_SHIFT = 6
SEL_TOP = 16
WINDOW = 512
ROPE_THETA = 10000.0
N_EXPERTS = 8
EPS = 1e-6
NEG = -1e30
FORCE = 1e6
SEL_BIAS = 1e30

LANES = 128
VMEM_LIMIT = 56 * 1024 * 1024


def _sigmoid(x):
    return 1.0 / (1.0 + jnp.exp(-x))


def _silu(x):
    return x * _sigmoid(x)


def _softplus(x):
    return jnp.maximum(x, 0.0) + jnp.log(1.0 + jnp.exp(-jnp.abs(x)))


def _dot(a, b):
    return jnp.dot(a, b, preferred_element_type=F32)


def _dot_nt(a, b):
    return lax.dot_general(a, b, (((1,), (1,)), ((), ())), preferred_element_type=F32)


def _dot_tn(a, b):
    return lax.dot_general(a, b, (((0,), (0,)), ((), ())), preferred_element_type=F32)


def _norm_mod(x, nw, sc, sh):
    y = x * lax.rsqrt(jnp.mean(x * x, -1, keepdims=True) + EPS) * nw
    return y * (1.0 + sc) + sh


def _params(sem):
    return pltpu.CompilerParams(dimension_semantics=sem, vmem_limit_bytes=VMEM_LIMIT)


def _mod_body(c_ref, w_ref, b_ref, o_ref):
    c = c_ref[...]
    o_ref[0] = _dot(_silu(c).astype(BF16), w_ref[0].astype(BF16)) + b_ref[0]


def _ada_mod(c, w_ada, b_ada):
    n_layers, d, n = w_ada.shape
    b = c.shape[0]
    tn = n // 4
    return pl.pallas_call(
        _mod_body,
        grid=(n_layers, n // tn),
        in_specs=[pl.BlockSpec((b, d), lambda l, j: (0, 0)),
                  pl.BlockSpec((1, d, tn), lambda l, j: (l, 0, j)),
                  pl.BlockSpec((1, 1, tn), lambda l, j: (l, 0, j))],
        out_specs=pl.BlockSpec((1, b, tn), lambda l, j: (l, 0, j)),
        out_shape=jax.ShapeDtypeStruct((n_layers, b, n), F32),
        compiler_params=_params(("parallel", "parallel")),
        name="ada_mod",
    )(c, w_ada, b_ada.reshape(n_layers, 1, n))


def _nm_mm_body(x_ref, nw_ref, sc_ref, sh_ref, w_ref, o_ref, h_ref):
    @pl.when(pl.program_id(2) == 0)
    def _():
        h_ref[...] = _norm_mod(x_ref[0], nw_ref[...], sc_ref[0], sh_ref[0]).astype(BF16)

    o_ref[0] = _dot(h_ref[...], w_ref[...]).astype(o_ref.dtype)


def _norm_mod_matmul(x, nw, sc, sh, w, out_dtype, tm, tn):
    b, s, d = x.shape
    n = w.shape[1]
    return pl.pallas_call(
        _nm_mm_body,
        grid=(b, s // tm, n // tn),
        in_specs=[pl.BlockSpec((1, tm, d), lambda bi, i, j: (bi, i, 0)),
                  pl.BlockSpec((1, d), lambda bi, i, j: (0, 0)),
                  pl.BlockSpec((1, 1, d), lambda bi, i, j: (bi, 0, 0)),
                  pl.BlockSpec((1, 1, d), lambda bi, i, j: (bi, 0, 0)),
                  pl.BlockSpec((d, tn), lambda bi, i, j: (0, j))],
        out_specs=pl.BlockSpec((1, tm, tn), lambda bi, i, j: (bi, i, j)),
        out_shape=jax.ShapeDtypeStruct((b, s, n), out_dtype),
        scratch_shapes=[pltpu.VMEM((tm, d), BF16)],
        compiler_params=_params(("parallel", "parallel", "arbitrary")),
        name="in_proj",
    )(x, nw, sc, sh, w)


def _dn_body(x_ref, sm_ref, at_ref, cw_ref, alog_ref, dtb_ref, alogt_ref, dtbt_ref, dnw_ref, o_ref,
             buf_ref, act_ref, gcn_ref, beta_ref, gct_ref, state_ref, *, ts):
    nc = ts // DN_CHUNK
    c64 = DN_CHUNK

    @pl.when(pl.program_id(1) == 0)
    def _():
        buf_ref[0:8, :] = jnp.zeros((8, DN_QKV), F32)
        state_ref[...] = jnp.zeros_like(state_ref)

    for sl in range(DN_QKV // LANES):
        cols = slice(sl * LANES, (sl + 1) * LANES)
        buf_ref[8:ts + 8, cols] = x_ref[0, :, cols].astype(F32)
        y = cw_ref[0:1, cols] * buf_ref[5:5 + ts, cols]
        for j in range(1, CONV_W):
            y = y + cw_ref[j:j + 1, cols] * buf_ref[5 + j:5 + j + ts, cols]
        buf_ref[0:8, cols] = buf_ref[ts:ts + 8, cols]
        act_ref[:, :, cols] = _silu(y).reshape(nc, c64, LANES)

    sm = sm_ref[0]
    beta_ref[...] = _sigmoid(sm).reshape(nc, c64, LANES)
    g = -jnp.exp(alog_ref[...]) * _softplus(sm + dtb_ref[...])
    row = lax.broadcasted_iota(jnp.int32, (ts, LANES), 0) & (c64 - 1)
    for sft in (1, 2, 4, 8, 16, 32):
        g = g + jnp.where(row >= sft, pltpu.roll(g, sft, 0), 0.0)
    gcn_ref[...] = g.reshape(nc, c64, LANES)
    gt = -jnp.exp(alogt_ref[...]) * _softplus(at_ref[0] + dtbt_ref[...])
    lane = lax.broadcasted_iota(jnp.int32, (DN_HEADS, ts), 1) & (c64 - 1)
    for sft in (1, 2, 4, 8, 16, 32):
        gt = gt + jnp.where(lane >= sft, pltpu.roll(gt, sft, 1), 0.0)
    for c in range(nc):
        gct_ref[c] = gt[:, c * c64:(c + 1) * c64]

    ri = lax.broadcasted_iota(jnp.int32, (c64, c64), 0)
    ci = lax.broadcasted_iota(jnp.int32, (c64, c64), 1)
    tril = ri >= ci
    strict = ri > ci
    eye = jnp.where(ri == ci, 1.0, 0.0).astype(F32)
    dnw = dnw_ref[...]

    def chunk(c, carry):
        gcn = gcn_ref[c]
        bet = beta_ref[c]
        gct = gct_ref[c]
        r0 = pl.multiple_of(c * c64, c64)
        hs = range(DN_HEADS)
        q = [act_ref[c, :, h * DN_DK:(h + 1) * DN_DK] for h in hs]
        k = [act_ref[c, :, (DN_HEADS + h) * DN_DK:(DN_HEADS + h + 1) * DN_DK] for h in hs]
        v = [act_ref[c, :, 2 * DN_HEADS * DN_DK + h * DN_DV:2 * DN_HEADS * DN_DK + (h + 1) * DN_DV]
             for h in hs]
        q = [x * lax.rsqrt(jnp.sum(x * x, -1, keepdims=True) + EPS) * (DN_DK ** -0.5) for x in q]
        k = [x * lax.rsqrt(jnp.sum(x * x, -1, keepdims=True) + EPS) for x in k]
        bcol = [bet[:, h:h + 1] for h in hs]
        gcol = [gcn[:, DN_HEADS + h:DN_HEADS + h + 1] for h in hs]
        decay = [jnp.where(tril, jnp.exp(jnp.where(tril, gcol[h] - gct[h:h + 1, :], 0.0)), 0.0) for h in hs]
        eg = [jnp.exp(x) for x in gcol]
        glast = [x[c64 - 1:c64, :] for x in gcol]
        kb = [k[h] * bcol[h] for h in hs]
        k16 = [x.astype(BF16) for x in k]
        kk = [_dot_nt(kb[h].astype(BF16), k16[h]) for h in hs]
        qk = [_dot_nt(q[h].astype(BF16), k16[h]) for h in hs]
        a16 = [jnp.where(tril, qk[h] * decay[h], 0.0).astype(BF16) for h in hs]
        m = [jnp.where(strict, -(kk[h] * decay[h]), 0.0) for h in hs]
        p = [eye + x for x in m]
        for _ in range(5):
            m = [_dot(x, x) for x in m]
            p = [p[h] + _dot(m[h], p[h]) for h in hs]
        tinv = [x.astype(BF16) for x in p]
        u = [_dot(tinv[h], (v[h] * bcol[h]).astype(BF16)) for h in hs]
        w = [_dot(tinv[h], (kb[h] * eg[h]).astype(BF16)) for h in hs]
        st = [state_ref[h] for h in hs]
        st16 = [x.astype(BF16) for x in st]
        ws = [_dot(w[h].astype(BF16), st16[h]) for h in hs]
        vn16 = [(u[h] - ws[h]).astype(BF16) for h in hs]
        qs = [_dot((q[h] * eg[h]).astype(BF16), st16[h]) for h in hs]
        av = [_dot(a16[h], vn16[h]) for h in hs]
        kv = [_dot_tn((k[h] * jnp.exp(glast[h] - gcol[h])).astype(BF16), vn16[h]) for h in hs]
        for h in hs:
            state_ref[h] = st[h] * jnp.exp(glast[h]) + kv[h]
            o = qs[h] + av[h]
            on = o * lax.rsqrt(jnp.mean(o * o, -1, keepdims=True) + EPS) * dnw
            z = x_ref[0, pl.ds(r0, c64), DN_QKV + h * DN_DV:DN_QKV + (h + 1) * DN_DV].astype(F32)
            o_ref[0, pl.ds(r0, c64), h * DN_DV:(h + 1) * DN_DV] = (on * _silu(z)).astype(o_ref.dtype)
        return carry

    lax.fori_loop(0, nc, chunk, 0)


def _deltanet(proj, small, a_t, conv_w, a_log, dt_bias, dn_norm, ts):
    b, s, _ = proj.shape
    wdn = DN_QKV + DN_HEADS * DN_DV
    pad = jnp.zeros((LANES - 2 * DN_HEADS,), F32)
    alog_row = jnp.concatenate([jnp.zeros((DN_HEADS,), F32), a_log, pad]).reshape(1, LANES)
    dtb_row = jnp.concatenate([jnp.zeros((DN_HEADS,), F32), dt_bias, pad]).reshape(1, LANES)
    nc = ts // DN_CHUNK
    return pl.pallas_call(
        functools.partial(_dn_body, ts=ts),
        grid=(b, s // ts),
        in_specs=[pl.BlockSpec((1, ts, wdn), lambda bi, i: (bi, i, 0)),
                  pl.BlockSpec((1, ts, LANES), lambda bi, i: (bi, i, 0)),
                  pl.BlockSpec((1, DN_HEADS, ts), lambda bi, i: (bi, 0, i)),
                  pl.BlockSpec((CONV_W, DN_QKV), lambda bi, i: (0, 0)),
                  pl.BlockSpec((1, LANES), lambda bi, i: (0, 0)),
                  pl.BlockSpec((1, LANES), lambda bi, i: (0, 0)),
                  pl.BlockSpec((DN_HEADS, 1), lambda bi, i: (0, 0)),
                  pl.BlockSpec((DN_HEADS, 1), lambda bi, i: (0, 0)),
                  pl.BlockSpec((1, DN_DV), lambda bi, i: (0, 0))],
        out_specs=pl.BlockSpec((1, ts, DN_HEADS * DN_DV), lambda bi, i: (bi, i, 0)),
        out_shape=jax.ShapeDtypeStruct((b, s, DN_HEADS * DN_DV), BF16),
        scratch_shapes=[pltpu.VMEM((ts + 8, DN_QKV), F32),
                        pltpu.VMEM((nc, DN_CHUNK, DN_QKV), F32),
                        pltpu.VMEM((nc, DN_CHUNK, LANES), F32),
                        pltpu.VMEM((nc, DN_CHUNK, LANES), F32),
                        pltpu.VMEM((nc, DN_HEADS, DN_CHUNK), F32),
                        pltpu.VMEM((DN_HEADS, DN_DK, DN_DV), F32)],
        compiler_params=_params(("parallel", "arbitrary")),
        name="deltanet",
    )(proj, small, a_t, conv_w, alog_row, dtb_row, a_log.reshape(DN_HEADS, 1),
      dt_bias.reshape(DN_HEADS, 1), dn_norm.reshape(1, DN_DV))


def _seg_ones():
    r = lax.broadcasted_iota(jnp.int32, (LANES, LANES), 0) // NSA_DH
    c = lax.broadcasted_iota(jnp.int32, (LANES, LANES), 1) // NSA_DH
    return jnp.where(r == c, 1.0, 0.0).astype(F32)


def _norm_rope(x, w, cos_f, sin_s, seg):
    ms = _dot(x * x, seg) * (1.0 / NSA_DH)
    y = x * lax.rsqrt(ms + EPS) * w
    half = NSA_DH // 2
    lane = lax.broadcasted_iota(jnp.int32, y.shape, 1) & (NSA_DH - 1)
    partner = jnp.where(lane < half, pltpu.roll(y, LANES - half, 1), pltpu.roll(y, half, 1))
    return y * cos_f + partner * sin_s


def _nsa_prep_body(q_ref, ck_ref, cv_ref, sk_ref, sv_ref, wk_ref, wv_ref, cos_ref, sin_ref,
                   qw_ref, skw_ref, wkw_ref,
                   qo_ref, sko_ref, svo_ref, wko_ref, wvo_ref, cko_ref, cvo_ref):
    seg = _seg_ones()
    cos_f = cos_ref[0]
    sin_s = sin_ref[0]
    for sl in range(NSA_HEADS * NSA_DH // LANES):
        x = q_ref[0, :, sl * LANES:(sl + 1) * LANES].astype(F32)
        y = (_norm_rope(x, qw_ref[...], cos_f, sin_s, seg) * (NSA_DH ** -0.5)).astype(BF16)
        for half in range(2):
            h = 2 * sl + half
            qo_ref[0, h // NSA_HPG, h % NSA_HPG, :, 0:NSA_DH] = y[:, half * NSA_DH:(half + 1) * NSA_DH]
            qo_ref[0, h // NSA_HPG, h % NSA_HPG, :, NSA_DH:LANES] = jnp.zeros(
                (y.shape[0], LANES - NSA_DH), BF16)
    sk = _norm_rope(sk_ref[0].astype(F32), skw_ref[...], cos_f, sin_s, seg).astype(BF16)
    wk = _norm_rope(wk_ref[0].astype(F32), wkw_ref[...], cos_f, sin_s, seg).astype(BF16)
    sv = sv_ref[0]
    wv = wv_ref[0]
    ts = sk.shape[0]
    blk = (pl.program_id(1) * ts + lax.broadcasted_iota(jnp.int32, (ts, LANES), 0)) >> SEL_SHIFT
    onehot = jnp.where(lax.broadcasted_iota(jnp.int32, (ts, LANES), 1) == blk, SEL_BIAS, 0.0).astype(BF16)
    for g in range(NSA_GROUPS):
        cols = slice(g * NSA_DH, (g + 1) * NSA_DH)
        sko_ref[0, g, :, 0:LANES] = onehot
        sko_ref[0, g, :, LANES:LANES + NSA_DH] = sk[:, cols]
        sko_ref[0, g, :, LANES + NSA_DH:2 * LANES] = jnp.zeros((ts, LANES - NSA_DH), BF16)
        wko_ref[0, g] = wk[:, cols]
        svo_ref[0, g] = sv[:, cols]
        wvo_ref[0, g] = wv[:, cols]
    cko_ref[0] = ck_ref[0]
    cvo_ref[0] = cv_ref[0]


def _nsa_prep(proj, col0, cos_f, sin_s, q_norm, k_norm_s, k_norm_w, ts):
    b, s, _ = proj.shape
    assert s // SEL_LEN <= LANES
    qw = NSA_HEADS * NSA_DH
    qblk = col0 // qw
    k0 = (col0 + qw) // LANES

    def kspec(i):
        return pl.BlockSpec((1, ts, LANES), lambda bi, t, i=i: (bi, t, k0 + i))

    tile2 = lambda w: jnp.tile(w.reshape(1, NSA_DH), (1, LANES // NSA_DH))
    gshape = jax.ShapeDtypeStruct((b, NSA_GROUPS, s, NSA_DH), BF16)
    gspec = pl.BlockSpec((1, NSA_GROUPS, ts, NSA_DH), lambda bi, t: (bi, 0, t, 0))
    ashape = jax.ShapeDtypeStruct((b, NSA_GROUPS, s, 2 * LANES), BF16)
    aspec = pl.BlockSpec((1, NSA_GROUPS, ts, 2 * LANES), lambda bi, t: (bi, 0, t, 0))
    cspec = pl.BlockSpec((1, ts, LANES), lambda bi, t: (bi, t, 0))
    wspec = pl.BlockSpec((1, LANES), lambda bi, t: (0, 0))
    return pl.pallas_call(
        _nsa_prep_body,
        grid=(b, s // ts),
        in_specs=[pl.BlockSpec((1, ts, qw), lambda bi, t: (bi, t, qblk)),
                  kspec(0), kspec(1), kspec(2), kspec(3), kspec(4), kspec(5),
                  cspec, cspec, wspec, wspec, wspec],
        out_specs=[pl.BlockSpec((1, NSA_GROUPS, NSA_HPG, ts, LANES), lambda bi, t: (bi, 0, 0, t, 0)),
                   aspec, gspec, gspec, gspec, cspec, cspec],
        out_shape=[jax.ShapeDtypeStruct((b, NSA_GROUPS, NSA_HPG, s, LANES), BF16),
                   ashape, gshape, gshape, gshape,
                   jax.ShapeDtypeStruct((b, s, LANES), BF16),
                   jax.ShapeDtypeStruct((b, s, LANES), BF16)],
        compiler_params=_params(("parallel", "parallel")),
        name="nsa_prep",
    )(proj, proj, proj, proj, proj, proj, proj, cos_f, sin_s,
      tile2(q_norm), tile2(k_norm_s), tile2(k_norm_w))


def _compress_body(ck_ref, cv_ref, pos_ref, w1a_ref, w1b_ref, w2_ref, kw_ref, cos_ref, sin_ref,
                   ko_ref, vo_ref):
    n = ck_ref.shape[1]
    outs = []
    for which, x_ref in enumerate((ck_ref, cv_ref)):
        x = x_ref[0].astype(F32)
        lo = _dot((x + pos_ref[which, 0:1, :]).astype(BF16), w1a_ref[which])
        hi = _dot((x + pos_ref[which, 1:2, :]).astype(BF16), w1b_ref[which])
        h1 = _silu(lo + pltpu.roll(hi, n - 1, 0))
        outs.append(_dot(h1.astype(BF16), w2_ref[which]))
    kc = _norm_rope(outs[0], kw_ref[...], cos_ref[0], sin_ref[0], _seg_ones()).astype(BF16)
    vc = outs[1].astype(BF16)
    for g in range(NSA_GROUPS):
        ko_ref[0, g] = kc[:, g * NSA_DH:(g + 1) * NSA_DH]
        vo_ref[0, g] = vc[:, g * NSA_DH:(g + 1) * NSA_DH]


def _compress(ck, cv, cmp_pos, w_cmp1, w_cmp2, k_norm_c, cos_c, sin_c):
    b, s, _ = ck.shape
    n = s // CMP_STRIDE
    width = CMP_STRIDE * LANES
    per_row = CMP_LEN // CMP_STRIDE
    eye_g = jnp.eye(NSA_GROUPS, dtype=F32)
    w1 = w_cmp1.reshape(2, per_row, CMP_STRIDE, NSA_DH, NSA_DH)
    w1 = jnp.einsum('khldo,gG->khlgdGo', w1, eye_g).reshape(2, per_row, width, LANES).astype(BF16)
    w2 = jnp.einsum('kdo,gG->kgdGo', w_cmp2, eye_g).reshape(2, LANES, LANES).astype(BF16)
    pos = jnp.broadcast_to(cmp_pos.reshape(2, per_row, CMP_STRIDE, 1, NSA_DH),
                           (2, per_row, CMP_STRIDE, NSA_GROUPS, NSA_DH)).reshape(2, per_row, width)
    kw = jnp.tile(k_norm_c.reshape(1, NSA_DH), (1, NSA_GROUPS))
    full = lambda shp: pl.BlockSpec(shp, lambda bi: (0,) * len(shp))
    bspec = pl.BlockSpec((1, n, width), lambda bi: (bi, 0, 0))
    tspec = pl.BlockSpec((1, n, LANES), lambda bi: (bi, 0, 0))
    ospec = pl.BlockSpec((1, NSA_GROUPS, n, NSA_DH), lambda bi: (bi, 0, 0, 0))
    oshape = jax.ShapeDtypeStruct((b, NSA_GROUPS, n, NSA_DH), BF16)
    return pl.pallas_call(
        _compress_body,
        grid=(b,),
        in_specs=[bspec, bspec, full((2, per_row, width)), full((2, width, LANES)),
                  full((2, width, LANES)), full((2, LANES, LANES)), full((1, LANES)), tspec, tspec],
        out_specs=[ospec, ospec],
        out_shape=[oshape, oshape],
        compiler_params=_params(("parallel",)),
        name="nsa_compress",
    )(ck.reshape(b, n, width), cv.reshape(b, n, width), pos, w1[:, 0], w1[:, 1], w2, kw, cos_c, sin_c)


def _nsa_body(q_ref, kc_ref, vc_ref, ksa_ref, vs_ref, kw_ref, vw_ref, gt_ref, o_ref,
              qa_ref, m_ref, lp_ref, acc_ref, *, tq, tk, n_sel):
    hp = NSA_HPG
    hs = range(hp)
    t0 = pl.program_id(2) * tq
    q = [q_ref[0, 0, j][:, 0:NSA_DH] for j in hs]
    t_q = t0 + lax.broadcasted_iota(jnp.int32, (tq, 1), 0)

    kc = kc_ref[0, 0]
    vc = vc_ref[0, 0]
    n_cmp = kc.shape[0]
    cmp_end = lax.broadcasted_iota(jnp.int32, (1, n_cmp), 1) * CMP_STRIDE + (CMP_LEN - 1)
    bias_c = jnp.where(cmp_end <= t_q, 0.0, NEG)
    valid_c = jnp.where(t_q >= CMP_LEN - 1, 1.0, 0.0)
    cs = lax.broadcasted_iota(jnp.int32, (LANES, n_cmp), 1) * CMP_STRIDE
    bs = lax.broadcasted_iota(jnp.int32, (LANES, n_cmp), 0) * SEL_LEN
    overlap_t = jnp.where((cs < bs + SEL_LEN) & (cs + CMP_LEN > bs), 1.0, 0.0).astype(BF16)
    s_c = [_dot_nt(q[j], kc) for j in hs]
    o_c = []
    imp = None
    for j in hs:
        sc = s_c[j] + bias_c
        e_c = jnp.exp(sc - jnp.max(sc, -1, keepdims=True))
        p16 = (e_c * (valid_c / jnp.sum(e_c, -1, keepdims=True))).astype(BF16)
        o_c.append(_dot(p16, vc))
        part = _dot_nt(overlap_t, p16)
        imp = part if imp is None else imp + part

    jb = lax.broadcasted_iota(jnp.int32, (LANES, 1), 0)
    t_row = t0 + lax.broadcasted_iota(jnp.int32, (1, tq), 1)
    cur = t_row >> SEL_SHIFT
    forced = (jb == 0) | (jb == cur) | (jb == cur - 1)
    imp = jnp.where(forced, FORCE, jnp.where(jb * SEL_LEN <= t_row, imp, -FORCE))
    imp = jnp.where(jb < n_sel, imp, -jnp.inf)
    jbf = jb.astype(F32)
    sel_t = jnp.zeros((LANES, tq), F32)
    for _ in range(min(SEL_TOP, n_sel)):
        mx = jnp.max(imp, 0, keepdims=True)
        first = jnp.min(jnp.where(imp == mx, jbf, float(LANES)), 0, keepdims=True)
        hit = jbf == first
        sel_t = jnp.where(hit, 1.0, sel_t)
        imp = jnp.where(hit, -jnp.inf, imp)
    ri = lax.broadcasted_iota(jnp.int32, (LANES, LANES), 0)
    ci = lax.broadcasted_iota(jnp.int32, (LANES, LANES), 1)
    sel = _dot_tn(sel_t.astype(BF16), jnp.where(ri == ci, 1.0, 0.0).astype(BF16))
    selm1 = (sel - 1.0).astype(BF16)
    for j in hs:
        qa_ref[j, :, 0:LANES] = selm1
        qa_ref[j, :, LANES:2 * LANES] = q_ref[0, 0, j]

    m_ref[...] = jnp.full(m_ref.shape, NEG, F32)
    lp_ref[...] = jnp.zeros(lp_ref.shape, F32)
    acc_ref[...] = jnp.zeros(acc_ref.shape, F32)

    def key_tile(kt, diagonal):
        k0 = pl.multiple_of(kt * tk, tk)
        k_aug = ksa_ref[0, 0, pl.ds(k0, tk), :]
        v = vs_ref[0, 0, pl.ds(k0, tk), :]
        s = [_dot_nt(qa_ref[j], k_aug) for j in hs]
        if diagonal:
            tok = k0 + lax.broadcasted_iota(jnp.int32, (1, tk), 1)
            bias = jnp.where(tok <= t_q, 0.0, NEG)
        for j in hs:
            sj = s[j] + bias if diagonal else s[j]
            m_prev = m_ref[j]
            m_new = jnp.maximum(m_prev, jnp.max(sj, -1, keepdims=True))
            alpha = jnp.exp(m_prev - m_new)
            e = jnp.exp(sj - m_new)
            part = e[:, 0:LANES]
            for i in range(1, tk // LANES):
                part = part + e[:, i * LANES:(i + 1) * LANES]
            lp_ref[j] = alpha * lp_ref[j] + part
            acc_ref[j] = alpha * acc_ref[j] + _dot(e.astype(BF16), v)
            m_ref[j] = m_new

    k_diag = t0 // tk

    def below_diagonal(kt, carry):
        key_tile(kt, False)
        return carry

    lax.fori_loop(0, k_diag, below_diagonal, 0)
    key_tile(k_diag, True)
    o_s = [acc_ref[j] / jnp.maximum(jnp.sum(lp_ref[j], -1, keepdims=True), 1e-30) for j in hs]

    wl = WINDOW + tq
    w0 = pl.multiple_of(jnp.maximum(t0 - WINDOW, 0), tq)
    dist = t_q - (w0 + lax.broadcasted_iota(jnp.int32, (1, wl), 1))
    bias_w = jnp.where((dist >= 0) & (dist < WINDOW), 0.0, NEG)
    k_w = kw_ref[0, 0, pl.ds(w0, wl), :]
    v_w = vw_ref[0, 0, pl.ds(w0, wl), :]
    s_w = [_dot_nt(q[j], k_w) for j in hs]
    o_w = []
    for j in hs:
        sw = s_w[j] + bias_w
        e_w = jnp.exp(sw - jnp.max(sw, -1, keepdims=True))
        o_w.append(_dot(e_w.astype(BF16), v_w) / jnp.sum(e_w, -1, keepdims=True))

    gates = _sigmoid(gt_ref[0, 0])
    for j in range(hp):
        o = (gates[:, 3 * j:3 * j + 1] * o_c[j] + gates[:, 3 * j + 1:3 * j + 2] * o_s[j]
             + gates[:, 3 * j + 2:3 * j + 3] * o_w[j])
        o_ref[0, :, j * NSA_DH:(j + 1) * NSA_DH] = o.astype(o_ref.dtype)


def _nsa_attention(q, kc, vc, ksa, vs, kw, vw, gates, tq, tk):
    b, g, hp, s, _ = q.shape
    dh = NSA_DH
    n_cmp = kc.shape[2]
    assert s >= WINDOW + tq and WINDOW % tq == 0 and s % tk == 0 and tk % tq == 0 and tk % LANES == 0
    seq_spec = pl.BlockSpec((1, 1, s, dh), lambda bi, gi, i: (bi, gi, 0, 0))
    cmp_spec = pl.BlockSpec((1, 1, n_cmp, dh), lambda bi, gi, i: (bi, gi, 0, 0))
    return pl.pallas_call(
        functools.partial(_nsa_body, tq=tq, tk=tk, n_sel=s // SEL_LEN),
        grid=(b, g, s // tq),
        in_specs=[pl.BlockSpec((1, 1, hp, tq, LANES), lambda bi, gi, i: (bi, gi, 0, i, 0)),
                  cmp_spec, cmp_spec,
                  pl.BlockSpec((1, 1, s, 2 * LANES), lambda bi, gi, i: (bi, gi, 0, 0)),
                  seq_spec, seq_spec, seq_spec,
                  pl.BlockSpec((1, 1, tq, 3 * hp), lambda bi, gi, i: (bi, gi, i, 0))],
        out_specs=pl.BlockSpec((1, tq, hp * dh), lambda bi, gi, i: (bi, i, gi)),
        out_shape=jax.ShapeDtypeStruct((b, s, g * hp * dh), BF16),
        scratch_shapes=[pltpu.VMEM((hp, tq, 2 * LANES), BF16), pltpu.VMEM((hp, tq, 1), F32),
                        pltpu.VMEM((hp, tq, LANES), F32), pltpu.VMEM((hp, tq, dh), F32)],
        compiler_params=_params(("parallel", "parallel", "arbitrary")),
        name="nsa_attention",
    )(q, kc, vc, ksa, vs, kw, vw, gates)


def _mix_out_body(x_ref, oa_ref, ob_ref, mg_ref, gm_ref, woa_ref, wob_ref, wout_ref, o_ref):
    d = x_ref.shape[2]
    y_a = _dot(oa_ref[0], woa_ref[...])
    y_b = _dot(ob_ref[0], wob_ref[...])
    merged = (_sigmoid(mg_ref[0, :, 0:d].astype(F32)) * y_a
              + _sigmoid(mg_ref[0, :, d:2 * d].astype(F32)) * y_b)
    o_ref[0] = x_ref[0] + gm_ref[0] * _dot(merged.astype(BF16), wout_ref[...])


def _mix_out(x, o_a, o_b, proj, mg_blk, g_m, w_oa, w_ob, w_out, tm):
    b, s, d = x.shape
    full = lambda a: pl.BlockSpec(a.shape, lambda bi, i: (0, 0))
    return pl.pallas_call(
        _mix_out_body,
        grid=(b, s // tm),
        in_specs=[pl.BlockSpec((1, tm, d), lambda bi, i: (bi, i, 0)),
                  pl.BlockSpec((1, tm, o_a.shape[2]), lambda bi, i: (bi, i, 0)),
                  pl.BlockSpec((1, tm, o_b.shape[2]), lambda bi, i: (bi, i, 0)),
                  pl.BlockSpec((1, tm, 2 * d), lambda bi, i: (bi, i, mg_blk)),
                  pl.BlockSpec((1, 1, d), lambda bi, i: (bi, 0, 0)),
                  full(w_oa), full(w_ob), full(w_out)],
        out_specs=pl.BlockSpec((1, tm, d), lambda bi, i: (bi, i, 0)),
        out_shape=jax.ShapeDtypeStruct(x.shape, F32),
        compiler_params=_params(("parallel", "parallel")),
        name="mix_out",
    )(x, o_a, o_b, proj, g_m, w_oa, w_ob, w_out)


def _ffn_body(x_ref, nw_ref, sc_ref, sh_ref, gf_ref, w1_ref, w3_ref, w2_ref, o_ref, h_ref, acc_ref):
    f = pl.program_id(2)

    @pl.when(f == 0)
    def _():
        h_ref[...] = _norm_mod(x_ref[0], nw_ref[...], sc_ref[0], sh_ref[0]).astype(BF16)
        acc_ref[...] = jnp.zeros_like(acc_ref)

    h = h_ref[...]
    t = _silu(_dot(h, w1_ref[...])) * _dot(h, w3_ref[...])
    acc_ref[...] += _dot(t.astype(BF16), w2_ref[...])

    @pl.when(f == pl.num_programs(2) - 1)
    def _():
        o_ref[0] = x_ref[0] + gf_ref[0] * acc_ref[...]


def _dense_ffn(x, nw, sc, sh, g_f, w1, w3, w2, tm, tf):
    b, s, d = x.shape
    ff = w1.shape[1]
    vec = pl.BlockSpec((1, 1, d), lambda bi, i, f: (bi, 0, 0))
    return pl.pallas_call(
        _ffn_body,
        grid=(b, s // tm, ff // tf),
        in_specs=[pl.BlockSpec((1, tm, d), lambda bi, i, f: (bi, i, 0)),
                  pl.BlockSpec((1, d), lambda bi, i, f: (0, 0)), vec, vec, vec,
                  pl.BlockSpec((d, tf), lambda bi, i, f: (0, f)),
                  pl.BlockSpec((d, tf), lambda bi, i, f: (0, f)),
                  pl.BlockSpec((tf, d), lambda bi, i, f: (f, 0))],
        out_specs=pl.BlockSpec((1, tm, d), lambda bi, i, f: (bi, i, 0)),
        out_shape=jax.ShapeDtypeStruct(x.shape, F32),
        scratch_shapes=[pltpu.VMEM((tm, d), BF16), pltpu.VMEM((tm, d), F32)],
        compiler_params=_params(("parallel", "parallel", "arbitrary")),
        name="dense_ffn",
    )(x, nw, sc, sh, g_f, w1, w3, w2)


MOE_ROWS = 128


def _moe_route_body(x_ref, nw_ref, sc_ref, sh_ref, wr_ref, h_ref, route_ref, meta_ref):
    tm = x_ref.shape[1]
    h16 = _norm_mod(x_ref[0], nw_ref[...], sc_ref[0], sh_ref[0]).astype(BF16)
    h_ref[0] = h16
    logits = _dot_nt(wr_ref[...], h16)
    ef = lax.broadcasted_iota(jnp.int32, (N_EXPERTS, tm), 0).astype(F32)
    m1 = jnp.max(logits, 0, keepdims=True)
    i1 = jnp.min(jnp.where(logits == m1, ef, float(N_EXPERTS)), 0, keepdims=True)
    rest = jnp.where(ef == i1, -jnp.inf, logits)
    m2 = jnp.max(rest, 0, keepdims=True)
    i2 = jnp.min(jnp.where(rest == m2, ef, float(N_EXPERTS)), 0, keepdims=True)
    e2 = jnp.exp(m2 - m1)
    oh1 = jnp.where(ef == i1, 1.0, 0.0)
    oh2 = jnp.where(ef == i2, 1.0, 0.0)
    member = oh1 + oh2
    lane = lax.broadcasted_iota(jnp.int32, (N_EXPERTS, tm), 1)
    csum = member
    sft = 1
    while sft < tm:
        csum = csum + jnp.where(lane >= sft, pltpu.roll(csum, sft, 1), 0.0)
        sft *= 2
    count = jnp.max(csum, 1, keepdims=True)
    nblk = jnp.floor((count + (MOE_ROWS - 1)) * (1.0 / MOE_ROWS))
    nblk_b = jnp.broadcast_to(nblk, (N_EXPERTS, LANES))
    row = lax.broadcasted_iota(jnp.int32, (N_EXPERTS, LANES), 0)
    bsum = nblk_b
    for sft in (1, 2, 4):
        bsum = bsum + jnp.where(row >= sft, pltpu.roll(bsum, sft, 0), 0.0)
    bstart = bsum - nblk_b
    slot = bstart[:, 0:1] * MOE_ROWS + (csum - member)
    rrow = lax.broadcasted_iota(jnp.int32, (8, tm), 0)
    route_ref[0] = jnp.where(
        rrow == 0, jnp.sum(oh1 * slot, 0, keepdims=True),
        jnp.where(rrow == 1, jnp.sum(oh2 * slot, 0, keepdims=True),
                  jnp.where(rrow == 2, 1.0 / (1.0 + e2), jnp.where(rrow == 3, e2 / (1.0 + e2), 0.0))))
    col = lax.broadcasted_iota(jnp.int32, (N_EXPERTS, LANES), 1)
    meta_ref[0] = jnp.where(col == 0, nblk_b, jnp.where(col == 1, bstart, 0.0)).astype(jnp.int32)


def _moe_group_body(nblk_ref, bstart_ref, x_ref, h_ref, route_ref, gf_ref, w1_ref, w3_ref, w2_ref, o_ref,
                    hb_ref, cw_ref, acc_ref, *, n_rows):
    i = pl.program_id(0)
    e = pl.program_id(1)
    f = pl.program_id(2)
    tm = x_ref.shape[1]
    nb = nblk_ref[i * N_EXPERTS + e]
    b0 = bstart_ref[i * N_EXPERTS + e]
    slot1 = route_ref[0, 0:1, :]
    slot2 = route_ref[0, 1:2, :]

    def hits(r0, rows):
        rr = (r0 + lax.broadcasted_iota(jnp.int32, (rows, 1), 0)).astype(F32)
        return rr == slot1, rr == slot2

    def expert_rows(k, first):
        r0 = pl.multiple_of((b0 + k) * MOE_ROWS, MOE_ROWS)
        rows = pl.ds(r0, MOE_ROWS)
        if first:
            hit1, hit2 = hits(r0, MOE_ROWS)
            gather = jnp.where(hit1, 1.0, jnp.where(hit2, 1.0, 0.0)).astype(BF16)
            hb_ref[rows, :] = _dot(gather, h_ref[0]).astype(BF16)
            cw_ref[rows, :] = jnp.sum(jnp.where(hit1, route_ref[0, 2:3, :], 0.0)
                                      + jnp.where(hit2, route_ref[0, 3:4, :], 0.0), -1, keepdims=True)
        hb = hb_ref[rows, :]
        t = _silu(_dot(hb, w1_ref[0])) * _dot(hb, w3_ref[0]) * cw_ref[rows, :]
        y = _dot(t.astype(BF16), w2_ref[0])
        if first:
            acc_ref[rows, :] = y
        else:
            acc_ref[rows, :] += y

    @pl.when(f == 0)
    def _():
        lax.fori_loop(0, nb, lambda k, c: (expert_rows(k, True), c)[1], 0)

    @pl.when(f != 0)
    def _():
        lax.fori_loop(0, nb, lambda k, c: (expert_rows(k, False), c)[1], 0)

    @pl.when((e == pl.num_programs(1) - 1) & (f == pl.num_programs(2) - 1))
    def _():
        def clear(k, c):
            acc_ref[pl.ds(pl.multiple_of(k * MOE_ROWS, MOE_ROWS), MOE_ROWS), :] = jnp.zeros(
                (MOE_ROWS, acc_ref.shape[1]), F32)
            return c

        lax.fori_loop(b0 + nb, n_rows // MOE_ROWS, clear, 0)
        chunk = 4 * MOE_ROWS
        for kc in range(n_rows // chunk):
            hit1, hit2 = hits(kc * chunk, chunk)
            scatter = jnp.where(hit1, 1.0, jnp.where(hit2, 1.0, 0.0)).astype(BF16)
            y = _dot_tn(scatter, acc_ref[kc * chunk:(kc + 1) * chunk, :].astype(BF16))
            if kc == 0:
                o_ref[0] = y
            else:
                o_ref[0] += y
        o_ref[0] = x_ref[0] + gf_ref[0] * o_ref[0]


def _moe_ffn(x, nw, sc, sh, g_f, w_router, w1, w3, w2, tm, tf):
    b, s, d = x.shape
    n_e, _, ff = w1.shape
    assert n_e == N_EXPERTS
    tiles_b = s // tm
    nt = b * tiles_b
    vec = pl.BlockSpec((1, 1, d), lambda bi, i: (bi, 0, 0))
    h16, route, meta = pl.pallas_call(
        _moe_route_body,
        grid=(b, tiles_b),
        in_specs=[pl.BlockSpec((1, tm, d), lambda bi, i: (bi, i, 0)),
                  pl.BlockSpec((1, d), lambda bi, i: (0, 0)), vec, vec,
                  pl.BlockSpec((n_e, d), lambda bi, i: (0, 0))],
        out_specs=[pl.BlockSpec((1, tm, d), lambda bi, i: (bi, i, 0)),
                   pl.BlockSpec((1, 8, tm), lambda bi, i: (bi * tiles_b + i, 0, 0)),
                   pl.BlockSpec((1, n_e, LANES), lambda bi, i: (bi * tiles_b + i, 0, 0))],
        out_shape=[jax.ShapeDtypeStruct((b, s, d), BF16),
                   jax.ShapeDtypeStruct((nt, 8, tm), F32),
                   jax.ShapeDtypeStruct((nt, n_e, LANES), jnp.int32)],
        compiler_params=_params(("parallel", "parallel")),
        name="moe_route",
    )(x, nw, sc, sh, w_router.T.astype(BF16))
    n_rows = -(-(2 * tm + n_e * (MOE_ROWS - 1)) // (4 * MOE_ROWS)) * (4 * MOE_ROWS)
    grid_spec = pltpu.PrefetchScalarGridSpec(
        num_scalar_prefetch=2,
        grid=(nt, n_e, ff // tf),
        in_specs=[pl.BlockSpec((1, tm, d), lambda i, e, f, nb, bs: (i, 0, 0)),
                  pl.BlockSpec((1, tm, d), lambda i, e, f, nb, bs: (i, 0, 0)),
                  pl.BlockSpec((1, 8, tm), lambda i, e, f, nb, bs: (i, 0, 0)),
                  pl.BlockSpec((1, 1, d), lambda i, e, f, nb, bs: (i // tiles_b, 0, 0)),
                  pl.BlockSpec((1, d, tf), lambda i, e, f, nb, bs: (e, 0, f)),
                  pl.BlockSpec((1, d, tf), lambda i, e, f, nb, bs: (e, 0, f)),
                  pl.BlockSpec((1, tf, d), lambda i, e, f, nb, bs: (e, f, 0))],
        out_specs=pl.BlockSpec((1, tm, d), lambda i, e, f, nb, bs: (i, 0, 0)),
        scratch_shapes=[pltpu.VMEM((n_rows, d), BF16), pltpu.VMEM((n_rows, 1), F32),
                        pltpu.VMEM((n_rows, d), F32)])
    out = pl.pallas_call(
        functools.partial(_moe_group_body, n_rows=n_rows),
        grid_spec=grid_spec,
        out_shape=jax.ShapeDtypeStruct((nt, tm, d), F32),
        compiler_params=_params(("parallel", "arbitrary", "arbitrary")),
        name="moe_group",
    )(meta[:, :, 0].reshape(-1), meta[:, :, 1].reshape(-1),
      x.reshape(nt, tm, d), h16.reshape(nt, tm, d), route, g_f, w1, w3, w2)
    return out.reshape(b, s, d)


def _rope_tables(pos):
    inv = 1.0 / (ROPE_THETA ** (jnp.arange(0, NSA_DH, 2, dtype=F32) / NSA_DH))
    ang = pos.astype(F32)[..., None] * inv
    cos, sin = jnp.cos(ang), jnp.sin(ang)
    reps = LANES // NSA_DH
    return (jnp.tile(jnp.concatenate([cos, cos], -1), (1, 1, reps)),
            jnp.tile(jnp.concatenate([-sin, sin], -1), (1, 1, reps)))


_SPLITS = (DN_QKV, DN_HEADS * DN_DV, DN_HEADS, DN_HEADS, NSA_HEADS * NSA_DH) + (NSA_GROUPS * NSA_DH,) * 6
_OFF = np.concatenate([[0], np.cumsum(_SPLITS)])
_OFF_NG = int(_OFF[-1])
_OFF_MG = _OFF_NG + 3 * NSA_HEADS


def kernel(x, c, positions, w_ada, b_ada, norm_mix, norm_ffn, w_in, conv_w, a_log, dt_bias, dn_norm, cmp_pos, w_cmp1, w_cmp2, q_norm, k_norm, w_oa, w_ob, w_out, w1_dense, w3_dense, w2_dense, w_router, w1_moe, w3_moe, w2_moe):
    b, s, d = x.shape
    depth = w_in.shape[0]
    wdn = DN_QKV + DN_HEADS * DN_DV
    n_small = 2 * DN_HEADS + 3 * NSA_HEADS

    cos_f, sin_s = _rope_tables(positions)
    n_cmp_pad = s // CMP_STRIDE
    cmp_end = jnp.minimum(jnp.arange(n_cmp_pad) * CMP_STRIDE + CMP_LEN - 1, s - 1)
    cos_c, sin_c = _rope_tables(positions[:, cmp_end])

    mod = _ada_mod(c, w_ada, b_ada)

    off_nq = int(_OFF[4])
    w_main = jnp.concatenate([w_in[:, :, 0:wdn], w_in[:, :, _OFF_MG:_OFF_MG + 2 * d],
                              w_in[:, :, off_nq:_OFF_NG]], -1).astype(BF16)
    w_small = jnp.concatenate([w_in[:, :, wdn:wdn + 2 * DN_HEADS], w_in[:, :, _OFF_NG:_OFF_MG],
                               jnp.zeros((depth, d, LANES - n_small), F32)], -1).astype(BF16)
    nsa_col0 = wdn + 2 * d
    n_main = w_main.shape[2]

    w_oa16, w_ob16, w_out16 = w_oa.astype(BF16), w_ob.astype(BF16), w_out.astype(BF16)
    w1d, w3d, w2d = w1_dense.astype(BF16), w3_dense.astype(BF16), w2_dense.astype(BF16)
    w1m, w3m, w2m = w1_moe.astype(BF16), w3_moe.astype(BF16), w2_moe.astype(BF16)

    for l in range(depth):
        sh_m, sc_m, g_m, sh_f, sc_f, g_f = [m.reshape(b, 1, d) for m in jnp.split(mod[l], 6, -1)]
        nw_m = norm_mix[l].reshape(1, d)
        proj = _norm_mod_matmul(x, nw_m, sc_m, sh_m, w_main[l], BF16, tm=1024, tn=n_main // 3)
        small = _norm_mod_matmul(x, nw_m, sc_m, sh_m, w_small[l], F32, tm=1024, tn=LANES)
        a_t = jnp.swapaxes(small[:, :, DN_HEADS:2 * DN_HEADS], 1, 2)
        o_a = _deltanet(proj, small, a_t, conv_w[l], a_log[l], dt_bias[l], dn_norm[l], ts=512)
        qn, ksn, vs, kwn, vw, ck, cv = _nsa_prep(proj, nsa_col0, cos_f, sin_s, q_norm[l],
                                                 k_norm[l, 1], k_norm[l, 2], ts=512)
        kc, vc = _compress(ck, cv, cmp_pos[l], w_cmp1[l], w_cmp2[l], k_norm[l, 0], cos_c, sin_c)
        gates = small[:, :, 2 * DN_HEADS:n_small].reshape(b, s, NSA_GROUPS, 3 * NSA_HPG)
        o_b = _nsa_attention(qn, kc, vc, ksn, vs, kwn, vw, jnp.swapaxes(gates, 1, 2), tq=256, tk=1024)
        x = _mix_out(x, o_a, o_b, proj, wdn // (2 * d), g_m, w_oa16[l], w_ob16[l], w_out16[l], tm=512)
        nw_f = norm_ffn[l].reshape(1, d)
        if l % 2 == 0:
            x = _dense_ffn(x, nw_f, sc_f, sh_f, g_f, w1d[l // 2], w3d[l // 2], w2d[l // 2], tm=1024, tf=512)
        else:
            x = _moe_ffn(x, nw_f, sc_f, sh_f, g_f, w_router[l // 2], w1m[l // 2], w3m[l // 2],
                         w2m[l // 2], tm=1024, tf=896)
    return x
```

```python
import functools

import jax
import jax.numpy as jnp
import numpy as np
from jax import lax
from jax.experimental import pallas as pl
from jax.experimental.pallas import tpu as pltpu

F32 = jnp.float32
BF16 = jnp.bfloat16

DN_HEADS = 8
DN_DK = 64
DN_DV = 64
DN_CHUNK = 64
CONV_W = 4
DN_QKV = DN_HEADS * (2 * DN_DK + DN_DV)
NSA_HEADS = 8
NSA_GROUPS = 2
NSA_HPG = NSA_HEADS // NSA_GROUPS
NSA_DH = 64
CMP_LEN = 32
CMP_STRIDE = 16
SEL_LEN = 64
SEL_SHIFT = 6
SEL_TOP = 16
WINDOW = 512
ROPE_THETA = 10000.0
N_EXPERTS = 8
EPS = 1e-6
NEG = -1e30
FORCE = 1e6
SEL_BIAS = 1e30

LANES = 128
VMEM_LIMIT = 56 * 1024 * 1024


def _sigmoid(x):
    return 1.0 / (1.0 + jnp.exp(-x))


def _silu(x):
    return x * _sigmoid(x)


def _softplus(x):
    return jnp.maximum(x, 0.0) + jnp.log(1.0 + jnp.exp(-jnp.abs(x)))


def _dot(a, b):
    return jnp.dot(a, b, preferred_element_type=F32)


def _dot_nt(a, b):
    return lax.dot_general(a, b, (((1,), (1,)), ((), ())), preferred_element_type=F32)


def _dot_tn(a, b):
    return lax.dot_general(a, b, (((0,), (0,)), ((), ())), preferred_element_type=F32)


def _norm_mod(x, nw, sc, sh):
    y = x * lax.rsqrt(jnp.mean(x * x, -1, keepdims=True) + EPS) * nw
    return y * (1.0 + sc) + sh


def _params(sem):
    return pltpu.CompilerParams(dimension_semantics=sem, vmem_limit_bytes=VMEM_LIMIT)


def _mod_body(c_ref, w_ref, b_ref, o_ref):
    c = c_ref[...]
    o_ref[0] = _dot(_silu(c).astype(BF16), w_ref[0].astype(BF16)) + b_ref[0]


def _ada_mod(c, w_ada, b_ada):
    n_layers, d, n = w_ada.shape
    b = c.shape[0]
    tn = n // 4
    return pl.pallas_call(
        _mod_body,
        grid=(n_layers, n // tn),
        in_specs=[pl.BlockSpec((b, d), lambda l, j: (0, 0)),
                  pl.BlockSpec((1, d, tn), lambda l, j: (l, 0, j)),
                  pl.BlockSpec((1, 1, tn), lambda l, j: (l, 0, j))],
        out_specs=pl.BlockSpec((1, b, tn), lambda l, j: (l, 0, j)),
        out_shape=jax.ShapeDtypeStruct((n_layers, b, n), F32),
        compiler_params=_params(("parallel", "parallel")),
        name="ada_mod",
    )(c, w_ada, b_ada.reshape(n_layers, 1, n))


def _nm_mm_body(x_ref, nw_ref, sc_ref, sh_ref, w_ref, o_ref, h_ref):
    @pl.when(pl.program_id(2) == 0)
    def _():
        h_ref[...] = _norm_mod(x_ref[0], nw_ref[...], sc_ref[0], sh_ref[0]).astype(BF16)

    o_ref[0] = _dot(h_ref[...], w_ref[...]).astype(o_ref.dtype)


def _norm_mod_matmul(x, nw, sc, sh, w, out_dtype, tm, tn):
    b, s, d = x.shape
    n = w.shape[1]
    return pl.pallas_call(
        _nm_mm_body,
        grid=(b, s // tm, n // tn),
        in_specs=[pl.BlockSpec((1, tm, d), lambda bi, i, j: (bi, i, 0)),
                  pl.BlockSpec((1, d), lambda bi, i, j: (0, 0)),
                  pl.BlockSpec((1, 1, d), lambda bi, i, j: (bi, 0, 0)),
                  pl.BlockSpec((1, 1, d), lambda bi, i, j: (bi, 0, 0)),
                  pl.BlockSpec((d, tn), lambda bi, i, j: (0, j))],
        out_specs=pl.BlockSpec((1, tm, tn), lambda bi, i, j: (bi, i, j)),
        out_shape=jax.ShapeDtypeStruct((b, s, n), out_dtype),
        scratch_shapes=[pltpu.VMEM((tm, d), BF16)],
        compiler_params=_params(("parallel", "parallel", "arbitrary")),
        name="in_proj",
    )(x, nw, sc, sh, w)


def _dn_body(x_ref, sm_ref, at_ref, cw_ref, alog_ref, dtb_ref, alogt_ref, dtbt_ref, dnw_ref, o_ref,
             buf_ref, act_ref, gcn_ref, beta_ref, gct_ref, state_ref, *, ts):
    nc = ts // DN_CHUNK
    c64 = DN_CHUNK

    @pl.when(pl.program_id(1) == 0)
    def _():
        buf_ref[0:8, :] = jnp.zeros((8, DN_QKV), F32)
        state_ref[...] = jnp.zeros_like(state_ref)

    for sl in range(DN_QKV // LANES):
        cols = slice(sl * LANES, (sl + 1) * LANES)
        buf_ref[8:ts + 8, cols] = x_ref[0, :, cols].astype(F32)
        y = cw_ref[0:1, cols] * buf_ref[5:5 + ts, cols]
        for j in range(1, CONV_W):
            y = y + cw_ref[j:j + 1, cols] * buf_ref[5 + j:5 + j + ts, cols]
        buf_ref[0:8, cols] = buf_ref[ts:ts + 8, cols]
        act_ref[:, :, cols] = _silu(y).reshape(nc, c64, LANES)

    sm = sm_ref[0]
    beta_ref[...] = _sigmoid(sm).reshape(nc, c64, LANES)
    g = -jnp.exp(alog_ref[...]) * _softplus(sm + dtb_ref[...])
    row = lax.broadcasted_iota(jnp.int32, (ts, LANES), 0) & (c64 - 1)
    for sft in (1, 2, 4, 8, 16, 32):
        g = g + jnp.where(row >= sft, pltpu.roll(g, sft, 0), 0.0)
    gcn_ref[...] = g.reshape(nc, c64, LANES)
    gt = -jnp.exp(alogt_ref[...]) * _softplus(at_ref[0] + dtbt_ref[...])
    lane = lax.broadcasted_iota(jnp.int32, (DN_HEADS, ts), 1) & (c64 - 1)
    for sft in (1, 2, 4, 8, 16, 32):
        gt = gt + jnp.where(lane >= sft, pltpu.roll(gt, sft, 1), 0.0)
    for c in range(nc):
        gct_ref[c] = gt[:, c * c64:(c + 1) * c64]

    ri = lax.broadcasted_iota(jnp.int32, (c64, c64), 0)
    ci = lax.broadcasted_iota(jnp.int32, (c64, c64), 1)
    tril = ri >= ci
    strict = ri > ci
    eye = jnp.where(ri == ci, 1.0, 0.0).astype(F32)
    dnw = dnw_ref[...]

    hs = range(DN_HEADS)
    grp = 2 if nc % 2 == 0 else 1

    def chunk_group(cg, carry):
        items = [(j, h) for j in range(grp) for h in hs]
        n = range(len(items))
        cs = [cg * grp + j for j in range(grp)]
        gcn = [gcn_ref[c] for c in cs]
        bet = [beta_ref[c] for c in cs]
        gct = [gct_ref[c] for c in cs]
        q = [act_ref[cs[j], :, h * DN_DK:(h + 1) * DN_DK] for j, h in items]
        k = [act_ref[cs[j], :, (DN_HEADS + h) * DN_DK:(DN_HEADS + h + 1) * DN_DK] for j, h in items]
        v = [act_ref[cs[j], :, 2 * DN_HEADS * DN_DK + h * DN_DV:2 * DN_HEADS * DN_DK + (h + 1) * DN_DV]
             for j, h in items]
        q = [x * lax.rsqrt(jnp.sum(x * x, -1, keepdims=True) + EPS) * (DN_DK ** -0.5) for x in q]
        k = [x * lax.rsqrt(jnp.sum(x * x, -1, keepdims=True) + EPS) for x in k]
        bcol = [bet[j][:, h:h + 1] for j, h in items]
        gcol = [gcn[j][:, DN_HEADS + h:DN_HEADS + h + 1] for j, h in items]
        grow = [gct[j][h:h + 1, :] for j, h in items]
        decay = [jnp.where(tril, jnp.exp(jnp.where(tril, gcol[i] - grow[i], 0.0)), 0.0) for i in n]
        eg = [jnp.exp(x) for x in gcol]
        glast = [x[c64 - 1:c64, :] for x in gcol]
        kb = [k[i] * bcol[i] for i in n]
        k16 = [x.astype(BF16) for x in k]
        kk = [_dot_nt(kb[i].astype(BF16), k16[i]) for i in n]
        qk = [_dot_nt(q[i].astype(BF16), k16[i]) for i in n]
        a16 = [jnp.where(tril, qk[i] * decay[i], 0.0).astype(BF16) for i in n]
        m = [jnp.where(strict, -(kk[i] * decay[i]), 0.0) for i in n]
        p = [eye + x for x in m]
        for _ in range(5):
            m = [_dot(x, x) for x in m]
            p = [p[i] + _dot(m[i], p[i]) for i in n]
        tinv = [x.astype(BF16) for x in p]
        u = [_dot(tinv[i], (v[i] * bcol[i]).astype(BF16)) for i in n]
        w16 = [_dot(tinv[i], (kb[i] * eg[i]).astype(BF16)).astype(BF16) for i in n]
        qe16 = [(q[i] * eg[i]).astype(BF16) for i in n]
        kd16 = [(k[i] * jnp.exp(glast[i] - gcol[i])).astype(BF16) for i in n]
        egl = [jnp.exp(x) for x in glast]
        for j in range(grp):
            idx = [j * DN_HEADS + h for h in hs]
            r0 = pl.multiple_of(cs[j] * c64, c64)
            st = [state_ref[h] for h in hs]
            st16 = [x.astype(BF16) for x in st]
            ws = [_dot(w16[idx[h]], st16[h]) for h in hs]
            vn16 = [(u[idx[h]] - ws[h]).astype(BF16) for h in hs]
            qs = [_dot(qe16[idx[h]], st16[h]) for h in hs]
            av = [_dot(a16[idx[h]], vn16[h]) for h in hs]
            kv = [_dot_tn(kd16[idx[h]], vn16[h]) for h in hs]
            for h in hs:
                state_ref[h] = st[h] * egl[idx[h]] + kv[h]
                o = qs[h] + av[h]
                on = o * lax.rsqrt(jnp.mean(o * o, -1, keepdims=True) + EPS) * dnw
                z = x_ref[0, pl.ds(r0, c64), DN_QKV + h * DN_DV:DN_QKV + (h + 1) * DN_DV].astype(F32)
                o_ref[0, pl.ds(r0, c64), h * DN_DV:(h + 1) * DN_DV] = (on * _silu(z)).astype(o_ref.dtype)
        return carry

    lax.fori_loop(0, nc // grp, chunk_group, 0)


def _deltanet(proj, small, a_t, conv_w, a_log, dt_bias, dn_norm, ts):
    b, s, _ = proj.shape
    wdn = DN_QKV + DN_HEADS * DN_DV
    pad = jnp.zeros((LANES - 2 * DN_HEADS,), F32)
    alog_row = jnp.concatenate([jnp.zeros((DN_HEADS,), F32), a_log, pad]).reshape(1, LANES)
    dtb_row = jnp.concatenate([jnp.zeros((DN_HEADS,), F32), dt_bias, pad]).reshape(1, LANES)
    nc = ts // DN_CHUNK
    return pl.pallas_call(
        functools.partial(_dn_body, ts=ts),
        grid=(b, s // ts),
        in_specs=[pl.BlockSpec((1, ts, wdn), lambda bi, i: (bi, i, 0)),
                  pl.BlockSpec((1, ts, LANES), lambda bi, i: (bi, i, 0)),
                  pl.BlockSpec((1, DN_HEADS, ts), lambda bi, i: (bi, 0, i)),
                  pl.BlockSpec((CONV_W, DN_QKV), lambda bi, i: (0, 0)),
                  pl.BlockSpec((1, LANES), lambda bi, i: (0, 0)),
                  pl.BlockSpec((1, LANES), lambda bi, i: (0, 0)),
                  pl.BlockSpec((DN_HEADS, 1), lambda bi, i: (0, 0)),
                  pl.BlockSpec((DN_HEADS, 1), lambda bi, i: (0, 0)),
                  pl.BlockSpec((1, DN_DV), lambda bi, i: (0, 0))],
        out_specs=pl.BlockSpec((1, ts, DN_HEADS * DN_DV), lambda bi, i: (bi, i, 0)),
        out_shape=jax.ShapeDtypeStruct((b, s, DN_HEADS * DN_DV), BF16),
        scratch_shapes=[pltpu.VMEM((ts + 8, DN_QKV), F32),
                        pltpu.VMEM((nc, DN_CHUNK, DN_QKV), F32),
                        pltpu.VMEM((nc, DN_CHUNK, LANES), F32),
                        pltpu.VMEM((nc, DN_CHUNK, LANES), F32),
                        pltpu.VMEM((nc, DN_HEADS, DN_CHUNK), F32),
                        pltpu.VMEM((DN_HEADS, DN_DK, DN_DV), F32)],
        compiler_params=_params(("parallel", "arbitrary")),
        name="deltanet",
    )(proj, small, a_t, conv_w, alog_row, dtb_row, a_log.reshape(DN_HEADS, 1),
      dt_bias.reshape(DN_HEADS, 1), dn_norm.reshape(1, DN_DV))


def _seg_ones():
    r = lax.broadcasted_iota(jnp.int32, (LANES, LANES), 0) // NSA_DH
    c = lax.broadcasted_iota(jnp.int32, (LANES, LANES), 1) // NSA_DH
    return jnp.where(r == c, 1.0, 0.0).astype(F32)


def _norm_rope(x, w, cos_f, sin_s, seg):
    ms = _dot(x * x, seg) * (1.0 / NSA_DH)
    y = x * lax.rsqrt(ms + EPS) * w
    half = NSA_DH // 2
    lane = lax.broadcasted_iota(jnp.int32, y.shape, 1) & (NSA_DH - 1)
    partner = jnp.where(lane < half, pltpu.roll(y, LANES - half, 1), pltpu.roll(y, half, 1))
    return y * cos_f + partner * sin_s


def _nsa_prep_body(q_ref, ck_ref, cv_ref, sk_ref, sv_ref, wk_ref, wv_ref, cos_ref, sin_ref,
                   qw_ref, skw_ref, wkw_ref,
                   qo_ref, sko_ref, svo_ref, wko_ref, wvo_ref, cko_ref, cvo_ref):
    seg = _seg_ones()
    cos_f = cos_ref[0]
    sin_s = sin_ref[0]
    for sl in range(NSA_HEADS * NSA_DH // LANES):
        x = q_ref[0, :, sl * LANES:(sl + 1) * LANES].astype(F32)
        y = (_norm_rope(x, qw_ref[...], cos_f, sin_s, seg) * (NSA_DH ** -0.5)).astype(BF16)
        for half in range(2):
            h = 2 * sl + half
            qo_ref[0, h // NSA_HPG, h % NSA_HPG, :, 0:NSA_DH] = y[:, half * NSA_DH:(half + 1) * NSA_DH]
            qo_ref[0, h // NSA_HPG, h % NSA_HPG, :, NSA_DH:LANES] = jnp.zeros(
                (y.shape[0], LANES - NSA_DH), BF16)
    sk = _norm_rope(sk_ref[0].astype(F32), skw_ref[...], cos_f, sin_s, seg).astype(BF16)
    wk = _norm_rope(wk_ref[0].astype(F32), wkw_ref[...], cos_f, sin_s, seg).astype(BF16)
    sv = sv_ref[0]
    wv = wv_ref[0]
    ts = sk.shape[0]
    blk = (pl.program_id(1) * ts + lax.broadcasted_iota(jnp.int32, (ts, LANES), 0)) >> SEL_SHIFT
    onehot = jnp.where(lax.broadcasted_iota(jnp.int32, (ts, LANES), 1) == blk, SEL_BIAS, 0.0).astype(BF16)
    for g in range(NSA_GROUPS):
        cols = slice(g * NSA_DH, (g + 1) * NSA_DH)
        sko_ref[0, g, :, 0:LANES] = onehot
        sko_ref[0, g, :, LANES:LANES + NSA_DH] = sk[:, cols]
        sko_ref[0, g, :, LANES + NSA_DH:2 * LANES] = jnp.zeros((ts, LANES - NSA_DH), BF16)
        wko_ref[0, g] = wk[:, cols]
        svo_ref[0, g] = sv[:, cols]
        wvo_ref[0, g] = wv[:, cols]
    cko_ref[0] = ck_ref[0]
    cvo_ref[0] = cv_ref[0]


def _nsa_prep(proj, col0, cos_f, sin_s, q_norm, k_norm_s, k_norm_w, ts):
    b, s, _ = proj.shape
    assert s // SEL_LEN <= LANES
    qw = NSA_HEADS * NSA_DH
    qblk = col0 // qw
    k0 = (col0 + qw) // LANES

    def kspec(i):
        return pl.BlockSpec((1, ts, LANES), lambda bi, t, i=i: (bi, t, k0 + i))

    tile2 = lambda w: jnp.tile(w.reshape(1, NSA_DH), (1, LANES // NSA_DH))
    gshape = jax.ShapeDtypeStruct((b, NSA_GROUPS, s, NSA_DH), BF16)
    gspec = pl.BlockSpec((1, NSA_GROUPS, ts, NSA_DH), lambda bi, t: (bi, 0, t, 0))
    ashape = jax.ShapeDtypeStruct((b, NSA_GROUPS, s, 2 * LANES), BF16)
    aspec = pl.BlockSpec((1, NSA_GROUPS, ts, 2 * LANES), lambda bi, t: (bi, 0, t, 0))
    cspec = pl.BlockSpec((1, ts, LANES), lambda bi, t: (bi, t, 0))
    wspec = pl.BlockSpec((1, LANES), lambda bi, t: (0, 0))
    return pl.pallas_call(
        _nsa_prep_body,
        grid=(b, s // ts),
        in_specs=[pl.BlockSpec((1, ts, qw), lambda bi, t: (bi, t, qblk)),
                  kspec(0), kspec(1), kspec(2), kspec(3), kspec(4), kspec(5),
                  cspec, cspec, wspec, wspec, wspec],
        out_specs=[pl.BlockSpec((1, NSA_GROUPS, NSA_HPG, ts, LANES), lambda bi, t: (bi, 0, 0, t, 0)),
                   aspec, gspec, gspec, gspec, cspec, cspec],
        out_shape=[jax.ShapeDtypeStruct((b, NSA_GROUPS, NSA_HPG, s, LANES), BF16),
                   ashape, gshape, gshape, gshape,
                   jax.ShapeDtypeStruct((b, s, LANES), BF16),
                   jax.ShapeDtypeStruct((b, s, LANES), BF16)],
        compiler_params=_params(("parallel", "parallel")),
        name="nsa_prep",
    )(proj, proj, proj, proj, proj, proj, proj, cos_f, sin_s,
      tile2(q_norm), tile2(k_norm_s), tile2(k_norm_w))


def _compress_body(ck_ref, cv_ref, pos_ref, w1a_ref, w1b_ref, w2_ref, kw_ref, cos_ref, sin_ref,
                   ko_ref, vo_ref):
    n = ck_ref.shape[1]
    outs = []
    for which, x_ref in enumerate((ck_ref, cv_ref)):
        x = x_ref[0].astype(F32)
        lo = _dot((x + pos_ref[which, 0:1, :]).astype(BF16), w1a_ref[which])
        hi = _dot((x + pos_ref[which, 1:2, :]).astype(BF16), w1b_ref[which])
        h1 = _silu(lo + pltpu.roll(hi, n - 1, 0))
        outs.append(_dot(h1.astype(BF16), w2_ref[which]))
    kc = _norm_rope(outs[0], kw_ref[...], cos_ref[0], sin_ref[0], _seg_ones()).astype(BF16)
    vc = outs[1].astype(BF16)
    for g in range(NSA_GROUPS):
        ko_ref[0, g] = kc[:, g * NSA_DH:(g + 1) * NSA_DH]
        vo_ref[0, g] = vc[:, g * NSA_DH:(g + 1) * NSA_DH]


def _compress(ck, cv, cmp_pos, w_cmp1, w_cmp2, k_norm_c, cos_c, sin_c):
    b, s, _ = ck.shape
    n = s // CMP_STRIDE
    width = CMP_STRIDE * LANES
    per_row = CMP_LEN // CMP_STRIDE
    eye_g = jnp.eye(NSA_GROUPS, dtype=F32)
    w1 = w_cmp1.reshape(2, per_row, CMP_STRIDE, NSA_DH, NSA_DH)
    w1 = jnp.einsum('khldo,gG->khlgdGo', w1, eye_g).reshape(2, per_row, width, LANES).astype(BF16)
    w2 = jnp.einsum('kdo,gG->kgdGo', w_cmp2, eye_g).reshape(2, LANES, LANES).astype(BF16)
    pos = jnp.broadcast_to(cmp_pos.reshape(2, per_row, CMP_STRIDE, 1, NSA_DH),
                           (2, per_row, CMP_STRIDE, NSA_GROUPS, NSA_DH)).reshape(2, per_row, width)
    kw = jnp.tile(k_norm_c.reshape(1, NSA_DH), (1, NSA_GROUPS))
    full = lambda shp: pl.BlockSpec(shp, lambda bi: (0,) * len(shp))
    bspec = pl.BlockSpec((1, n, width), lambda bi: (bi, 0, 0))
    tspec = pl.BlockSpec((1, n, LANES), lambda bi: (bi, 0, 0))
    ospec = pl.BlockSpec((1, NSA_GROUPS, n, NSA_DH), lambda bi: (bi, 0, 0, 0))
    oshape = jax.ShapeDtypeStruct((b, NSA_GROUPS, n, NSA_DH), BF16)
    return pl.pallas_call(
        _compress_body,
        grid=(b,),
        in_specs=[bspec, bspec, full((2, per_row, width)), full((2, width, LANES)),
                  full((2, width, LANES)), full((2, LANES, LANES)), full((1, LANES)), tspec, tspec],
        out_specs=[ospec, ospec],
        out_shape=[oshape, oshape],
        compiler_params=_params(("parallel",)),
        name="nsa_compress",
    )(ck.reshape(b, n, width), cv.reshape(b, n, width), pos, w1[:, 0], w1[:, 1], w2, kw, cos_c, sin_c)


def _nsa_body(q_ref, kc_ref, vc_ref, ksa_ref, vs_ref, kw_ref, vw_ref, gt_ref, o_ref,
              qa_ref, m_ref, lp_ref, acc_ref, *, tq, tk, n_sel):
    hp = NSA_HPG
    hs = range(hp)
    t0 = pl.program_id(2) * tq
    q = [q_ref[0, 0, j][:, 0:NSA_DH] for j in hs]
    t_q = t0 + lax.broadcasted_iota(jnp.int32, (tq, 1), 0)

    kc = kc_ref[0, 0]
    vc = vc_ref[0, 0]
    n_cmp = kc.shape[0]
    cmp_end = lax.broadcasted_iota(jnp.int32, (1, n_cmp), 1) * CMP_STRIDE + (CMP_LEN - 1)
    bias_c = jnp.where(cmp_end <= t_q, 0.0, NEG)
    valid_c = jnp.where(t_q >= CMP_LEN - 1, 1.0, 0.0)
    cs = lax.broadcasted_iota(jnp.int32, (LANES, n_cmp), 1) * CMP_STRIDE
    bs = lax.broadcasted_iota(jnp.int32, (LANES, n_cmp), 0) * SEL_LEN
    overlap_t = jnp.where((cs < bs + SEL_LEN) & (cs + CMP_LEN > bs), 1.0, 0.0).astype(BF16)
    s_c = [_dot_nt(q[j], kc) for j in hs]
    o_c = []
    imp = None
    for j in hs:
        sc = s_c[j] + bias_c
        e_c = jnp.exp(sc - jnp.max(sc, -1, keepdims=True))
        p16 = (e_c * (valid_c / jnp.sum(e_c, -1, keepdims=True))).astype(BF16)
        o_c.append(_dot(p16, vc))
        part = _dot_nt(overlap_t, p16)
        imp = part if imp is None else imp + part

    jb = lax.broadcasted_iota(jnp.int32, (LANES, 1), 0)
    t_row = t0 + lax.broadcasted_iota(jnp.int32, (1, tq), 1)
    cur = t_row >> SEL_SHIFT
    forced = (jb == 0) | (jb == cur) | (jb == cur - 1)
    imp = jnp.where(forced, FORCE, jnp.where(jb * SEL_LEN <= t_row, imp, -FORCE))
    imp = jnp.where(jb < n_sel, imp, -jnp.inf)
    jbf = jb.astype(F32)
    sel_t = jnp.zeros((LANES, tq), F32)
    for _ in range(min(SEL_TOP, n_sel)):
        mx = jnp.max(imp, 0, keepdims=True)
        first = jnp.min(jnp.where(imp == mx, jbf, float(LANES)), 0, keepdims=True)
        hit = jbf == first
        sel_t = jnp.where(hit, 1.0, sel_t)
        imp = jnp.where(hit, -jnp.inf, imp)
    ri = lax.broadcasted_iota(jnp.int32, (LANES, LANES), 0)
    ci = lax.broadcasted_iota(jnp.int32, (LANES, LANES), 1)
    sel = _dot_tn(sel_t.astype(BF16), jnp.where(ri == ci, 1.0, 0.0).astype(BF16))
    selm1 = (sel - 1.0).astype(BF16)
    for j in hs:
        qa_ref[j, :, 0:LANES] = selm1
        qa_ref[j, :, LANES:2 * LANES] = q_ref[0, 0, j]

    m_ref[...] = jnp.full(m_ref.shape, NEG, F32)
    lp_ref[...] = jnp.zeros(lp_ref.shape, F32)
    acc_ref[...] = jnp.zeros(acc_ref.shape, F32)

    def key_tile(kt, diagonal):
        k0 = pl.multiple_of(kt * tk, tk)
        k_aug = ksa_ref[0, 0, pl.ds(k0, tk), :]
        v = vs_ref[0, 0, pl.ds(k0, tk), :]
        s = [_dot_nt(qa_ref[j], k_aug) for j in hs]
        if diagonal:
            tok = k0 + lax.broadcasted_iota(jnp.int32, (1, tk), 1)
            bias = jnp.where(tok <= t_q, 0.0, NEG)
        for j in hs:
            sj = s[j] + bias if diagonal else s[j]
            m_prev = m_ref[j]
            m_new = jnp.maximum(m_prev, jnp.max(sj, -1, keepdims=True))
            alpha = jnp.exp(m_prev - m_new)
            e = jnp.exp(sj - m_new)
            part = e[:, 0:LANES]
            for i in range(1, tk // LANES):
                part = part + e[:, i * LANES:(i + 1) * LANES]
            lp_ref[j] = alpha * lp_ref[j] + part
            acc_ref[j] = alpha * acc_ref[j] + _dot(e.astype(BF16), v)
            m_ref[j] = m_new

    k_diag = t0 // tk

    def below_diagonal(kt, carry):
        key_tile(kt, False)
        return carry

    lax.fori_loop(0, k_diag, below_diagonal, 0)
    key_tile(k_diag, True)
    o_s = [acc_ref[j] / jnp.maximum(jnp.sum(lp_ref[j], -1, keepdims=True), 1e-30) for j in hs]

    wl = WINDOW + tq
    w0 = pl.multiple_of(jnp.maximum(t0 - WINDOW, 0), tq)
    dist = t_q - (w0 + lax.broadcasted_iota(jnp.int32, (1, wl), 1))
    bias_w = jnp.where((dist >= 0) & (dist < WINDOW), 0.0, NEG)
    k_w = kw_ref[0, 0, pl.ds(w0, wl), :]
    v_w = vw_ref[0, 0, pl.ds(w0, wl), :]
    s_w = [_dot_nt(q[j], k_w) for j in hs]
    o_w = []
    for j in hs:
        sw = s_w[j] + bias_w
        e_w = jnp.exp(sw - jnp.max(sw, -1, keepdims=True))
        o_w.append(_dot(e_w.astype(BF16), v_w) / jnp.sum(e_w, -1, keepdims=True))

    gates = _sigmoid(gt_ref[0, 0])
    for j in range(hp):
        o = (gates[:, 3 * j:3 * j + 1] * o_c[j] + gates[:, 3 * j + 1:3 * j + 2] * o_s[j]
             + gates[:, 3 * j + 2:3 * j + 3] * o_w[j])
        o_ref[0, :, j * NSA_DH:(j + 1) * NSA_DH] = o.astype(o_ref.dtype)


def _nsa_attention(q, kc, vc, ksa, vs, kw, vw, gates, tq, tk):
    b, g, hp, s, _ = q.shape
    dh = NSA_DH
    n_cmp = kc.shape[2]
    assert s >= WINDOW + tq and WINDOW % tq == 0 and s % tk == 0 and tk % tq == 0 and tk % LANES == 0
    seq_spec = pl.BlockSpec((1, 1, s, dh), lambda bi, gi, i: (bi, gi, 0, 0))
    cmp_spec = pl.BlockSpec((1, 1, n_cmp, dh), lambda bi, gi, i: (bi, gi, 0, 0))
    return pl.pallas_call(
        functools.partial(_nsa_body, tq=tq, tk=tk, n_sel=s // SEL_LEN),
        grid=(b, g, s // tq),
        in_specs=[pl.BlockSpec((1, 1, hp, tq, LANES), lambda bi, gi, i: (bi, gi, 0, i, 0)),
                  cmp_spec, cmp_spec,
                  pl.BlockSpec((1, 1, s, 2 * LANES), lambda bi, gi, i: (bi, gi, 0, 0)),
                  seq_spec, seq_spec, seq_spec,
                  pl.BlockSpec((1, 1, tq, 3 * hp), lambda bi, gi, i: (bi, gi, i, 0))],
        out_specs=pl.BlockSpec((1, tq, hp * dh), lambda bi, gi, i: (bi, i, gi)),
        out_shape=jax.ShapeDtypeStruct((b, s, g * hp * dh), BF16),
        scratch_shapes=[pltpu.VMEM((hp, tq, 2 * LANES), BF16), pltpu.VMEM((hp, tq, 1), F32),
                        pltpu.VMEM((hp, tq, LANES), F32), pltpu.VMEM((hp, tq, dh), F32)],
        compiler_params=_params(("parallel", "parallel", "arbitrary")),
        name="nsa_attention",
    )(q, kc, vc, ksa, vs, kw, vw, gates)


def _mix_out_body(x_ref, oa_ref, ob_ref, mg_ref, gm_ref, woa_ref, wob_ref, wout_ref, o_ref):
    d = x_ref.shape[2]
    y_a = _dot(oa_ref[0], woa_ref[...])
    y_b = _dot(ob_ref[0], wob_ref[...])
    merged = (_sigmoid(mg_ref[0, :, 0:d].astype(F32)) * y_a
              + _sigmoid(mg_ref[0, :, d:2 * d].astype(F32)) * y_b)
    o_ref[0] = x_ref[0] + gm_ref[0] * _dot(merged.astype(BF16), wout_ref[...])


def _mix_out(x, o_a, o_b, proj, mg_blk, g_m, w_oa, w_ob, w_out, tm):
    b, s, d = x.shape
    full = lambda a: pl.BlockSpec(a.shape, lambda bi, i: (0, 0))
    return pl.pallas_call(
        _mix_out_body,
        grid=(b, s // tm),
        in_specs=[pl.BlockSpec((1, tm, d), lambda bi, i: (bi, i, 0)),
                  pl.BlockSpec((1, tm, o_a.shape[2]), lambda bi, i: (bi, i, 0)),
                  pl.BlockSpec((1, tm, o_b.shape[2]), lambda bi, i: (bi, i, 0)),
                  pl.BlockSpec((1, tm, 2 * d), lambda bi, i: (bi, i, mg_blk)),
                  pl.BlockSpec((1, 1, d), lambda bi, i: (bi, 0, 0)),
                  full(w_oa), full(w_ob), full(w_out)],
        out_specs=pl.BlockSpec((1, tm, d), lambda bi, i: (bi, i, 0)),
        out_shape=jax.ShapeDtypeStruct(x.shape, F32),
        compiler_params=_params(("parallel", "parallel")),
        name="mix_out",
    )(x, o_a, o_b, proj, g_m, w_oa, w_ob, w_out)


def _ffn_body(x_ref, nw_ref, sc_ref, sh_ref, gf_ref, w1_ref, w3_ref, w2_ref, o_ref, h_ref, acc_ref):
    f = pl.program_id(2)

    @pl.when(f == 0)
    def _():
        h_ref[...] = _norm_mod(x_ref[0], nw_ref[...], sc_ref[0], sh_ref[0]).astype(BF16)
        acc_ref[...] = jnp.zeros_like(acc_ref)

    h = h_ref[...]
    t = _silu(_dot(h, w1_ref[...])) * _dot(h, w3_ref[...])
    acc_ref[...] += _dot(t.astype(BF16), w2_ref[...])

    @pl.when(f == pl.num_programs(2) - 1)
    def _():
        o_ref[0] = x_ref[0] + gf_ref[0] * acc_ref[...]


def _dense_ffn(x, nw, sc, sh, g_f, w1, w3, w2, tm, tf):
    b, s, d = x.shape
    ff = w1.shape[1]
    vec = pl.BlockSpec((1, 1, d), lambda bi, i, f: (bi, 0, 0))
    return pl.pallas_call(
        _ffn_body,
        grid=(b, s // tm, ff // tf),
        in_specs=[pl.BlockSpec((1, tm, d), lambda bi, i, f: (bi, i, 0)),
                  pl.BlockSpec((1, d), lambda bi, i, f: (0, 0)), vec, vec, vec,
                  pl.BlockSpec((d, tf), lambda bi, i, f: (0, f)),
                  pl.BlockSpec((d, tf), lambda bi, i, f: (0, f)),
                  pl.BlockSpec((tf, d), lambda bi, i, f: (f, 0))],
        out_specs=pl.BlockSpec((1, tm, d), lambda bi, i, f: (bi, i, 0)),
        out_shape=jax.ShapeDtypeStruct(x.shape, F32),
        scratch_shapes=[pltpu.VMEM((tm, d), BF16), pltpu.VMEM((tm, d), F32)],
        compiler_params=_params(("parallel", "parallel", "arbitrary")),
        name="dense_ffn",
    )(x, nw, sc, sh, g_f, w1, w3, w2)


MOE_ROWS = 128


def _moe_route_body(x_ref, nw_ref, sc_ref, sh_ref, wr_ref, h_ref, route_ref, meta_ref):
    tm = x_ref.shape[1]
    h16 = _norm_mod(x_ref[0], nw_ref[...], sc_ref[0], sh_ref[0]).astype(BF16)
    h_ref[0] = h16
    logits = _dot_nt(wr_ref[...], h16)
    ef = lax.broadcasted_iota(jnp.int32, (N_EXPERTS, tm), 0).astype(F32)
    m1 = jnp.max(logits, 0, keepdims=True)
    i1 = jnp.min(jnp.where(logits == m1, ef, float(N_EXPERTS)), 0, keepdims=True)
    rest = jnp.where(ef == i1, -jnp.inf, logits)
    m2 = jnp.max(rest, 0, keepdims=True)
    i2 = jnp.min(jnp.where(rest == m2, ef, float(N_EXPERTS)), 0, keepdims=True)
    e2 = jnp.exp(m2 - m1)
    oh1 = jnp.where(ef == i1, 1.0, 0.0)
    oh2 = jnp.where(ef == i2, 1.0, 0.0)
    member = oh1 + oh2
    lane = lax.broadcasted_iota(jnp.int32, (N_EXPERTS, tm), 1)
    csum = member
    sft = 1
    while sft < tm:
        csum = csum + jnp.where(lane >= sft, pltpu.roll(csum, sft, 1), 0.0)
        sft *= 2
    count = jnp.max(csum, 1, keepdims=True)
    nblk = jnp.floor((count + (MOE_ROWS - 1)) * (1.0 / MOE_ROWS))
    nblk_b = jnp.broadcast_to(nblk, (N_EXPERTS, LANES))
    row = lax.broadcasted_iota(jnp.int32, (N_EXPERTS, LANES), 0)
    bsum = nblk_b
    for sft in (1, 2, 4):
        bsum = bsum + jnp.where(row >= sft, pltpu.roll(bsum, sft, 0), 0.0)
    bstart = bsum - nblk_b
    slot = bstart[:, 0:1] * MOE_ROWS + (csum - member)
    rrow = lax.broadcasted_iota(jnp.int32, (8, tm), 0)
    route_ref[0] = jnp.where(
        rrow == 0, jnp.sum(oh1 * slot, 0, keepdims=True),
        jnp.where(rrow == 1, jnp.sum(oh2 * slot, 0, keepdims=True),
                  jnp.where(rrow == 2, 1.0 / (1.0 + e2), jnp.where(rrow == 3, e2 / (1.0 + e2), 0.0))))
    col = lax.broadcasted_iota(jnp.int32, (N_EXPERTS, LANES), 1)
    meta_ref[0] = jnp.where(col == 0, nblk_b, jnp.where(col == 1, bstart, 0.0)).astype(jnp.int32)


def _moe_group_body(nblk_ref, bstart_ref, x_ref, h_ref, route_ref, gf_ref, w1_ref, w3_ref, w2_ref, o_ref,
                    hb_ref, cw_ref, acc_ref, *, n_rows):
    i = pl.program_id(0)
    e = pl.program_id(1)
    f = pl.program_id(2)
    tm = x_ref.shape[1]
    nb = nblk_ref[i * N_EXPERTS + e]
    b0 = bstart_ref[i * N_EXPERTS + e]
    slot1 = route_ref[0, 0:1, :]
    slot2 = route_ref[0, 1:2, :]

    def hits(r0, rows):
        rr = (r0 + lax.broadcasted_iota(jnp.int32, (rows, 1), 0)).astype(F32)
        return rr == slot1, rr == slot2

    def expert_rows(blk, n_blk, first):
        r0 = pl.multiple_of((b0 + blk) * MOE_ROWS, MOE_ROWS)
        rows = pl.ds(r0, n_blk * MOE_ROWS)
        if first:
            hit1, hit2 = hits(r0, n_blk * MOE_ROWS)
            gather = jnp.where(hit1, 1.0, jnp.where(hit2, 1.0, 0.0)).astype(BF16)
            hb_ref[rows, :] = _dot(gather, h_ref[0]).astype(BF16)
            cw_ref[rows, :] = jnp.sum(jnp.where(hit1, route_ref[0, 2:3, :], 0.0)
                                      + jnp.where(hit2, route_ref[0, 3:4, :], 0.0), -1, keepdims=True)
        hb = hb_ref[rows, :]
        t = _silu(_dot(hb, w1_ref[0, 0])) * _dot(hb, w3_ref[0, 0]) * cw_ref[rows, :]
        y = _dot(t.astype(BF16), w2_ref[0])
        if first:
            acc_ref[rows, :] = y
        else:
            acc_ref[rows, :] += y

    def expert_all(first):
        lax.fori_loop(0, nb // 2, lambda k, c: (expert_rows(2 * k, 2, first), c)[1], 0)

        @pl.when(nb % 2 == 1)
        def _():
            expert_rows(nb - 1, 1, first)

    @pl.when(f == 0)
    def _():
        expert_all(True)

    @pl.when(f != 0)
    def _():
        expert_all(False)

    @pl.when((e == pl.num_programs(1) - 1) & (f == pl.num_programs(2) - 1))
    def _():
        def clear(k, c):
            acc_ref[pl.ds(pl.multiple_of(k * MOE_ROWS, MOE_ROWS), MOE_ROWS), :] = jnp.zeros(
                (MOE_ROWS, acc_ref.shape[1]), F32)
            return c

        lax.fori_loop(b0 + nb, n_rows // MOE_ROWS, clear, 0)
        chunk = 4 * MOE_ROWS
        for kc in range(n_rows // chunk):
            hit1, hit2 = hits(kc * chunk, chunk)
            scatter = jnp.where(hit1, 1.0, jnp.where(hit2, 1.0, 0.0)).astype(BF16)
            y = _dot_tn(scatter, acc_ref[kc * chunk:(kc + 1) * chunk, :].astype(BF16))
            if kc == 0:
                o_ref[0] = y
            else:
                o_ref[0] += y
        o_ref[0] = x_ref[0] + gf_ref[0] * o_ref[0]


def _moe_ffn(x, nw, sc, sh, g_f, w_router, w1, w3, w2, tm, tf):
    b, s, d = x.shape
    n_e, _, ff = w1.shape
    assert n_e == N_EXPERTS
    tiles_b = s // tm
    nt = b * tiles_b
    vec = pl.BlockSpec((1, 1, d), lambda bi, i: (bi, 0, 0))
    h16, route, meta = pl.pallas_call(
        _moe_route_body,
        grid=(b, tiles_b),
        in_specs=[pl.BlockSpec((1, tm, d), lambda bi, i: (bi, i, 0)),
                  pl.BlockSpec((1, d), lambda bi, i: (0, 0)), vec, vec,
                  pl.BlockSpec((n_e, d), lambda bi, i: (0, 0))],
        out_specs=[pl.BlockSpec((1, tm, d), lambda bi, i: (bi, i, 0)),
                   pl.BlockSpec((1, 8, tm), lambda bi, i: (bi * tiles_b + i, 0, 0)),
                   pl.BlockSpec((1, n_e, LANES), lambda bi, i: (bi * tiles_b + i, 0, 0))],
        out_shape=[jax.ShapeDtypeStruct((b, s, d), BF16),
                   jax.ShapeDtypeStruct((nt, 8, tm), F32),
                   jax.ShapeDtypeStruct((nt, n_e, LANES), jnp.int32)],
        compiler_params=_params(("parallel", "parallel")),
        name="moe_route",
    )(x, nw, sc, sh, w_router.T.astype(BF16))
    chunked = lambda w: w.reshape(n_e, d, ff // tf, tf).transpose(0, 2, 1, 3)
    n_rows = -(-(2 * tm + n_e * (MOE_ROWS - 1)) // (4 * MOE_ROWS)) * (4 * MOE_ROWS)
    grid_spec = pltpu.PrefetchScalarGridSpec(
        num_scalar_prefetch=2,
        grid=(nt, n_e, ff // tf),
        in_specs=[pl.BlockSpec((1, tm, d), lambda i, e, f, nb, bs: (i, 0, 0)),
                  pl.BlockSpec((1, tm, d), lambda i, e, f, nb, bs: (i, 0, 0)),
                  pl.BlockSpec((1, 8, tm), lambda i, e, f, nb, bs: (i, 0, 0)),
                  pl.BlockSpec((1, 1, d), lambda i, e, f, nb, bs: (i // tiles_b, 0, 0)),
                  pl.BlockSpec((1, 1, d, tf), lambda i, e, f, nb, bs: (e, f, 0, 0)),
                  pl.BlockSpec((1, 1, d, tf), lambda i, e, f, nb, bs: (e, f, 0, 0)),
                  pl.BlockSpec((1, tf, d), lambda i, e, f, nb, bs: (e, f, 0))],
        out_specs=pl.BlockSpec((1, tm, d), lambda i, e, f, nb, bs: (i, 0, 0)),
        scratch_shapes=[pltpu.VMEM((n_rows, d), BF16), pltpu.VMEM((n_rows, 1), F32),
                        pltpu.VMEM((n_rows, d), F32)])
    out = pl.pallas_call(
        functools.partial(_moe_group_body, n_rows=n_rows),
        grid_spec=grid_spec,
        out_shape=jax.ShapeDtypeStruct((nt, tm, d), F32),
        compiler_params=_params(("parallel", "arbitrary", "arbitrary")),
        name="moe_group",
    )(meta[:, :, 0].reshape(-1), meta[:, :, 1].reshape(-1),
      x.reshape(nt, tm, d), h16.reshape(nt, tm, d), route, g_f, chunked(w1), chunked(w3), w2)
    return out.reshape(b, s, d)


def _rope_tables(pos):
    inv = 1.0 / (ROPE_THETA ** (jnp.arange(0, NSA_DH, 2, dtype=F32) / NSA_DH))
    ang = pos.astype(F32)[..., None] * inv
    cos, sin = jnp.cos(ang), jnp.sin(ang)
    reps = LANES // NSA_DH
    return (jnp.tile(jnp.concatenate([cos, cos], -1), (1, 1, reps)),
            jnp.tile(jnp.concatenate([-sin, sin], -1), (1, 1, reps)))


_SPLITS = (DN_QKV, DN_HEADS * DN_DV, DN_HEADS, DN_HEADS, NSA_HEADS * NSA_DH) + (NSA_GROUPS * NSA_DH,) * 6
_OFF = np.concatenate([[0], np.cumsum(_SPLITS)])
_OFF_NG = int(_OFF[-1])
_OFF_MG = _OFF_NG + 3 * NSA_HEADS


def kernel(x, c, positions, w_ada, b_ada, norm_mix, norm_ffn, w_in, conv_w, a_log, dt_bias, dn_norm, cmp_pos, w_cmp1, w_cmp2, q_norm, k_norm, w_oa, w_ob, w_out, w1_dense, w3_dense, w2_dense, w_router, w1_moe, w3_moe, w2_moe):
    b, s, d = x.shape
    depth = w_in.shape[0]
    wdn = DN_QKV + DN_HEADS * DN_DV
    n_small = 2 * DN_HEADS + 3 * NSA_HEADS

    cos_f, sin_s = _rope_tables(positions)
    n_cmp_pad = s // CMP_STRIDE
    cmp_end = jnp.minimum(jnp.arange(n_cmp_pad) * CMP_STRIDE + CMP_LEN - 1, s - 1)
    cos_c, sin_c = _rope_tables(positions[:, cmp_end])

    mod = _ada_mod(c, w_ada, b_ada)

    off_nq = int(_OFF[4])
    w_main = jnp.concatenate([w_in[:, :, 0:wdn], w_in[:, :, _OFF_MG:_OFF_MG + 2 * d],
                              w_in[:, :, off_nq:_OFF_NG]], -1).astype(BF16)
    w_small = jnp.concatenate([w_in[:, :, wdn:wdn + 2 * DN_HEADS], w_in[:, :, _OFF_NG:_OFF_MG],
                               jnp.zeros((depth, d, LANES - n_small), F32)], -1).astype(BF16)
    nsa_col0 = wdn + 2 * d
    n_main = w_main.shape[2]

    w_oa16, w_ob16, w_out16 = w_oa.astype(BF16), w_ob.astype(BF16), w_out.astype(BF16)
    w1d, w3d, w2d = w1_dense.astype(BF16), w3_dense.astype(BF16), w2_dense.astype(BF16)
    w1m, w3m, w2m = w1_moe.astype(BF16), w3_moe.astype(BF16), w2_moe.astype(BF16)

    for l in range(depth):
        sh_m, sc_m, g_m, sh_f, sc_f, g_f = [m.reshape(b, 1, d) for m in jnp.split(mod[l], 6, -1)]
        nw_m = norm_mix[l].reshape(1, d)
        proj = _norm_mod_matmul(x, nw_m, sc_m, sh_m, w_main[l], BF16, tm=1024, tn=n_main // 3)
        small = _norm_mod_matmul(x, nw_m, sc_m, sh_m, w_small[l], F32, tm=1024, tn=LANES)
        a_t = jnp.swapaxes(small[:, :, DN_HEADS:2 * DN_HEADS], 1, 2)
        o_a = _deltanet(proj, small, a_t, conv_w[l], a_log[l], dt_bias[l], dn_norm[l], ts=512)
        qn, ksn, vs, kwn, vw, ck, cv = _nsa_prep(proj, nsa_col0, cos_f, sin_s, q_norm[l],
                                                 k_norm[l, 1], k_norm[l, 2], ts=512)
        kc, vc = _compress(ck, cv, cmp_pos[l], w_cmp1[l], w_cmp2[l], k_norm[l, 0], cos_c, sin_c)
        gates = small[:, :, 2 * DN_HEADS:n_small].reshape(b, s, NSA_GROUPS, 3 * NSA_HPG)
        o_b = _nsa_attention(qn, kc, vc, ksn, vs, kwn, vw, jnp.swapaxes(gates, 1, 2), tq=256, tk=1024)
        x = _mix_out(x, o_a, o_b, proj, wdn // (2 * d), g_m, w_oa16[l], w_ob16[l], w_out16[l], tm=512)
        nw_f = norm_ffn[l].reshape(1, d)
        if l % 2 == 0:
            x = _dense_ffn(x, nw_f, sc_f, sh_f, g_f, w1d[l // 2], w3d[l // 2], w2d[l // 2], tm=1024, tf=512)
        else:
            x = _moe_ffn(x, nw_f, sc_f, sh_f, g_f, w_router[l // 2], w1m[l // 2], w3m[l // 2],
                         w2m[l // 2], tm=1024, tf=896)
    return x
```

```python
import functools

import jax
import jax.numpy as jnp
import numpy as np
from jax import lax
from jax.experimental import pallas as pl
from jax.experimental.pallas import tpu as pltpu

F32 = jnp.float32
BF16 = jnp.bfloat16

DN_HEADS = 8
DN_DK = 64
DN_DV = 64
DN_CHUNK = 64
CONV_W = 4
DN_QKV = DN_HEADS * (2 * DN_DK + DN_DV)
NSA_HEADS = 8
NSA_GROUPS = 2
NSA_HPG = NSA_HEADS // NSA_GROUPS
NSA_DH = 64
CMP_LEN = 32
CMP_STRIDE = 16
SEL_LEN = 64
SEL_SHIFT = 6
SEL_TOP = 16
WINDOW = 512
ROPE_THETA = 10000.0
N_EXPERTS = 8
EPS = 1e-6
NEG = -1e30
FORCE = 1e6
SEL_BIAS = 1e30

LANES = 128
VMEM_LIMIT = 56 * 1024 * 1024


def _sigmoid(x):
    return 1.0 / (1.0 + jnp.exp(-x))


def _silu(x):
    return x * _sigmoid(x)


def _softplus(x):
    return jnp.maximum(x, 0.0) + jnp.log(1.0 + jnp.exp(-jnp.abs(x)))


def _dot(a, b):
    return jnp.dot(a, b, preferred_element_type=F32)


def _dot_nt(a, b):
    return lax.dot_general(a, b, (((1,), (1,)), ((), ())), preferred_element_type=F32)


def _dot_tn(a, b):
    return lax.dot_general(a, b, (((0,), (0,)), ((), ())), preferred_element_type=F32)


def _norm_mod(x, nw, sc, sh):
    y = x * lax.rsqrt(jnp.mean(x * x, -1, keepdims=True) + EPS) * nw
    return y * (1.0 + sc) + sh


def _params(sem):
    return pltpu.CompilerParams(dimension_semantics=sem, vmem_limit_bytes=VMEM_LIMIT)


def _mod_body(c_ref, w_ref, b_ref, o_ref):
    c = c_ref[...]
    o_ref[0] = _dot(_silu(c).astype(BF16), w_ref[0].astype(BF16)) + b_ref[0]


def _ada_mod(c, w_ada, b_ada):
    n_layers, d, n = w_ada.shape
    b = c.shape[0]
    tn = n // 4
    return pl.pallas_call(
        _mod_body,
        grid=(n_layers, n // tn),
        in_specs=[pl.BlockSpec((b, d), lambda l, j: (0, 0)),
                  pl.BlockSpec((1, d, tn), lambda l, j: (l, 0, j)),
                  pl.BlockSpec((1, 1, tn), lambda l, j: (l, 0, j))],
        out_specs=pl.BlockSpec((1, b, tn), lambda l, j: (l, 0, j)),
        out_shape=jax.ShapeDtypeStruct((n_layers, b, n), F32),
        compiler_params=_params(("parallel", "parallel")),
        name="ada_mod",
    )(c, w_ada, b_ada.reshape(n_layers, 1, n))


def _nm_mm_body(x_ref, nw_ref, sc_ref, sh_ref, w_ref, o_ref, h_ref):
    @pl.when(pl.program_id(2) == 0)
    def _():
        h_ref[...] = _norm_mod(x_ref[0], nw_ref[...], sc_ref[0], sh_ref[0]).astype(BF16)

    o_ref[0] = _dot(h_ref[...], w_ref[...]).astype(o_ref.dtype)


def _norm_mod_matmul(x, nw, sc, sh, w, out_dtype, tm, tn):
    b, s, d = x.shape
    n = w.shape[1]
    return pl.pallas_call(
        _nm_mm_body,
        grid=(b, s // tm, n // tn),
        in_specs=[pl.BlockSpec((1, tm, d), lambda bi, i, j: (bi, i, 0)),
                  pl.BlockSpec((1, d), lambda bi, i, j: (0, 0)),
                  pl.BlockSpec((1, 1, d), lambda bi, i, j: (bi, 0, 0)),
                  pl.BlockSpec((1, 1, d), lambda bi, i, j: (bi, 0, 0)),
                  pl.BlockSpec((d, tn), lambda bi, i, j: (0, j))],
        out_specs=pl.BlockSpec((1, tm, tn), lambda bi, i, j: (bi, i, j)),
        out_shape=jax.ShapeDtypeStruct((b, s, n), out_dtype),
        scratch_shapes=[pltpu.VMEM((tm, d), BF16)],
        compiler_params=_params(("parallel", "parallel", "arbitrary")),
        name="in_proj",
    )(x, nw, sc, sh, w)


def _dn_body(x_ref, sm_ref, at_ref, cw_ref, alog_ref, dtb_ref, alogt_ref, dtbt_ref, dnw_ref, o_ref,
             buf_ref, act_ref, gcn_ref, beta_ref, gct_ref, state_ref, *, ts):
    nc = ts // DN_CHUNK
    c64 = DN_CHUNK

    @pl.when(pl.program_id(1) == 0)
    def _():
        buf_ref[0:8, :] = jnp.zeros((8, DN_QKV), F32)
        state_ref[...] = jnp.zeros_like(state_ref)

    for sl in range(DN_QKV // LANES):
        cols = slice(sl * LANES, (sl + 1) * LANES)
        buf_ref[8:ts + 8, cols] = x_ref[0, :, cols].astype(F32)
        y = cw_ref[0:1, cols] * buf_ref[5:5 + ts, cols]
        for j in range(1, CONV_W):
            y = y + cw_ref[j:j + 1, cols] * buf_ref[5 + j:5 + j + ts, cols]
        buf_ref[0:8, cols] = buf_ref[ts:ts + 8, cols]
        act_ref[:, :, cols] = _silu(y).reshape(nc, c64, LANES)

    sm = sm_ref[0]
    beta_ref[...] = _sigmoid(sm).reshape(nc, c64, LANES)
    g = -jnp.exp(alog_ref[...]) * _softplus(sm + dtb_ref[...])
    row = lax.broadcasted_iota(jnp.int32, (ts, LANES), 0) & (c64 - 1)
    for sft in (1, 2, 4, 8, 16, 32):
        g = g + jnp.where(row >= sft, pltpu.roll(g, sft, 0), 0.0)
    gcn_ref[...] = g.reshape(nc, c64, LANES)
    gt = -jnp.exp(alogt_ref[...]) * _softplus(at_ref[0] + dtbt_ref[...])
    lane = lax.broadcasted_iota(jnp.int32, (DN_HEADS, ts), 1) & (c64 - 1)
    for sft in (1, 2, 4, 8, 16, 32):
        gt = gt + jnp.where(lane >= sft, pltpu.roll(gt, sft, 1), 0.0)
    for c in range(nc):
        gct_ref[c] = gt[:, c * c64:(c + 1) * c64]

    ri = lax.broadcasted_iota(jnp.int32, (c64, c64), 0)
    ci = lax.broadcasted_iota(jnp.int32, (c64, c64), 1)
    tril = ri >= ci
    strict = ri > ci
    eye = jnp.where(ri == ci, 1.0, 0.0).astype(F32)
    dnw = dnw_ref[...]

    hs = range(DN_HEADS)
    grp = 2 if nc % 2 == 0 else 1

    def chunk_group(cg, carry):
        items = [(j, h) for j in range(grp) for h in hs]
        n = range(len(items))
        cs = [cg * grp + j for j in range(grp)]
        gcn = [gcn_ref[c] for c in cs]
        bet = [beta_ref[c] for c in cs]
        gct = [gct_ref[c] for c in cs]
        q = [act_ref[cs[j], :, h * DN_DK:(h + 1) * DN_DK] for j, h in items]
        k = [act_ref[cs[j], :, (DN_HEADS + h) * DN_DK:(DN_HEADS + h + 1) * DN_DK] for j, h in items]
        v = [act_ref[cs[j], :, 2 * DN_HEADS * DN_DK + h * DN_DV:2 * DN_HEADS * DN_DK + (h + 1) * DN_DV]
             for j, h in items]
        q = [x * lax.rsqrt(jnp.sum(x * x, -1, keepdims=True) + EPS) * (DN_DK ** -0.5) for x in q]
        k = [x * lax.rsqrt(jnp.sum(x * x, -1, keepdims=True) + EPS) for x in k]
        bcol = [bet[j][:, h:h + 1] for j, h in items]
        gcol = [gcn[j][:, DN_HEADS + h:DN_HEADS + h + 1] for j, h in items]
        grow = [gct[j][h:h + 1, :] for j, h in items]
        decay = [jnp.where(tril, jnp.exp(jnp.where(tril, gcol[i] - grow[i], 0.0)), 0.0) for i in n]
        eg = [jnp.exp(x) for x in gcol]
        glast = [x[c64 - 1:c64, :] for x in gcol]
        kb = [k[i] * bcol[i] for i in n]
        k16 = [x.astype(BF16) for x in k]
        kk = [_dot_nt(kb[i].astype(BF16), k16[i]) for i in n]
        qk = [_dot_nt(q[i].astype(BF16), k16[i]) for i in n]
        a16 = [jnp.where(tril, qk[i] * decay[i], 0.0).astype(BF16) for i in n]
        m = [jnp.where(strict, -(kk[i] * decay[i]), 0.0) for i in n]
        p = [eye + x for x in m]
        for _ in range(5):
            m = [_dot(x, x) for x in m]
            p = [p[i] + _dot(m[i], p[i]) for i in n]
        tinv = [x.astype(BF16) for x in p]
        u = [_dot(tinv[i], (v[i] * bcol[i]).astype(BF16)) for i in n]
        w16 = [_dot(tinv[i], (kb[i] * eg[i]).astype(BF16)).astype(BF16) for i in n]
        qe16 = [(q[i] * eg[i]).astype(BF16) for i in n]
        kd16 = [(k[i] * jnp.exp(glast[i] - gcol[i])).astype(BF16) for i in n]
        egl = [jnp.exp(x) for x in glast]
        for j in range(grp):
            idx = [j * DN_HEADS + h for h in hs]
            r0 = pl.multiple_of(cs[j] * c64, c64)
            st = [state_ref[h] for h in hs]
            st16 = [x.astype(BF16) for x in st]
            ws = [_dot(w16[idx[h]], st16[h]) for h in hs]
            vn16 = [(u[idx[h]] - ws[h]).astype(BF16) for h in hs]
            qs = [_dot(qe16[idx[h]], st16[h]) for h in hs]
            av = [_dot(a16[idx[h]], vn16[h]) for h in hs]
            kv = [_dot_tn(kd16[idx[h]], vn16[h]) for h in hs]
            for h in hs:
                state_ref[h] = st[h] * egl[idx[h]] + kv[h]
                o = qs[h] + av[h]
                on = o * lax.rsqrt(jnp.mean(o * o, -1, keepdims=True) + EPS) * dnw
                z = x_ref[0, pl.ds(r0, c64), DN_QKV + h * DN_DV:DN_QKV + (h + 1) * DN_DV].astype(F32)
                o_ref[0, pl.ds(r0, c64), h * DN_DV:(h + 1) * DN_DV] = (on * _silu(z)).astype(o_ref.dtype)
        return carry

    lax.fori_loop(0, nc // grp, chunk_group, 0)


def _deltanet(proj, small, a_t, conv_w, a_log, dt_bias, dn_norm, ts):
    b, s, _ = proj.shape
    wdn = DN_QKV + DN_HEADS * DN_DV
    pad = jnp.zeros((LANES - 2 * DN_HEADS,), F32)
    alog_row = jnp.concatenate([jnp.zeros((DN_HEADS,), F32), a_log, pad]).reshape(1, LANES)
    dtb_row = jnp.concatenate([jnp.zeros((DN_HEADS,), F32), dt_bias, pad]).reshape(1, LANES)
    nc = ts // DN_CHUNK
    return pl.pallas_call(
        functools.partial(_dn_body, ts=ts),
        grid=(b, s // ts),
        in_specs=[pl.BlockSpec((1, ts, wdn), lambda bi, i: (bi, i, 0)),
                  pl.BlockSpec((1, ts, LANES), lambda bi, i: (bi, i, 0)),
                  pl.BlockSpec((1, DN_HEADS, ts), lambda bi, i: (bi, 0, i)),
                  pl.BlockSpec((CONV_W, DN_QKV), lambda bi, i: (0, 0)),
                  pl.BlockSpec((1, LANES), lambda bi, i: (0, 0)),
                  pl.BlockSpec((1, LANES), lambda bi, i: (0, 0)),
                  pl.BlockSpec((DN_HEADS, 1), lambda bi, i: (0, 0)),
                  pl.BlockSpec((DN_HEADS, 1), lambda bi, i: (0, 0)),
                  pl.BlockSpec((1, DN_DV), lambda bi, i: (0, 0))],
        out_specs=pl.BlockSpec((1, ts, DN_HEADS * DN_DV), lambda bi, i: (bi, i, 0)),
        out_shape=jax.ShapeDtypeStruct((b, s, DN_HEADS * DN_DV), BF16),
        scratch_shapes=[pltpu.VMEM((ts + 8, DN_QKV), F32),
                        pltpu.VMEM((nc, DN_CHUNK, DN_QKV), F32),
                        pltpu.VMEM((nc, DN_CHUNK, LANES), F32),
                        pltpu.VMEM((nc, DN_CHUNK, LANES), F32),
                        pltpu.VMEM((nc, DN_HEADS, DN_CHUNK), F32),
                        pltpu.VMEM((DN_HEADS, DN_DK, DN_DV), F32)],
        compiler_params=_params(("parallel", "arbitrary")),
        name="deltanet",
    )(proj, small, a_t, conv_w, alog_row, dtb_row, a_log.reshape(DN_HEADS, 1),
      dt_bias.reshape(DN_HEADS, 1), dn_norm.reshape(1, DN_DV))


def _seg_ones():
    r = lax.broadcasted_iota(jnp.int32, (LANES, LANES), 0) // NSA_DH
    c = lax.broadcasted_iota(jnp.int32, (LANES, LANES), 1) // NSA_DH
    return jnp.where(r == c, 1.0, 0.0).astype(F32)


V_ROWS = 80
LOG2E = 1.4426950408889634


def _eye16(n):
    r = lax.broadcasted_iota(jnp.int32, (n, n), 0)
    c = lax.broadcasted_iota(jnp.int32, (n, n), 1)
    return jnp.where(r == c, 1.0, 0.0).astype(BF16)


def _transpose16(x16, eye):
    return _dot_tn(x16, eye).astype(BF16)


def _norm_rope(x, w, cos_f, sin_s, seg):
    ms = _dot(x * x, seg) * (1.0 / NSA_DH)
    y = x * lax.rsqrt(ms + EPS) * w
    half = NSA_DH // 2
    lane = lax.broadcasted_iota(jnp.int32, y.shape, 1) & (NSA_DH - 1)
    partner = jnp.where(lane < half, pltpu.roll(y, LANES - half, 1), pltpu.roll(y, half, 1))
    return y * cos_f + partner * sin_s


def _nsa_prep_body(q_ref, ck_ref, cv_ref, sk_ref, sv_ref, wk_ref, wv_ref, cos_ref, sin_ref,
                   qw_ref, skw_ref, wkw_ref,
                   qo_ref, sko_ref, svo_ref, wko_ref, wvo_ref, cko_ref, cvo_ref):
    seg = _seg_ones()
    cos_f = cos_ref[0]
    sin_s = sin_ref[0]
    ts = cos_f.shape[0]
    eye = _eye16(ts)
    for sl in range(NSA_HEADS * NSA_DH // LANES):
        x = q_ref[0, :, sl * LANES:(sl + 1) * LANES].astype(F32)
        y = (_norm_rope(x, qw_ref[...], cos_f, sin_s, seg) * (NSA_DH ** -0.5 * LOG2E)).astype(BF16)
        y_t = _transpose16(y, eye)
        for half in range(2):
            h = 2 * sl + half
            qo_ref[0, h // NSA_HPG, h % NSA_HPG, 0:NSA_DH, :] = y_t[half * NSA_DH:(half + 1) * NSA_DH]
            qo_ref[0, h // NSA_HPG, h % NSA_HPG, NSA_DH:LANES, :] = jnp.zeros((LANES - NSA_DH, ts), BF16)
    sk = _norm_rope(sk_ref[0].astype(F32), skw_ref[...], cos_f, sin_s, seg).astype(BF16)
    wk = _norm_rope(wk_ref[0].astype(F32), wkw_ref[...], cos_f, sin_s, seg).astype(BF16)
    sv_t = _transpose16(sv_ref[0], eye)
    wv_t = _transpose16(wv_ref[0], eye)
    ones_row = jnp.where(lax.broadcasted_iota(jnp.int32, (V_ROWS - NSA_DH, ts), 0) == 0, 1.0, 0.0).astype(BF16)
    blk = (pl.program_id(1) * ts + lax.broadcasted_iota(jnp.int32, (ts, LANES), 0)) >> SEL_SHIFT
    onehot = jnp.where(lax.broadcasted_iota(jnp.int32, (ts, LANES), 1) == blk, SEL_BIAS, 0.0).astype(BF16)
    for g in range(NSA_GROUPS):
        cols = slice(g * NSA_DH, (g + 1) * NSA_DH)
        sko_ref[0, g, :, 0:LANES] = onehot
        sko_ref[0, g, :, LANES:LANES + NSA_DH] = sk[:, cols]
        sko_ref[0, g, :, LANES + NSA_DH:2 * LANES] = jnp.zeros((ts, LANES - NSA_DH), BF16)
        wko_ref[0, g] = wk[:, cols]
        svo_ref[0, g, 0:NSA_DH, :] = sv_t[cols]
        svo_ref[0, g, NSA_DH:V_ROWS, :] = ones_row
        wvo_ref[0, g, 0:NSA_DH, :] = wv_t[cols]
        wvo_ref[0, g, NSA_DH:V_ROWS, :] = ones_row
    cko_ref[0] = ck_ref[0]
    cvo_ref[0] = cv_ref[0]


def _nsa_prep(proj, col0, cos_f, sin_s, q_norm, k_norm_s, k_norm_w, ts):
    b, s, _ = proj.shape
    assert s // SEL_LEN <= LANES
    qw = NSA_HEADS * NSA_DH
    qblk = col0 // qw
    k0 = (col0 + qw) // LANES

    def kspec(i):
        return pl.BlockSpec((1, ts, LANES), lambda bi, t, i=i: (bi, t, k0 + i))

    tile2 = lambda w: jnp.tile(w.reshape(1, NSA_DH), (1, LANES // NSA_DH))
    gshape = jax.ShapeDtypeStruct((b, NSA_GROUPS, s, NSA_DH), BF16)
    gspec = pl.BlockSpec((1, NSA_GROUPS, ts, NSA_DH), lambda bi, t: (bi, 0, t, 0))
    ashape = jax.ShapeDtypeStruct((b, NSA_GROUPS, s, 2 * LANES), BF16)
    aspec = pl.BlockSpec((1, NSA_GROUPS, ts, 2 * LANES), lambda bi, t: (bi, 0, t, 0))
    vshape = jax.ShapeDtypeStruct((b, NSA_GROUPS, V_ROWS, s), BF16)
    vspec = pl.BlockSpec((1, NSA_GROUPS, V_ROWS, ts), lambda bi, t: (bi, 0, 0, t))
    cspec = pl.BlockSpec((1, ts, LANES), lambda bi, t: (bi, t, 0))
    wspec = pl.BlockSpec((1, LANES), lambda bi, t: (0, 0))
    return pl.pallas_call(
        _nsa_prep_body,
        grid=(b, s // ts),
        in_specs=[pl.BlockSpec((1, ts, qw), lambda bi, t: (bi, t, qblk)),
                  kspec(0), kspec(1), kspec(2), kspec(3), kspec(4), kspec(5),
                  cspec, cspec, wspec, wspec, wspec],
        out_specs=[pl.BlockSpec((1, NSA_GROUPS, NSA_HPG, LANES, ts), lambda bi, t: (bi, 0, 0, 0, t)),
                   aspec, vspec, gspec, vspec, cspec, cspec],
        out_shape=[jax.ShapeDtypeStruct((b, NSA_GROUPS, NSA_HPG, LANES, s), BF16),
                   ashape, vshape, gshape, vshape,
                   jax.ShapeDtypeStruct((b, s, LANES), BF16),
                   jax.ShapeDtypeStruct((b, s, LANES), BF16)],
        compiler_params=_params(("parallel", "parallel")),
        name="nsa_prep",
    )(proj, proj, proj, proj, proj, proj, proj, cos_f, sin_s,
      tile2(q_norm), tile2(k_norm_s), tile2(k_norm_w))


def _compress_body(ck_ref, cv_ref, pos_ref, w1a_ref, w1b_ref, w2_ref, kw_ref, cos_ref, sin_ref,
                   ko_ref, vo_ref):
    n = ck_ref.shape[1]
    outs = []
    for which, x_ref in enumerate((ck_ref, cv_ref)):
        x = x_ref[0].astype(F32)
        lo = _dot((x + pos_ref[which, 0:1, :]).astype(BF16), w1a_ref[which])
        hi = _dot((x + pos_ref[which, 1:2, :]).astype(BF16), w1b_ref[which])
        h1 = _silu(lo + pltpu.roll(hi, n - 1, 0))
        outs.append(_dot(h1.astype(BF16), w2_ref[which]))
    kc = _norm_rope(outs[0], kw_ref[...], cos_ref[0], sin_ref[0], _seg_ones()).astype(BF16)
    vc_t = _transpose16(outs[1].astype(BF16), _eye16(n))
    for g in range(NSA_GROUPS):
        ko_ref[0, g] = kc[:, g * NSA_DH:(g + 1) * NSA_DH]
        vo_ref[0, g] = vc_t[g * NSA_DH:(g + 1) * NSA_DH]


def _compress(ck, cv, cmp_pos, w_cmp1, w_cmp2, k_norm_c, cos_c, sin_c):
    b, s, _ = ck.shape
    n = s // CMP_STRIDE
    width = CMP_STRIDE * LANES
    per_row = CMP_LEN // CMP_STRIDE
    eye_g = jnp.eye(NSA_GROUPS, dtype=F32)
    w1 = w_cmp1.reshape(2, per_row, CMP_STRIDE, NSA_DH, NSA_DH)
    w1 = jnp.einsum('khldo,gG->khlgdGo', w1, eye_g).reshape(2, per_row, width, LANES).astype(BF16)
    w2 = jnp.einsum('kdo,gG->kgdGo', w_cmp2, eye_g).reshape(2, LANES, LANES).astype(BF16)
    pos = jnp.broadcast_to(cmp_pos.reshape(2, per_row, CMP_STRIDE, 1, NSA_DH),
                           (2, per_row, CMP_STRIDE, NSA_GROUPS, NSA_DH)).reshape(2, per_row, width)
    kw = jnp.tile(k_norm_c.reshape(1, NSA_DH), (1, NSA_GROUPS))
    full = lambda shp: pl.BlockSpec(shp, lambda bi: (0,) * len(shp))
    bspec = pl.BlockSpec((1, n, width), lambda bi: (bi, 0, 0))
    tspec = pl.BlockSpec((1, n, LANES), lambda bi: (bi, 0, 0))
    ospec = pl.BlockSpec((1, NSA_GROUPS, n, NSA_DH), lambda bi: (bi, 0, 0, 0))
    oshape = jax.ShapeDtypeStruct((b, NSA_GROUPS, n, NSA_DH), BF16)
    return pl.pallas_call(
        _compress_body,
        grid=(b,),
        in_specs=[bspec, bspec, full((2, per_row, width)), full((2, width, LANES)),
                  full((2, width, LANES)), full((2, LANES, LANES)), full((1, LANES)), tspec, tspec],
        out_specs=[ospec, pl.BlockSpec((1, NSA_GROUPS, NSA_DH, n), lambda bi: (bi, 0, 0, 0))],
        out_shape=[oshape, jax.ShapeDtypeStruct((b, NSA_GROUPS, NSA_DH, n), BF16)],
        compiler_params=_params(("parallel",)),
        name="nsa_compress",
    )(ck.reshape(b, n, width), cv.reshape(b, n, width), pos, w1[:, 0], w1[:, 1], w2, kw, cos_c, sin_c)


def _nsa_body(qt_ref, kc_ref, vct_ref, ksa_ref, vst_ref, kw_ref, vwt_ref, gt_ref, o_ref,
              qa_ref, *, tq, tk, n_sel):
    hp = NSA_HPG
    hs = range(hp)
    t0 = pl.program_id(2) * tq
    q_t = [qt_ref[0, 0, j, 0:NSA_DH, :] for j in hs]
    t_row = t0 + lax.broadcasted_iota(jnp.int32, (1, tq), 1)

    kc = kc_ref[0, 0]
    vc_t = vct_ref[0, 0]
    n_cmp = kc.shape[0]
    cmp_end = lax.broadcasted_iota(jnp.int32, (n_cmp, 1), 0) * CMP_STRIDE + (CMP_LEN - 1)
    bias_c = jnp.where(cmp_end <= t_row, 0.0, NEG)
    valid_c = jnp.where(t_row >= CMP_LEN - 1, 1.0, 0.0)
    cs = lax.broadcasted_iota(jnp.int32, (LANES, n_cmp), 1) * CMP_STRIDE
    bs = lax.broadcasted_iota(jnp.int32, (LANES, n_cmp), 0) * SEL_LEN
    overlap_t = jnp.where((cs < bs + SEL_LEN) & (cs + CMP_LEN > bs), 1.0, 0.0).astype(BF16)
    o_c = []
    imp = None
    s_c = [_dot(kc, q_t[j]) for j in hs]
    for j in hs:
        sc = s_c[j] + bias_c
        e_c = jnp.exp2(sc - jnp.max(sc, 0, keepdims=True))
        p16 = (e_c * (valid_c / jnp.sum(e_c, 0, keepdims=True))).astype(BF16)
        o_c.append(_dot(vc_t, p16))
        part = _dot(overlap_t, p16)
        imp = part if imp is None else imp + part

    jb = lax.broadcasted_iota(jnp.int32, (LANES, 1), 0)
    cur = t_row >> SEL_SHIFT
    forced = (jb == 0) | (jb == cur) | (jb == cur - 1)
    imp = jnp.where(forced, FORCE, jnp.where(jb * SEL_LEN <= t_row, imp, -FORCE))
    imp = jnp.where(jb < n_sel, imp, -jnp.inf)
    jbf = jb.astype(F32)
    sel_t = jnp.zeros((LANES, tq), F32)
    for _ in range(min(SEL_TOP, n_sel)):
        mx = jnp.max(imp, 0, keepdims=True)
        first = jnp.min(jnp.where(imp == mx, jbf, float(LANES)), 0, keepdims=True)
        hit = jbf == first
        sel_t = jnp.where(hit, 1.0, sel_t)
        imp = jnp.where(hit, -jnp.inf, imp)
    selm1_t = (sel_t - 1.0).astype(BF16)
    for j in hs:
        qa_ref[j, 0:LANES, :] = selm1_t
        qa_ref[j, LANES:2 * LANES, :] = qt_ref[0, 0, j]

    def key_tile(kt, diagonal, state):
        k0 = pl.multiple_of(kt * tk, tk)
        m, acc = list(state[0]), list(state[1])
        k_aug = ksa_ref[0, 0, pl.ds(k0, tk), :]
        v_t = vst_ref[0, 0, :, pl.ds(k0, tk)]
        if diagonal:
            tok = k0 + lax.broadcasted_iota(jnp.int32, (tk, 1), 0)
            bias = jnp.where(tok <= t_row, 0.0, NEG)
        s = [_dot(k_aug, qa_ref[j]) for j in hs]
        for j in hs:
            sj = s[j] + bias if diagonal else s[j]
            m_new = jnp.maximum(m[j], jnp.max(sj, 0, keepdims=True))
            alpha = jnp.exp2(m[j] - m_new)
            e16 = jnp.exp2((sj - m_new).astype(BF16))
            acc[j] = alpha * acc[j] + _dot(v_t, e16)
            m[j] = m_new
        return tuple(m), tuple(acc)

    state0 = (tuple(jnp.full((1, tq), NEG, F32) for _ in hs),
              tuple(jnp.zeros((V_ROWS, tq), F32) for _ in hs))
    k_diag = t0 // tk
    state = lax.fori_loop(0, k_diag, lambda kt, st: key_tile(kt, False, st), state0)
    _, acc = key_tile(k_diag, True, state)
    o_s = [acc[j][0:NSA_DH] / jnp.maximum(acc[j][NSA_DH:NSA_DH + 1], 1e-30) for j in hs]

    wl = WINDOW + tq
    w0 = pl.multiple_of(jnp.maximum(t0 - WINDOW, 0), tq)
    dist = t_row - (w0 + lax.broadcasted_iota(jnp.int32, (wl, 1), 0))
    bias_w = jnp.where((dist >= 0) & (dist < WINDOW), 0.0, NEG)
    k_w = kw_ref[0, 0, pl.ds(w0, wl), :]
    vw_t = vwt_ref[0, 0, :, pl.ds(w0, wl)]
    o_w = []
    s_w = [_dot(k_w, q_t[j]) for j in hs]
    for j in hs:
        sw = s_w[j] + bias_w
        e16 = jnp.exp2((sw - jnp.max(sw, 0, keepdims=True)).astype(BF16))
        oa = _dot(vw_t, e16)
        o_w.append(oa[0:NSA_DH] / oa[NSA_DH:NSA_DH + 1])

    gates = _sigmoid(gt_ref[0, 0])
    for j in range(hp):
        o = (gates[3 * j:3 * j + 1] * o_c[j] + gates[3 * j + 1:3 * j + 2] * o_s[j]
             + gates[3 * j + 2:3 * j + 3] * o_w[j])
        o_ref[0, j * NSA_DH:(j + 1) * NSA_DH, :] = o.astype(o_ref.dtype)


def _nsa_attention(q_t, kc, vc_t, ksa, vs_t, kw, vw_t, gates_t, tq, tk):
    b, g, hp, _, s = q_t.shape
    dh = NSA_DH
    n_cmp = kc.shape[2]
    assert s >= WINDOW + tq and WINDOW % tq == 0 and s % tk == 0 and tk % tq == 0 and tq % LANES == 0
    vt_spec = pl.BlockSpec((1, 1, V_ROWS, s), lambda bi, gi, i: (bi, gi, 0, 0))
    return pl.pallas_call(
        functools.partial(_nsa_body, tq=tq, tk=tk, n_sel=s // SEL_LEN),
        grid=(b, g, s // tq),
        in_specs=[pl.BlockSpec((1, 1, hp, LANES, tq), lambda bi, gi, i: (bi, gi, 0, 0, i)),
                  pl.BlockSpec((1, 1, n_cmp, dh), lambda bi, gi, i: (bi, gi, 0, 0)),
                  pl.BlockSpec((1, 1, dh, n_cmp), lambda bi, gi, i: (bi, gi, 0, 0)),
                  pl.BlockSpec((1, 1, s, 2 * LANES), lambda bi, gi, i: (bi, gi, 0, 0)),
                  vt_spec,
                  pl.BlockSpec((1, 1, s, dh), lambda bi, gi, i: (bi, gi, 0, 0)),
                  vt_spec,
                  pl.BlockSpec((1, 1, 3 * hp, tq), lambda bi, gi, i: (bi, gi, 0, i))],
        out_specs=pl.BlockSpec((1, hp * dh, tq), lambda bi, gi, i: (bi, gi, i)),
        out_shape=jax.ShapeDtypeStruct((b, g * hp * dh, s), BF16),
        scratch_shapes=[pltpu.VMEM((hp, 2 * LANES, tq), BF16)],
        compiler_params=_params(("parallel", "parallel", "arbitrary")),
        name="nsa_attention",
    )(q_t, kc, vc_t, ksa, vs_t, kw, vw_t, gates_t)


def _mix_out_body(x_ref, oa_ref, ob_ref, mg_ref, gm_ref, woa_ref, wob_ref, wout_ref, o_ref):
    d = x_ref.shape[2]
    y_a = _dot(oa_ref[0], woa_ref[...])
    y_b = _dot_tn(ob_ref[0], wob_ref[...])
    merged = (_sigmoid(mg_ref[0, :, 0:d].astype(F32)) * y_a
              + _sigmoid(mg_ref[0, :, d:2 * d].astype(F32)) * y_b)
    o_ref[0] = x_ref[0] + gm_ref[0] * _dot(merged.astype(BF16), wout_ref[...])


def _mix_out(x, o_a, o_b, proj, mg_blk, g_m, w_oa, w_ob, w_out, tm):
    b, s, d = x.shape
    full = lambda a: pl.BlockSpec(a.shape, lambda bi, i: (0, 0))
    return pl.pallas_call(
        _mix_out_body,
        grid=(b, s // tm),
        in_specs=[pl.BlockSpec((1, tm, d), lambda bi, i: (bi, i, 0)),
                  pl.BlockSpec((1, tm, o_a.shape[2]), lambda bi, i: (bi, i, 0)),
                  pl.BlockSpec((1, o_b.shape[1], tm), lambda bi, i: (bi, 0, i)),
                  pl.BlockSpec((1, tm, 2 * d), lambda bi, i: (bi, i, mg_blk)),
                  pl.BlockSpec((1, 1, d), lambda bi, i: (bi, 0, 0)),
                  full(w_oa), full(w_ob), full(w_out)],
        out_specs=pl.BlockSpec((1, tm, d), lambda bi, i: (bi, i, 0)),
        out_shape=jax.ShapeDtypeStruct(x.shape, F32),
        compiler_params=_params(("parallel", "parallel")),
        name="mix_out",
    )(x, o_a, o_b, proj, g_m, w_oa, w_ob, w_out)


def _ffn_body(x_ref, nw_ref, sc_ref, sh_ref, gf_ref, w1_ref, w3_ref, w2_ref, o_ref, h_ref, acc_ref):
    f = pl.program_id(2)

    @pl.when(f == 0)
    def _():
        h_ref[...] = _norm_mod(x_ref[0], nw_ref[...], sc_ref[0], sh_ref[0]).astype(BF16)
        acc_ref[...] = jnp.zeros_like(acc_ref)

    h = h_ref[...]
    t = _silu(_dot(h, w1_ref[...])) * _dot(h, w3_ref[...])
    acc_ref[...] += _dot(t.astype(BF16), w2_ref[...])

    @pl.when(f == pl.num_programs(2) - 1)
    def _():
        o_ref[0] = x_ref[0] + gf_ref[0] * acc_ref[...]


def _dense_ffn(x, nw, sc, sh, g_f, w1, w3, w2, tm, tf):
    b, s, d = x.shape
    ff = w1.shape[1]
    vec = pl.BlockSpec((1, 1, d), lambda bi, i, f: (bi, 0, 0))
    return pl.pallas_call(
        _ffn_body,
        grid=(b, s // tm, ff // tf),
        in_specs=[pl.BlockSpec((1, tm, d), lambda bi, i, f: (bi, i, 0)),
                  pl.BlockSpec((1, d), lambda bi, i, f: (0, 0)), vec, vec, vec,
                  pl.BlockSpec((d, tf), lambda bi, i, f: (0, f)),
                  pl.BlockSpec((d, tf), lambda bi, i, f: (0, f)),
                  pl.BlockSpec((tf, d), lambda bi, i, f: (f, 0))],
        out_specs=pl.BlockSpec((1, tm, d), lambda bi, i, f: (bi, i, 0)),
        out_shape=jax.ShapeDtypeStruct(x.shape, F32),
        scratch_shapes=[pltpu.VMEM((tm, d), BF16), pltpu.VMEM((tm, d), F32)],
        compiler_params=_params(("parallel", "parallel", "arbitrary")),
        name="dense_ffn",
    )(x, nw, sc, sh, g_f, w1, w3, w2)


MOE_ROWS = 128


def _moe_route_body(x_ref, nw_ref, sc_ref, sh_ref, wr_ref, h_ref, route_ref, meta_ref):
    tm = x_ref.shape[1]
    h16 = _norm_mod(x_ref[0], nw_ref[...], sc_ref[0], sh_ref[0]).astype(BF16)
    h_ref[0] = h16
    logits = _dot_nt(wr_ref[...], h16)
    ef = lax.broadcasted_iota(jnp.int32, (N_EXPERTS, tm), 0).astype(F32)
    m1 = jnp.max(logits, 0, keepdims=True)
    i1 = jnp.min(jnp.where(logits == m1, ef, float(N_EXPERTS)), 0, keepdims=True)
    rest = jnp.where(ef == i1, -jnp.inf, logits)
    m2 = jnp.max(rest, 0, keepdims=True)
    i2 = jnp.min(jnp.where(rest == m2, ef, float(N_EXPERTS)), 0, keepdims=True)
    e2 = jnp.exp(m2 - m1)
    oh1 = jnp.where(ef == i1, 1.0, 0.0)
    oh2 = jnp.where(ef == i2, 1.0, 0.0)
    member = oh1 + oh2
    lane = lax.broadcasted_iota(jnp.int32, (N_EXPERTS, tm), 1)
    csum = member
    sft = 1
    while sft < tm:
        csum = csum + jnp.where(lane >= sft, pltpu.roll(csum, sft, 1), 0.0)
        sft *= 2
    count = jnp.max(csum, 1, keepdims=True)
    nblk = jnp.floor((count + (MOE_ROWS - 1)) * (1.0 / MOE_ROWS))
    nblk_b = jnp.broadcast_to(nblk, (N_EXPERTS, LANES))
    row = lax.broadcasted_iota(jnp.int32, (N_EXPERTS, LANES), 0)
    bsum = nblk_b
    for sft in (1, 2, 4):
        bsum = bsum + jnp.where(row >= sft, pltpu.roll(bsum, sft, 0), 0.0)
    bstart = bsum - nblk_b
    slot = bstart[:, 0:1] * MOE_ROWS + (csum - member)
    rrow = lax.broadcasted_iota(jnp.int32, (8, tm), 0)
    route_ref[0] = jnp.where(
        rrow == 0, jnp.sum(oh1 * slot, 0, keepdims=True),
        jnp.where(rrow == 1, jnp.sum(oh2 * slot, 0, keepdims=True),
                  jnp.where(rrow == 2, 1.0 / (1.0 + e2), jnp.where(rrow == 3, e2 / (1.0 + e2), 0.0))))
    col = lax.broadcasted_iota(jnp.int32, (N_EXPERTS, LANES), 1)
    meta_ref[0] = jnp.where(col == 0, nblk_b, jnp.where(col == 1, bstart, 0.0)).astype(jnp.int32)


def _moe_group_body(nblk_ref, bstart_ref, x_ref, h_ref, route_ref, gf_ref, w1_ref, w3_ref, w2_ref, o_ref,
                    hb_ref, cw_ref, acc_ref, *, n_rows):
    i = pl.program_id(0)
    e = pl.program_id(1)
    f = pl.program_id(2)
    tm = x_ref.shape[1]
    nb = nblk_ref[i * N_EXPERTS + e]
    b0 = bstart_ref[i * N_EXPERTS + e]
    slot1 = route_ref[0, 0:1, :]
    slot2 = route_ref[0, 1:2, :]

    def hits(r0, rows):
        rr = (r0 + lax.broadcasted_iota(jnp.int32, (rows, 1), 0)).astype(F32)
        return rr == slot1, rr == slot2

    def expert_rows(blk, n_blk, first):
        r0 = pl.multiple_of((b0 + blk) * MOE_ROWS, MOE_ROWS)
        rows = pl.ds(r0, n_blk * MOE_ROWS)
        if first:
            hit1, hit2 = hits(r0, n_blk * MOE_ROWS)
            gather = jnp.where(hit1, 1.0, jnp.where(hit2, 1.0, 0.0)).astype(BF16)
            hb_ref[rows, :] = _dot(gather, h_ref[0]).astype(BF16)
            cw_ref[rows, :] = jnp.sum(jnp.where(hit1, route_ref[0, 2:3, :], 0.0)
                                      + jnp.where(hit2, route_ref[0, 3:4, :], 0.0), -1, keepdims=True)
        hb = hb_ref[rows, :]
        t = _silu(_dot(hb, w1_ref[0, 0])) * _dot(hb, w3_ref[0, 0]) * cw_ref[rows, :]
        y = _dot(t.astype(BF16), w2_ref[0])
        if first:
            acc_ref[rows, :] = y
        else:
            acc_ref[rows, :] += y

    def expert_all(first):
        lax.fori_loop(0, nb // 2, lambda k, c: (expert_rows(2 * k, 2, first), c)[1], 0)

        @pl.when(nb % 2 == 1)
        def _():
            expert_rows(nb - 1, 1, first)

    @pl.when(f == 0)
    def _():
        expert_all(True)

    @pl.when(f != 0)
    def _():
        expert_all(False)

    @pl.when((e == pl.num_programs(1) - 1) & (f == pl.num_programs(2) - 1))
    def _():
        def clear(k, c):
            acc_ref[pl.ds(pl.multiple_of(k * MOE_ROWS, MOE_ROWS), MOE_ROWS), :] = jnp.zeros(
                (MOE_ROWS, acc_ref.shape[1]), F32)
            return c

        lax.fori_loop(b0 + nb, n_rows // MOE_ROWS, clear, 0)
        chunk = 4 * MOE_ROWS
        for kc in range(n_rows // chunk):
            hit1, hit2 = hits(kc * chunk, chunk)
            scatter = jnp.where(hit1, 1.0, jnp.where(hit2, 1.0, 0.0)).astype(BF16)
            y = _dot_tn(scatter, acc_ref[kc * chunk:(kc + 1) * chunk, :].astype(BF16))
            if kc == 0:
                o_ref[0] = y
            else:
                o_ref[0] += y
        o_ref[0] = x_ref[0] + gf_ref[0] * o_ref[0]


def _moe_ffn(x, nw, sc, sh, g_f, w_router, w1, w3, w2, tm, tf):
    b, s, d = x.shape
    n_e, _, ff = w1.shape
    assert n_e == N_EXPERTS
    tiles_b = s // tm
    nt = b * tiles_b
    vec = pl.BlockSpec((1, 1, d), lambda bi, i: (bi, 0, 0))
    h16, route, meta = pl.pallas_call(
        _moe_route_body,
        grid=(b, tiles_b),
        in_specs=[pl.BlockSpec((1, tm, d), lambda bi, i: (bi, i, 0)),
                  pl.BlockSpec((1, d), lambda bi, i: (0, 0)), vec, vec,
                  pl.BlockSpec((n_e, d), lambda bi, i: (0, 0))],
        out_specs=[pl.BlockSpec((1, tm, d), lambda bi, i: (bi, i, 0)),
                   pl.BlockSpec((1, 8, tm), lambda bi, i: (bi * tiles_b + i, 0, 0)),
                   pl.BlockSpec((1, n_e, LANES), lambda bi, i: (bi * tiles_b + i, 0, 0))],
        out_shape=[jax.ShapeDtypeStruct((b, s, d), BF16),
                   jax.ShapeDtypeStruct((nt, 8, tm), F32),
                   jax.ShapeDtypeStruct((nt, n_e, LANES), jnp.int32)],
        compiler_params=_params(("parallel", "parallel")),
        name="moe_route",
    )(x, nw, sc, sh, w_router.T.astype(BF16))
    chunked = lambda w: w.reshape(n_e, d, ff // tf, tf).transpose(0, 2, 1, 3)
    n_rows = -(-(2 * tm + n_e * (MOE_ROWS - 1)) // (4 * MOE_ROWS)) * (4 * MOE_ROWS)
    grid_spec = pltpu.PrefetchScalarGridSpec(
        num_scalar_prefetch=2,
        grid=(nt, n_e, ff // tf),
        in_specs=[pl.BlockSpec((1, tm, d), lambda i, e, f, nb, bs: (i, 0, 0)),
                  pl.BlockSpec((1, tm, d), lambda i, e, f, nb, bs: (i, 0, 0)),
                  pl.BlockSpec((1, 8, tm), lambda i, e, f, nb, bs: (i, 0, 0)),
                  pl.BlockSpec((1, 1, d), lambda i, e, f, nb, bs: (i // tiles_b, 0, 0)),
                  pl.BlockSpec((1, 1, d, tf), lambda i, e, f, nb, bs: (e, f, 0, 0)),
                  pl.BlockSpec((1, 1, d, tf), lambda i, e, f, nb, bs: (e, f, 0, 0)),
                  pl.BlockSpec((1, tf, d), lambda i, e, f, nb, bs: (e, f, 0))],
        out_specs=pl.BlockSpec((1, tm, d), lambda i, e, f, nb, bs: (i, 0, 0)),
        scratch_shapes=[pltpu.VMEM((n_rows, d), BF16), pltpu.VMEM((n_rows, 1), F32),
                        pltpu.VMEM((n_rows, d), F32)])
    out = pl.pallas_call(
        functools.partial(_moe_group_body, n_rows=n_rows),
        grid_spec=grid_spec,
        out_shape=jax.ShapeDtypeStruct((nt, tm, d), F32),
        compiler_params=_params(("parallel", "arbitrary", "arbitrary")),
        name="moe_group",
    )(meta[:, :, 0].reshape(-1), meta[:, :, 1].reshape(-1),
      x.reshape(nt, tm, d), h16.reshape(nt, tm, d), route, g_f, chunked(w1), chunked(w3), w2)
    return out.reshape(b, s, d)


def _rope_tables(pos):
    inv = 1.0 / (ROPE_THETA ** (jnp.arange(0, NSA_DH, 2, dtype=F32) / NSA_DH))
    ang = pos.astype(F32)[..., None] * inv
    cos, sin = jnp.cos(ang), jnp.sin(ang)
    reps = LANES // NSA_DH
    return (jnp.tile(jnp.concatenate([cos, cos], -1), (1, 1, reps)),
            jnp.tile(jnp.concatenate([-sin, sin], -1), (1, 1, reps)))


_SPLITS = (DN_QKV, DN_HEADS * DN_DV, DN_HEADS, DN_HEADS, NSA_HEADS * NSA_DH) + (NSA_GROUPS * NSA_DH,) * 6
_OFF = np.concatenate([[0], np.cumsum(_SPLITS)])
_OFF_NG = int(_OFF[-1])
_OFF_MG = _OFF_NG + 3 * NSA_HEADS


def kernel(x, c, positions, w_ada, b_ada, norm_mix, norm_ffn, w_in, conv_w, a_log, dt_bias, dn_norm, cmp_pos, w_cmp1, w_cmp2, q_norm, k_norm, w_oa, w_ob, w_out, w1_dense, w3_dense, w2_dense, w_router, w1_moe, w3_moe, w2_moe):
    b, s, d = x.shape
    depth = w_in.shape[0]
    wdn = DN_QKV + DN_HEADS * DN_DV
    n_small = 2 * DN_HEADS + 3 * NSA_HEADS

    cos_f, sin_s = _rope_tables(positions)
    n_cmp_pad = s // CMP_STRIDE
    cmp_end = jnp.minimum(jnp.arange(n_cmp_pad) * CMP_STRIDE + CMP_LEN - 1, s - 1)
    cos_c, sin_c = _rope_tables(positions[:, cmp_end])

    mod = _ada_mod(c, w_ada, b_ada)

    off_nq = int(_OFF[4])
    w_main = jnp.concatenate([w_in[:, :, 0:wdn], w_in[:, :, _OFF_MG:_OFF_MG + 2 * d],
                              w_in[:, :, off_nq:_OFF_NG]], -1).astype(BF16)
    w_small = jnp.concatenate([w_in[:, :, wdn:wdn + 2 * DN_HEADS], w_in[:, :, _OFF_NG:_OFF_MG],
                               jnp.zeros((depth, d, LANES - n_small), F32)], -1).astype(BF16)
    nsa_col0 = wdn + 2 * d
    n_main = w_main.shape[2]

    w_oa16, w_ob16, w_out16 = w_oa.astype(BF16), w_ob.astype(BF16), w_out.astype(BF16)
    w1d, w3d, w2d = w1_dense.astype(BF16), w3_dense.astype(BF16), w2_dense.astype(BF16)
    w1m, w3m, w2m = w1_moe.astype(BF16), w3_moe.astype(BF16), w2_moe.astype(BF16)

    for l in range(depth):
        sh_m, sc_m, g_m, sh_f, sc_f, g_f = [m.reshape(b, 1, d) for m in jnp.split(mod[l], 6, -1)]
        nw_m = norm_mix[l].reshape(1, d)
        proj = _norm_mod_matmul(x, nw_m, sc_m, sh_m, w_main[l], BF16, tm=1024, tn=n_main // 3)
        small = _norm_mod_matmul(x, nw_m, sc_m, sh_m, w_small[l], F32, tm=1024, tn=LANES)
        a_t = jnp.swapaxes(small[:, :, DN_HEADS:2 * DN_HEADS], 1, 2)
        o_a = _deltanet(proj, small, a_t, conv_w[l], a_log[l], dt_bias[l], dn_norm[l], ts=512)
        qn, ksn, vs, kwn, vw, ck, cv = _nsa_prep(proj, nsa_col0, cos_f, sin_s, q_norm[l],
                                                 k_norm[l, 1], k_norm[l, 2], ts=512)
        kc, vc = _compress(ck, cv, cmp_pos[l], w_cmp1[l], w_cmp2[l], k_norm[l, 0], cos_c, sin_c)
        gates_t = jnp.swapaxes(small[:, :, 2 * DN_HEADS:n_small], 1, 2).reshape(b, NSA_GROUPS, 3 * NSA_HPG, s)
        o_b = _nsa_attention(qn, kc, vc, ksn, vs, kwn, vw, gates_t, tq=256, tk=1024)
        x = _mix_out(x, o_a, o_b, proj, wdn // (2 * d), g_m, w_oa16[l], w_ob16[l], w_out16[l], tm=512)
        nw_f = norm_ffn[l].reshape(1, d)
        if l % 2 == 0:
            x = _dense_ffn(x, nw_f, sc_f, sh_f, g_f, w1d[l // 2], w3d[l // 2], w2d[l // 2], tm=1024, tf=512)
        else:
            x = _moe_ffn(x, nw_f, sc_f, sh_f, g_f, w_router[l // 2], w1m[l // 2], w3m[l // 2],
                         w2m[l // 2], tm=1024, tf=896)
    return x
```

```python
import functools

import jax
import jax.numpy as jnp
import numpy as np
from jax import lax
from jax.experimental import pallas as pl
from jax.experimental.pallas import tpu as pltpu

F32 = jnp.float32
BF16 = jnp.bfloat16

DN_HEADS = 8
DN_DK = 64
DN_DV = 64
DN_CHUNK = 64
CONV_W = 4
DN_QKV = DN_HEADS * (2 * DN_DK + DN_DV)
NSA_HEADS = 8
NSA_GROUPS = 2
NSA_HPG = NSA_HEADS // NSA_GROUPS
NSA_DH = 64
CMP_LEN = 32
CMP_STRIDE = 16
SEL_LEN = 64
SEL_SHIFT = 6
SEL_TOP = 16
WINDOW = 512
ROPE_THETA = 10000.0
N_EXPERTS = 8
EPS = 1e-6
NEG = -1e30
FORCE = 1e6
SEL_BIAS = 1e30

LANES = 128
VMEM_LIMIT = 56 * 1024 * 1024


def _sigmoid(x):
    return 1.0 / (1.0 + jnp.exp(-x))


def _silu(x):
    return x * _sigmoid(x)


def _softplus(x):
    return jnp.maximum(x, 0.0) + jnp.log(1.0 + jnp.exp(-jnp.abs(x)))


def _dot(a, b):
    return jnp.dot(a, b, preferred_element_type=F32)


def _dot_nt(a, b):
    return lax.dot_general(a, b, (((1,), (1,)), ((), ())), preferred_element_type=F32)


def _dot_tn(a, b):
    return lax.dot_general(a, b, (((0,), (0,)), ((), ())), preferred_element_type=F32)


def _norm_mod(x, nw, sc, sh):
    y = x * lax.rsqrt(jnp.mean(x * x, -1, keepdims=True) + EPS) * nw
    return y * (1.0 + sc) + sh


def _params(sem):
    return pltpu.CompilerParams(dimension_semantics=sem, vmem_limit_bytes=VMEM_LIMIT)


def _mod_body(c_ref, w_ref, b_ref, o_ref):
    c = c_ref[...]
    o_ref[0] = _dot(_silu(c).astype(BF16), w_ref[0].astype(BF16)) + b_ref[0]


def _ada_mod(c, w_ada, b_ada):
    n_layers, d, n = w_ada.shape
    b = c.shape[0]
    tn = n // 4
    return pl.pallas_call(
        _mod_body,
        grid=(n_layers, n // tn),
        in_specs=[pl.BlockSpec((b, d), lambda l, j: (0, 0)),
                  pl.BlockSpec((1, d, tn), lambda l, j: (l, 0, j)),
                  pl.BlockSpec((1, 1, tn), lambda l, j: (l, 0, j))],
        out_specs=pl.BlockSpec((1, b, tn), lambda l, j: (l, 0, j)),
        out_shape=jax.ShapeDtypeStruct((n_layers, b, n), F32),
        compiler_params=_params(("parallel", "parallel")),
        name="ada_mod",
    )(c, w_ada, b_ada.reshape(n_layers, 1, n))


def _nm_mm_body(x_ref, nw_ref, sc_ref, sh_ref, w_ref, o_ref, h_ref):
    @pl.when(pl.program_id(2) == 0)
    def _():
        h_ref[...] = _norm_mod(x_ref[0], nw_ref[...], sc_ref[0], sh_ref[0]).astype(BF16)

    o_ref[0] = _dot(h_ref[...], w_ref[...]).astype(o_ref.dtype)


def _norm_mod_matmul(x, nw, sc, sh, w, out_dtype, tm, tn):
    b, s, d = x.shape
    n = w.shape[1]
    return pl.pallas_call(
        _nm_mm_body,
        grid=(b, s // tm, n // tn),
        in_specs=[pl.BlockSpec((1, tm, d), lambda bi, i, j: (bi, i, 0)),
                  pl.BlockSpec((1, d), lambda bi, i, j: (0, 0)),
                  pl.BlockSpec((1, 1, d), lambda bi, i, j: (bi, 0, 0)),
                  pl.BlockSpec((1, 1, d), lambda bi, i, j: (bi, 0, 0)),
                  pl.BlockSpec((d, tn), lambda bi, i, j: (0, j))],
        out_specs=pl.BlockSpec((1, tm, tn), lambda bi, i, j: (bi, i, j)),
        out_shape=jax.ShapeDtypeStruct((b, s, n), out_dtype),
        scratch_shapes=[pltpu.VMEM((tm, d), BF16)],
        compiler_params=_params(("parallel", "parallel", "arbitrary")),
        name="in_proj",
    )(x, nw, sc, sh, w)


def _dn_body(x_ref, sm_ref, at_ref, cw_ref, alog_ref, dtb_ref, alogt_ref, dtbt_ref, dnw_ref, o_ref,
             buf_ref, act_ref, gcn_ref, beta_ref, gct_ref, state_ref, *, ts):
    nc = ts // DN_CHUNK
    c64 = DN_CHUNK

    @pl.when(pl.program_id(1) == 0)
    def _():
        buf_ref[0:8, :] = jnp.zeros((8, DN_QKV), F32)
        state_ref[...] = jnp.zeros_like(state_ref)

    for sl in range(DN_QKV // LANES):
        cols = slice(sl * LANES, (sl + 1) * LANES)
        buf_ref[8:ts + 8, cols] = x_ref[0, :, cols].astype(F32)
        y = cw_ref[0:1, cols] * buf_ref[5:5 + ts, cols]
        for j in range(1, CONV_W):
            y = y + cw_ref[j:j + 1, cols] * buf_ref[5 + j:5 + j + ts, cols]
        buf_ref[0:8, cols] = buf_ref[ts:ts + 8, cols]
        act_ref[:, :, cols] = _silu(y).reshape(nc, c64, LANES)

    sm = sm_ref[0]
    beta_ref[...] = _sigmoid(sm).reshape(nc, c64, LANES)
    g = -jnp.exp(alog_ref[...]) * _softplus(sm + dtb_ref[...])
    row = lax.broadcasted_iota(jnp.int32, (ts, LANES), 0) & (c64 - 1)
    for sft in (1, 2, 4, 8, 16, 32):
        g = g + jnp.where(row >= sft, pltpu.roll(g, sft, 0), 0.0)
    gcn_ref[...] = g.reshape(nc, c64, LANES)
    gt = -jnp.exp(alogt_ref[...]) * _softplus(at_ref[0] + dtbt_ref[...])
    lane = lax.broadcasted_iota(jnp.int32, (DN_HEADS, ts), 1) & (c64 - 1)
    for sft in (1, 2, 4, 8, 16, 32):
        gt = gt + jnp.where(lane >= sft, pltpu.roll(gt, sft, 1), 0.0)
    for c in range(nc):
        gct_ref[c] = gt[:, c * c64:(c + 1) * c64]

    ri = lax.broadcasted_iota(jnp.int32, (c64, c64), 0)
    ci = lax.broadcasted_iota(jnp.int32, (c64, c64), 1)
    tril = ri >= ci
    strict = ri > ci
    eye = jnp.where(ri == ci, 1.0, 0.0).astype(F32)
    dnw = dnw_ref[...]

    hs = range(DN_HEADS)
    grp = 2 if nc % 2 == 0 else 1

    def chunk_group(cg, carry):
        items = [(j, h) for j in range(grp) for h in hs]
        n = range(len(items))
        cs = [cg * grp + j for j in range(grp)]
        gcn = [gcn_ref[c] for c in cs]
        bet = [beta_ref[c] for c in cs]
        gct = [gct_ref[c] for c in cs]
        q = [act_ref[cs[j], :, h * DN_DK:(h + 1) * DN_DK] for j, h in items]
        k = [act_ref[cs[j], :, (DN_HEADS + h) * DN_DK:(DN_HEADS + h + 1) * DN_DK] for j, h in items]
        v = [act_ref[cs[j], :, 2 * DN_HEADS * DN_DK + h * DN_DV:2 * DN_HEADS * DN_DK + (h + 1) * DN_DV]
             for j, h in items]
        q = [x * lax.rsqrt(jnp.sum(x * x, -1, keepdims=True) + EPS) * (DN_DK ** -0.5) for x in q]
        k = [x * lax.rsqrt(jnp.sum(x * x, -1, keepdims=True) + EPS) for x in k]
        bcol = [bet[j][:, h:h + 1] for j, h in items]
        gcol = [gcn[j][:, DN_HEADS + h:DN_HEADS + h + 1] for j, h in items]
        grow = [gct[j][h:h + 1, :] for j, h in items]
        decay = [jnp.where(tril, jnp.exp(jnp.where(tril, gcol[i] - grow[i], 0.0)), 0.0) for i in n]
        eg = [jnp.exp(x) for x in gcol]
        glast = [x[c64 - 1:c64, :] for x in gcol]
        kb = [k[i] * bcol[i] for i in n]
        k16 = [x.astype(BF16) for x in k]
        kk = [_dot_nt(kb[i].astype(BF16), k16[i]) for i in n]
        qk = [_dot_nt(q[i].astype(BF16), k16[i]) for i in n]
        a16 = [jnp.where(tril, qk[i] * decay[i], 0.0).astype(BF16) for i in n]
        m = [jnp.where(strict, -(kk[i] * decay[i]), 0.0) for i in n]
        p = [eye + x for x in m]
        for _ in range(5):
            m = [_dot(x, x) for x in m]
            p = [p[i] + _dot(m[i], p[i]) for i in n]
        tinv = [x.astype(BF16) for x in p]
        u = [_dot(tinv[i], (v[i] * bcol[i]).astype(BF16)) for i in n]
        w16 = [_dot(tinv[i], (kb[i] * eg[i]).astype(BF16)).astype(BF16) for i in n]
        qe16 = [(q[i] * eg[i]).astype(BF16) for i in n]
        kd16 = [(k[i] * jnp.exp(glast[i] - gcol[i])).astype(BF16) for i in n]
        egl = [jnp.exp(x) for x in glast]
        for j in range(grp):
            idx = [j * DN_HEADS + h for h in hs]
            r0 = pl.multiple_of(cs[j] * c64, c64)
            st = [state_ref[h] for h in hs]
            st16 = [x.astype(BF16) for x in st]
            ws = [_dot(w16[idx[h]], st16[h]) for h in hs]
            vn16 = [(u[idx[h]] - ws[h]).astype(BF16) for h in hs]
            qs = [_dot(qe16[idx[h]], st16[h]) for h in hs]
            av = [_dot(a16[idx[h]], vn16[h]) for h in hs]
            kv = [_dot_tn(kd16[idx[h]], vn16[h]) for h in hs]
            for h in hs:
                state_ref[h] = st[h] * egl[idx[h]] + kv[h]
                o = qs[h] + av[h]
                on = o * lax.rsqrt(jnp.mean(o * o, -1, keepdims=True) + EPS) * dnw
                z = x_ref[0, pl.ds(r0, c64), DN_QKV + h * DN_DV:DN_QKV + (h + 1) * DN_DV].astype(F32)
                o_ref[0, pl.ds(r0, c64), h * DN_DV:(h + 1) * DN_DV] = (on * _silu(z)).astype(o_ref.dtype)
        return carry

    lax.fori_loop(0, nc // grp, chunk_group, 0)


def _deltanet(proj, small, a_t, conv_w, a_log, dt_bias, dn_norm, ts):
    b, s, _ = proj.shape
    wdn = DN_QKV + DN_HEADS * DN_DV
    pad = jnp.zeros((LANES - 2 * DN_HEADS,), F32)
    alog_row = jnp.concatenate([jnp.zeros((DN_HEADS,), F32), a_log, pad]).reshape(1, LANES)
    dtb_row = jnp.concatenate([jnp.zeros((DN_HEADS,), F32), dt_bias, pad]).reshape(1, LANES)
    nc = ts // DN_CHUNK
    return pl.pallas_call(
        functools.partial(_dn_body, ts=ts),
        grid=(b, s // ts),
        in_specs=[pl.BlockSpec((1, ts, wdn), lambda bi, i: (bi, i, 0)),
                  pl.BlockSpec((1, ts, LANES), lambda bi, i: (bi, i, 0)),
                  pl.BlockSpec((1, DN_HEADS, ts), lambda bi, i: (bi, 0, i)),
                  pl.BlockSpec((CONV_W, DN_QKV), lambda bi, i: (0, 0)),
                  pl.BlockSpec((1, LANES), lambda bi, i: (0, 0)),
                  pl.BlockSpec((1, LANES), lambda bi, i: (0, 0)),
                  pl.BlockSpec((DN_HEADS, 1), lambda bi, i: (0, 0)),
                  pl.BlockSpec((DN_HEADS, 1), lambda bi, i: (0, 0)),
                  pl.BlockSpec((1, DN_DV), lambda bi, i: (0, 0))],
        out_specs=pl.BlockSpec((1, ts, DN_HEADS * DN_DV), lambda bi, i: (bi, i, 0)),
        out_shape=jax.ShapeDtypeStruct((b, s, DN_HEADS * DN_DV), BF16),
        scratch_shapes=[pltpu.VMEM((ts + 8, DN_QKV), F32),
                        pltpu.VMEM((nc, DN_CHUNK, DN_QKV), F32),
                        pltpu.VMEM((nc, DN_CHUNK, LANES), F32),
                        pltpu.VMEM((nc, DN_CHUNK, LANES), F32),
                        pltpu.VMEM((nc, DN_HEADS, DN_CHUNK), F32),
                        pltpu.VMEM((DN_HEADS, DN_DK, DN_DV), F32)],
        compiler_params=_params(("parallel", "arbitrary")),
        name="deltanet",
    )(proj, small, a_t, conv_w, alog_row, dtb_row, a_log.reshape(DN_HEADS, 1),
      dt_bias.reshape(DN_HEADS, 1), dn_norm.reshape(1, DN_DV))


def _seg_ones():
    r = lax.broadcasted_iota(jnp.int32, (LANES, LANES), 0) // NSA_DH
    c = lax.broadcasted_iota(jnp.int32, (LANES, LANES), 1) // NSA_DH
    return jnp.where(r == c, 1.0, 0.0).astype(F32)


V_ROWS = 80
LOG2E = 1.4426950408889634


def _eye16(n):
    r = lax.broadcasted_iota(jnp.int32, (n, n), 0)
    c = lax.broadcasted_iota(jnp.int32, (n, n), 1)
    return jnp.where(r == c, 1.0, 0.0).astype(BF16)


def _transpose16(x16, eye):
    return _dot_tn(x16, eye).astype(BF16)


def _norm_rope(x, w, cos_f, sin_s, seg):
    ms = _dot(x * x, seg) * (1.0 / NSA_DH)
    y = x * lax.rsqrt(ms + EPS) * w
    half = NSA_DH // 2
    lane = lax.broadcasted_iota(jnp.int32, y.shape, 1) & (NSA_DH - 1)
    partner = jnp.where(lane < half, pltpu.roll(y, LANES - half, 1), pltpu.roll(y, half, 1))
    return y * cos_f + partner * sin_s


def _nsa_prep_body(q_ref, ck_ref, cv_ref, sk_ref, sv_ref, wk_ref, wv_ref, cos_ref, sin_ref,
                   qw_ref, skw_ref, wkw_ref,
                   qo_ref, sko_ref, svo_ref, wko_ref, wvo_ref, cko_ref, cvo_ref):
    seg = _seg_ones()
    cos_f = cos_ref[0]
    sin_s = sin_ref[0]
    ts = cos_f.shape[0]
    eye = _eye16(ts)
    for sl in range(NSA_HEADS * NSA_DH // LANES):
        x = q_ref[0, :, sl * LANES:(sl + 1) * LANES].astype(F32)
        y = (_norm_rope(x, qw_ref[...], cos_f, sin_s, seg) * (NSA_DH ** -0.5 * LOG2E)).astype(BF16)
        y_t = _transpose16(y, eye)
        for half in range(2):
            h = 2 * sl + half
            qo_ref[0, h // NSA_HPG, h % NSA_HPG, 0:NSA_DH, :] = y_t[half * NSA_DH:(half + 1) * NSA_DH]
            qo_ref[0, h // NSA_HPG, h % NSA_HPG, NSA_DH:LANES, :] = jnp.zeros((LANES - NSA_DH, ts), BF16)
    sk = _norm_rope(sk_ref[0].astype(F32), skw_ref[...], cos_f, sin_s, seg).astype(BF16)
    wk = _norm_rope(wk_ref[0].astype(F32), wkw_ref[...], cos_f, sin_s, seg).astype(BF16)
    sv_t = _transpose16(sv_ref[0], eye)
    wv_t = _transpose16(wv_ref[0], eye)
    ones_row = jnp.where(lax.broadcasted_iota(jnp.int32, (V_ROWS - NSA_DH, ts), 0) == 0, 1.0, 0.0).astype(BF16)
    blk = (pl.program_id(1) * ts + lax.broadcasted_iota(jnp.int32, (ts, LANES), 0)) >> SEL_SHIFT
    onehot = jnp.where(lax.broadcasted_iota(jnp.int32, (ts, LANES), 1) == blk, SEL_BIAS, 0.0).astype(BF16)
    for g in range(NSA_GROUPS):
        cols = slice(g * NSA_DH, (g + 1) * NSA_DH)
        sko_ref[0, g, :, 0:LANES] = onehot
        sko_ref[0, g, :, LANES:LANES + NSA_DH] = sk[:, cols]
        sko_ref[0, g, :, LANES + NSA_DH:2 * LANES] = jnp.zeros((ts, LANES - NSA_DH), BF16)
        wko_ref[0, g] = wk[:, cols]
        svo_ref[0, g, 0:NSA_DH, :] = sv_t[cols]
        svo_ref[0, g, NSA_DH:V_ROWS, :] = ones_row
        wvo_ref[0, g, 0:NSA_DH, :] = wv_t[cols]
        wvo_ref[0, g, NSA_DH:V_ROWS, :] = ones_row
    cko_ref[0] = ck_ref[0]
    cvo_ref[0] = cv_ref[0]


def _nsa_prep(proj, col0, cos_f, sin_s, q_norm, k_norm_s, k_norm_w, ts):
    b, s, _ = proj.shape
    assert s // SEL_LEN <= LANES
    qw = NSA_HEADS * NSA_DH
    qblk = col0 // qw
    k0 = (col0 + qw) // LANES

    def kspec(i):
        return pl.BlockSpec((1, ts, LANES), lambda bi, t, i=i: (bi, t, k0 + i))

    tile2 = lambda w: jnp.tile(w.reshape(1, NSA_DH), (1, LANES // NSA_DH))
    gshape = jax.ShapeDtypeStruct((b, NSA_GROUPS, s, NSA_DH), BF16)
    gspec = pl.BlockSpec((1, NSA_GROUPS, ts, NSA_DH), lambda bi, t: (bi, 0, t, 0))
    ashape = jax.ShapeDtypeStruct((b, NSA_GROUPS, s, 2 * LANES), BF16)
    aspec = pl.BlockSpec((1, NSA_GROUPS, ts, 2 * LANES), lambda bi, t: (bi, 0, t, 0))
    vshape = jax.ShapeDtypeStruct((b, NSA_GROUPS, V_ROWS, s), BF16)
    vspec = pl.BlockSpec((1, NSA_GROUPS, V_ROWS, ts), lambda bi, t: (bi, 0, 0, t))
    cspec = pl.BlockSpec((1, ts, LANES), lambda bi, t: (bi, t, 0))
    wspec = pl.BlockSpec((1, LANES), lambda bi, t: (0, 0))
    return pl.pallas_call(
        _nsa_prep_body,
        grid=(b, s // ts),
        in_specs=[pl.BlockSpec((1, ts, qw), lambda bi, t: (bi, t, qblk)),
                  kspec(0), kspec(1), kspec(2), kspec(3), kspec(4), kspec(5),
                  cspec, cspec, wspec, wspec, wspec],
        out_specs=[pl.BlockSpec((1, NSA_GROUPS, NSA_HPG, LANES, ts), lambda bi, t: (bi, 0, 0, 0, t)),
                   aspec, vspec, gspec, vspec, cspec, cspec],
        out_shape=[jax.ShapeDtypeStruct((b, NSA_GROUPS, NSA_HPG, LANES, s), BF16),
                   ashape, vshape, gshape, vshape,
                   jax.ShapeDtypeStruct((b, s, LANES), BF16),
                   jax.ShapeDtypeStruct((b, s, LANES), BF16)],
        compiler_params=_params(("parallel", "parallel")),
        name="nsa_prep",
    )(proj, proj, proj, proj, proj, proj, proj, cos_f, sin_s,
      tile2(q_norm), tile2(k_norm_s), tile2(k_norm_w))


def _compress_body(ck_ref, cv_ref, pos_ref, w1a_ref, w1b_ref, w2_ref, kw_ref, cos_ref, sin_ref,
                   ko_ref, vo_ref):
    n = ck_ref.shape[1]
    outs = []
    for which, x_ref in enumerate((ck_ref, cv_ref)):
        x = x_ref[0].astype(F32)
        lo = _dot((x + pos_ref[which, 0:1, :]).astype(BF16), w1a_ref[which])
        hi = _dot((x + pos_ref[which, 1:2, :]).astype(BF16), w1b_ref[which])
        h1 = _silu(lo + pltpu.roll(hi, n - 1, 0))
        outs.append(_dot(h1.astype(BF16), w2_ref[which]))
    kc = _norm_rope(outs[0], kw_ref[...], cos_ref[0], sin_ref[0], _seg_ones()).astype(BF16)
    vc_t = _transpose16(outs[1].astype(BF16), _eye16(n))
    for g in range(NSA_GROUPS):
        ko_ref[0, g] = kc[:, g * NSA_DH:(g + 1) * NSA_DH]
        vo_ref[0, g] = vc_t[g * NSA_DH:(g + 1) * NSA_DH]


def _compress(ck, cv, cmp_pos, w_cmp1, w_cmp2, k_norm_c, cos_c, sin_c):
    b, s, _ = ck.shape
    n = s // CMP_STRIDE
    width = CMP_STRIDE * LANES
    per_row = CMP_LEN // CMP_STRIDE
    eye_g = jnp.eye(NSA_GROUPS, dtype=F32)
    w1 = w_cmp1.reshape(2, per_row, CMP_STRIDE, NSA_DH, NSA_DH)
    w1 = jnp.einsum('khldo,gG->khlgdGo', w1, eye_g).reshape(2, per_row, width, LANES).astype(BF16)
    w2 = jnp.einsum('kdo,gG->kgdGo', w_cmp2, eye_g).reshape(2, LANES, LANES).astype(BF16)
    pos = jnp.broadcast_to(cmp_pos.reshape(2, per_row, CMP_STRIDE, 1, NSA_DH),
                           (2, per_row, CMP_STRIDE, NSA_GROUPS, NSA_DH)).reshape(2, per_row, width)
    kw = jnp.tile(k_norm_c.reshape(1, NSA_DH), (1, NSA_GROUPS))
    full = lambda shp: pl.BlockSpec(shp, lambda bi: (0,) * len(shp))
    bspec = pl.BlockSpec((1, n, width), lambda bi: (bi, 0, 0))
    tspec = pl.BlockSpec((1, n, LANES), lambda bi: (bi, 0, 0))
    ospec = pl.BlockSpec((1, NSA_GROUPS, n, NSA_DH), lambda bi: (bi, 0, 0, 0))
    oshape = jax.ShapeDtypeStruct((b, NSA_GROUPS, n, NSA_DH), BF16)
    return pl.pallas_call(
        _compress_body,
        grid=(b,),
        in_specs=[bspec, bspec, full((2, per_row, width)), full((2, width, LANES)),
                  full((2, width, LANES)), full((2, LANES, LANES)), full((1, LANES)), tspec, tspec],
        out_specs=[ospec, pl.BlockSpec((1, NSA_GROUPS, NSA_DH, n), lambda bi: (bi, 0, 0, 0))],
        out_shape=[oshape, jax.ShapeDtypeStruct((b, NSA_GROUPS, NSA_DH, n), BF16)],
        compiler_params=_params(("parallel",)),
        name="nsa_compress",
    )(ck.reshape(b, n, width), cv.reshape(b, n, width), pos, w1[:, 0], w1[:, 1], w2, kw, cos_c, sin_c)


def _nsa_body(qt_ref, kc_ref, vct_ref, ksa_ref, vst_ref, kw_ref, vwt_ref, gt_ref, o_ref,
              qa_ref, sa_ref, sb_ref, os_ref, *, tq, tk, n_sel):
    hp = NSA_HPG
    hs = range(hp)
    t0 = pl.program_id(2) * tq
    q_t = [qt_ref[0, 0, j, 0:NSA_DH, :] for j in hs]
    t_row = t0 + lax.broadcasted_iota(jnp.int32, (1, tq), 1)

    kc = kc_ref[0, 0]
    vc_t = vct_ref[0, 0]
    n_cmp = kc.shape[0]
    cmp_end = lax.broadcasted_iota(jnp.int32, (n_cmp, 1), 0) * CMP_STRIDE + (CMP_LEN - 1)
    bias_c = jnp.where(cmp_end <= t_row, 0.0, NEG)
    valid_c = jnp.where(t_row >= CMP_LEN - 1, 1.0, 0.0)
    cs = lax.broadcasted_iota(jnp.int32, (LANES, n_cmp), 1) * CMP_STRIDE
    bs = lax.broadcasted_iota(jnp.int32, (LANES, n_cmp), 0) * SEL_LEN
    overlap_t = jnp.where((cs < bs + SEL_LEN) & (cs + CMP_LEN > bs), 1.0, 0.0).astype(BF16)
    o_c = []
    imp = None
    s_c = [_dot(kc, q_t[j]) for j in hs]
    for j in hs:
        sc = s_c[j] + bias_c
        e_c = jnp.exp2(sc - jnp.max(sc, 0, keepdims=True))
        p16 = (e_c * (valid_c / jnp.sum(e_c, 0, keepdims=True))).astype(BF16)
        o_c.append(_dot(vc_t, p16))
        part = _dot(overlap_t, p16)
        imp = part if imp is None else imp + part

    jb = lax.broadcasted_iota(jnp.int32, (LANES, 1), 0)
    cur = t_row >> SEL_SHIFT
    forced = (jb == 0) | (jb == cur) | (jb == cur - 1)
    imp = jnp.where(forced, FORCE, jnp.where(jb * SEL_LEN <= t_row, imp, -FORCE))
    imp = jnp.where(jb < n_sel, imp, -jnp.inf)
    jbf = jb.astype(F32)
    for _ in range(min(SEL_TOP, n_sel)):
        mx = jnp.max(imp, 0, keepdims=True)
        first = jnp.min(jnp.where(imp == mx, jbf, float(LANES)), 0, keepdims=True)
        imp = jnp.where(jbf == first, -jnp.inf, imp)
    selm1_t = jnp.where((imp == -jnp.inf) & (jb < n_sel), 0.0, -1.0).astype(BF16)
    for j in hs:
        qa_ref[j, 0:LANES, :] = selm1_t
        qa_ref[j, LANES:2 * LANES, :] = qt_ref[0, 0, j]

    def scores(kt, s_ref):
        k_aug = ksa_ref[0, 0, pl.ds(pl.multiple_of(kt * tk, tk), tk), :]
        for j in hs:
            s_ref[j] = _dot(k_aug, qa_ref[j])

    def softmax_pv(kt, s_ref, state, diagonal):
        k0 = pl.multiple_of(kt * tk, tk)
        m, acc = list(state[0]), list(state[1])
        v_t = vst_ref[0, 0, :, pl.ds(k0, tk)]
        if diagonal:
            tok = k0 + lax.broadcasted_iota(jnp.int32, (tk, 1), 0)
            bias = jnp.where(tok <= t_row, 0.0, NEG)
        for j in hs:
            sj = s_ref[j] + bias if diagonal else s_ref[j]
            m_new = jnp.maximum(m[j], jnp.max(sj, 0, keepdims=True))
            alpha = jnp.exp2(m[j] - m_new)
            e16 = jnp.exp2((sj - m_new).astype(BF16))
            acc[j] = alpha * acc[j] + _dot(v_t, e16)
            m[j] = m_new
        return tuple(m), tuple(acc)

    def tile_pair(p, state):
        scores(2 * p + 1, sb_ref)
        state = softmax_pv(2 * p, sa_ref, state, False)
        scores(2 * p + 2, sa_ref)
        return softmax_pv(2 * p + 1, sb_ref, state, False)

    def finish(state):
        acc = state[1]
        for j in hs:
            os_ref[j] = acc[j][0:NSA_DH] / jnp.maximum(acc[j][NSA_DH:NSA_DH + 1], 1e-30)

    k_diag = t0 // tk
    state0 = (tuple(jnp.full((1, tq), NEG, F32) for _ in hs),
              tuple(jnp.zeros((V_ROWS, tq), F32) for _ in hs))
    scores(0, sa_ref)
    state = lax.fori_loop(0, k_diag // 2, tile_pair, state0)

    @pl.when(k_diag % 2 == 0)
    def _():
        finish(softmax_pv(k_diag, sa_ref, state, True))

    @pl.when(k_diag % 2 == 1)
    def _():
        scores(k_diag, sb_ref)
        finish(softmax_pv(k_diag, sb_ref, softmax_pv(k_diag - 1, sa_ref, state, False), True))

    o_s = [os_ref[j] for j in hs]

    wl = WINDOW + tq
    w0 = pl.multiple_of(jnp.maximum(t0 - WINDOW, 0), tq)
    dist = t_row - (w0 + lax.broadcasted_iota(jnp.int32, (wl, 1), 0))
    bias_w = jnp.where((dist >= 0) & (dist < WINDOW), 0.0, NEG)
    k_w = kw_ref[0, 0, pl.ds(w0, wl), :]
    vw_t = vwt_ref[0, 0, :, pl.ds(w0, wl)]
    o_w = []
    s_w = [_dot(k_w, q_t[j]) for j in hs]
    for j in hs:
        sw = s_w[j] + bias_w
        e16 = jnp.exp2((sw - jnp.max(sw, 0, keepdims=True)).astype(BF16))
        oa = _dot(vw_t, e16)
        o_w.append(oa[0:NSA_DH] / oa[NSA_DH:NSA_DH + 1])

    gates = _sigmoid(gt_ref[0, 0])
    for j in range(hp):
        o = (gates[3 * j:3 * j + 1] * o_c[j] + gates[3 * j + 1:3 * j + 2] * o_s[j]
             + gates[3 * j + 2:3 * j + 3] * o_w[j])
        o_ref[0, j * NSA_DH:(j + 1) * NSA_DH, :] = o.astype(o_ref.dtype)


def _nsa_attention(q_t, kc, vc_t, ksa, vs_t, kw, vw_t, gates_t, tq, tk):
    b, g, hp, _, s = q_t.shape
    dh = NSA_DH
    n_cmp = kc.shape[2]
    assert s >= WINDOW + tq and WINDOW % tq == 0 and s % tk == 0 and tk % tq == 0 and tq % LANES == 0
    vt_spec = pl.BlockSpec((1, 1, V_ROWS, s), lambda bi, gi, i: (bi, gi, 0, 0))
    return pl.pallas_call(
        functools.partial(_nsa_body, tq=tq, tk=tk, n_sel=s // SEL_LEN),
        grid=(b, g, s // tq),
        in_specs=[pl.BlockSpec((1, 1, hp, LANES, tq), lambda bi, gi, i: (bi, gi, 0, 0, i)),
                  pl.BlockSpec((1, 1, n_cmp, dh), lambda bi, gi, i: (bi, gi, 0, 0)),
                  pl.BlockSpec((1, 1, dh, n_cmp), lambda bi, gi, i: (bi, gi, 0, 0)),
                  pl.BlockSpec((1, 1, s, 2 * LANES), lambda bi, gi, i: (bi, gi, 0, 0)),
                  vt_spec,
                  pl.BlockSpec((1, 1, s, dh), lambda bi, gi, i: (bi, gi, 0, 0)),
                  vt_spec,
                  pl.BlockSpec((1, 1, 3 * hp, tq), lambda bi, gi, i: (bi, gi, 0, i))],
        out_specs=pl.BlockSpec((1, hp * dh, tq), lambda bi, gi, i: (bi, gi, i)),
        out_shape=jax.ShapeDtypeStruct((b, g * hp * dh, s), BF16),
        scratch_shapes=[pltpu.VMEM((hp, 2 * LANES, tq), BF16), pltpu.VMEM((hp, tk, tq), F32),
                        pltpu.VMEM((hp, tk, tq), F32), pltpu.VMEM((hp, dh, tq), F32)],
        compiler_params=_params(("parallel", "parallel", "arbitrary")),
        name="nsa_attention",
    )(q_t, kc, vc_t, ksa, vs_t, kw, vw_t, gates_t)


def _mix_out_body(x_ref, oa_ref, ob_ref, mg_ref, gm_ref, woa_ref, wob_ref, wout_ref, o_ref):
    d = x_ref.shape[2]
    y_a = _dot(oa_ref[0], woa_ref[...])
    y_b = _dot_tn(ob_ref[0], wob_ref[...])
    merged = (_sigmoid(mg_ref[0, :, 0:d].astype(F32)) * y_a
              + _sigmoid(mg_ref[0, :, d:2 * d].astype(F32)) * y_b)
    o_ref[0] = x_ref[0] + gm_ref[0] * _dot(merged.astype(BF16), wout_ref[...])


def _mix_out(x, o_a, o_b, proj, mg_blk, g_m, w_oa, w_ob, w_out, tm):
    b, s, d = x.shape
    full = lambda a: pl.BlockSpec(a.shape, lambda bi, i: (0, 0))
    return pl.pallas_call(
        _mix_out_body,
        grid=(b, s // tm),
        in_specs=[pl.BlockSpec((1, tm, d), lambda bi, i: (bi, i, 0)),
                  pl.BlockSpec((1, tm, o_a.shape[2]), lambda bi, i: (bi, i, 0)),
                  pl.BlockSpec((1, o_b.shape[1], tm), lambda bi, i: (bi, 0, i)),
                  pl.BlockSpec((1, tm, 2 * d), lambda bi, i: (bi, i, mg_blk)),
                  pl.BlockSpec((1, 1, d), lambda bi, i: (bi, 0, 0)),
                  full(w_oa), full(w_ob), full(w_out)],
        out_specs=pl.BlockSpec((1, tm, d), lambda bi, i: (bi, i, 0)),
        out_shape=jax.ShapeDtypeStruct(x.shape, F32),
        compiler_params=_params(("parallel", "parallel")),
        name="mix_out",
    )(x, o_a, o_b, proj, g_m, w_oa, w_ob, w_out)


def _ffn_body(x_ref, nw_ref, sc_ref, sh_ref, gf_ref, w1_ref, w3_ref, w2_ref, o_ref, h_ref, acc_ref):
    f = pl.program_id(2)

    @pl.when(f == 0)
    def _():
        h_ref[...] = _norm_mod(x_ref[0], nw_ref[...], sc_ref[0], sh_ref[0]).astype(BF16)
        acc_ref[...] = jnp.zeros_like(acc_ref)

    h = h_ref[...]
    t = _silu(_dot(h, w1_ref[...])) * _dot(h, w3_ref[...])
    acc_ref[...] += _dot(t.astype(BF16), w2_ref[...])

    @pl.when(f == pl.num_programs(2) - 1)
    def _():
        o_ref[0] = x_ref[0] + gf_ref[0] * acc_ref[...]


def _dense_ffn(x, nw, sc, sh, g_f, w1, w3, w2, tm, tf):
    b, s, d = x.shape
    ff = w1.shape[1]
    vec = pl.BlockSpec((1, 1, d), lambda bi, i, f: (bi, 0, 0))
    return pl.pallas_call(
        _ffn_body,
        grid=(b, s // tm, ff // tf),
        in_specs=[pl.BlockSpec((1, tm, d), lambda bi, i, f: (bi, i, 0)),
                  pl.BlockSpec((1, d), lambda bi, i, f: (0, 0)), vec, vec, vec,
                  pl.BlockSpec((d, tf), lambda bi, i, f: (0, f)),
                  pl.BlockSpec((d, tf), lambda bi, i, f: (0, f)),
                  pl.BlockSpec((tf, d), lambda bi, i, f: (f, 0))],
        out_specs=pl.BlockSpec((1, tm, d), lambda bi, i, f: (bi, i, 0)),
        out_shape=jax.ShapeDtypeStruct(x.shape, F32),
        scratch_shapes=[pltpu.VMEM((tm, d), BF16), pltpu.VMEM((tm, d), F32)],
        compiler_params=_params(("parallel", "parallel", "arbitrary")),
        name="dense_ffn",
    )(x, nw, sc, sh, g_f, w1, w3, w2)


MOE_ROWS = 128


def _moe_route_body(x_ref, nw_ref, sc_ref, sh_ref, wr_ref, h_ref, route_ref, meta_ref):
    tm = x_ref.shape[1]
    h16 = _norm_mod(x_ref[0], nw_ref[...], sc_ref[0], sh_ref[0]).astype(BF16)
    h_ref[0] = h16
    logits = _dot_nt(wr_ref[...], h16)
    ef = lax.broadcasted_iota(jnp.int32, (N_EXPERTS, tm), 0).astype(F32)
    m1 = jnp.max(logits, 0, keepdims=True)
    i1 = jnp.min(jnp.where(logits == m1, ef, float(N_EXPERTS)), 0, keepdims=True)
    rest = jnp.where(ef == i1, -jnp.inf, logits)
    m2 = jnp.max(rest, 0, keepdims=True)
    i2 = jnp.min(jnp.where(rest == m2, ef, float(N_EXPERTS)), 0, keepdims=True)
    e2 = jnp.exp(m2 - m1)
    oh1 = jnp.where(ef == i1, 1.0, 0.0)
    oh2 = jnp.where(ef == i2, 1.0, 0.0)
    member = oh1 + oh2
    lane = lax.broadcasted_iota(jnp.int32, (N_EXPERTS, tm), 1)
    csum = member
    sft = 1
    while sft < tm:
        csum = csum + jnp.where(lane >= sft, pltpu.roll(csum, sft, 1), 0.0)
        sft *= 2
    count = jnp.max(csum, 1, keepdims=True)
    nblk = jnp.floor((count + (MOE_ROWS - 1)) * (1.0 / MOE_ROWS))
    nblk_b = jnp.broadcast_to(nblk, (N_EXPERTS, LANES))
    row = lax.broadcasted_iota(jnp.int32, (N_EXPERTS, LANES), 0)
    bsum = nblk_b
    for sft in (1, 2, 4):
        bsum = bsum + jnp.where(row >= sft, pltpu.roll(bsum, sft, 0), 0.0)
    bstart = bsum - nblk_b
    slot = bstart[:, 0:1] * MOE_ROWS + (csum - member)
    rrow = lax.broadcasted_iota(jnp.int32, (8, tm), 0)
    route_ref[0] = jnp.where(
        rrow == 0, jnp.sum(oh1 * slot, 0, keepdims=True),
        jnp.where(rrow == 1, jnp.sum(oh2 * slot, 0, keepdims=True),
                  jnp.where(rrow == 2, 1.0 / (1.0 + e2), jnp.where(rrow == 3, e2 / (1.0 + e2), 0.0))))
    col = lax.broadcasted_iota(jnp.int32, (N_EXPERTS, LANES), 1)
    meta_ref[0] = jnp.where(col == 0, nblk_b, jnp.where(col == 1, bstart, 0.0)).astype(jnp.int32)


def _moe_group_body(nblk_ref, bstart_ref, x_ref, h_ref, route_ref, gf_ref, w1_ref, w3_ref, w2_ref, o_ref,
                    hb_ref, cw_ref, acc_ref, *, n_rows):
    i = pl.program_id(0)
    e = pl.program_id(1)
    f = pl.program_id(2)
    tm = x_ref.shape[1]
    nb = nblk_ref[i * N_EXPERTS + e]
    b0 = bstart_ref[i * N_EXPERTS + e]
    slot1 = route_ref[0, 0:1, :]
    slot2 = route_ref[0, 1:2, :]

    def hits(r0, rows):
        rr = (r0 + lax.broadcasted_iota(jnp.int32, (rows, 1), 0)).astype(F32)
        return rr == slot1, rr == slot2

    def expert_rows(blk, n_blk, first):
        r0 = pl.multiple_of((b0 + blk) * MOE_ROWS, MOE_ROWS)
        rows = pl.ds(r0, n_blk * MOE_ROWS)
        if first:
            hit1, hit2 = hits(r0, n_blk * MOE_ROWS)
            gather = jnp.where(hit1, 1.0, jnp.where(hit2, 1.0, 0.0)).astype(BF16)
            hb_ref[rows, :] = _dot(gather, h_ref[0]).astype(BF16)
            cw_ref[rows, :] = jnp.sum(jnp.where(hit1, route_ref[0, 2:3, :], 0.0)
                                      + jnp.where(hit2, route_ref[0, 3:4, :], 0.0), -1, keepdims=True)
        hb = hb_ref[rows, :]
        t = _silu(_dot(hb, w1_ref[0, 0])) * _dot(hb, w3_ref[0, 0]) * cw_ref[rows, :]
        y = _dot(t.astype(BF16), w2_ref[0])
        if first:
            acc_ref[rows, :] = y
        else:
            acc_ref[rows, :] += y

    def expert_all(first):
        lax.fori_loop(0, nb // 2, lambda k, c: (expert_rows(2 * k, 2, first), c)[1], 0)

        @pl.when(nb % 2 == 1)
        def _():
            expert_rows(nb - 1, 1, first)

    @pl.when(f == 0)
    def _():
        expert_all(True)

    @pl.when(f != 0)
    def _():
        expert_all(False)

    @pl.when((e == pl.num_programs(1) - 1) & (f == pl.num_programs(2) - 1))
    def _():
        def clear(k, c):
            acc_ref[pl.ds(pl.multiple_of(k * MOE_ROWS, MOE_ROWS), MOE_ROWS), :] = jnp.zeros(
                (MOE_ROWS, acc_ref.shape[1]), F32)
            return c

        lax.fori_loop(b0 + nb, n_rows // MOE_ROWS, clear, 0)
        chunk = 4 * MOE_ROWS
        for kc in range(n_rows // chunk):
            hit1, hit2 = hits(kc * chunk, chunk)
            scatter = jnp.where(hit1, 1.0, jnp.where(hit2, 1.0, 0.0)).astype(BF16)
            y = _dot_tn(scatter, acc_ref[kc * chunk:(kc + 1) * chunk, :].astype(BF16))
            if kc == 0:
                o_ref[0] = y
            else:
                o_ref[0] += y
        o_ref[0] = x_ref[0] + gf_ref[0] * o_ref[0]


def _moe_ffn(x, nw, sc, sh, g_f, w_router, w1, w3, w2, tm, tf):
    b, s, d = x.shape
    n_e, _, ff = w1.shape
    assert n_e == N_EXPERTS
    tiles_b = s // tm
    nt = b * tiles_b
    vec = pl.BlockSpec((1, 1, d), lambda bi, i: (bi, 0, 0))
    h16, route, meta = pl.pallas_call(
        _moe_route_body,
        grid=(b, tiles_b),
        in_specs=[pl.BlockSpec((1, tm, d), lambda bi, i: (bi, i, 0)),
                  pl.BlockSpec((1, d), lambda bi, i: (0, 0)), vec, vec,
                  pl.BlockSpec((n_e, d), lambda bi, i: (0, 0))],
        out_specs=[pl.BlockSpec((1, tm, d), lambda bi, i: (bi, i, 0)),
                   pl.BlockSpec((1, 8, tm), lambda bi, i: (bi * tiles_b + i, 0, 0)),
                   pl.BlockSpec((1, n_e, LANES), lambda bi, i: (bi * tiles_b + i, 0, 0))],
        out_shape=[jax.ShapeDtypeStruct((b, s, d), BF16),
                   jax.ShapeDtypeStruct((nt, 8, tm), F32),
                   jax.ShapeDtypeStruct((nt, n_e, LANES), jnp.int32)],
        compiler_params=_params(("parallel", "parallel")),
        name="moe_route",
    )(x, nw, sc, sh, w_router.T.astype(BF16))
    chunked = lambda w: w.reshape(n_e, d, ff // tf, tf).transpose(0, 2, 1, 3)
    n_rows = -(-(2 * tm + n_e * (MOE_ROWS - 1)) // (4 * MOE_ROWS)) * (4 * MOE_ROWS)
    grid_spec = pltpu.PrefetchScalarGridSpec(
        num_scalar_prefetch=2,
        grid=(nt, n_e, ff // tf),
        in_specs=[pl.BlockSpec((1, tm, d), lambda i, e, f, nb, bs: (i, 0, 0)),
                  pl.BlockSpec((1, tm, d), lambda i, e, f, nb, bs: (i, 0, 0)),
                  pl.BlockSpec((1, 8, tm), lambda i, e, f, nb, bs: (i, 0, 0)),
                  pl.BlockSpec((1, 1, d), lambda i, e, f, nb, bs: (i // tiles_b, 0, 0)),
                  pl.BlockSpec((1, 1, d, tf), lambda i, e, f, nb, bs: (e, f, 0, 0)),
                  pl.BlockSpec((1, 1, d, tf), lambda i, e, f, nb, bs: (e, f, 0, 0)),
                  pl.BlockSpec((1, tf, d), lambda i, e, f, nb, bs: (e, f, 0))],
        out_specs=pl.BlockSpec((1, tm, d), lambda i, e, f, nb, bs: (i, 0, 0)),
        scratch_shapes=[pltpu.VMEM((n_rows, d), BF16), pltpu.VMEM((n_rows, 1), F32),
                        pltpu.VMEM((n_rows, d), F32)])
    out = pl.pallas_call(
        functools.partial(_moe_group_body, n_rows=n_rows),
        grid_spec=grid_spec,
        out_shape=jax.ShapeDtypeStruct((nt, tm, d), F32),
        compiler_params=_params(("parallel", "arbitrary", "arbitrary")),
        name="moe_group",
    )(meta[:, :, 0].reshape(-1), meta[:, :, 1].reshape(-1),
      x.reshape(nt, tm, d), h16.reshape(nt, tm, d), route, g_f, chunked(w1), chunked(w3), w2)
    return out.reshape(b, s, d)


def _rope_tables(pos):
    inv = 1.0 / (ROPE_THETA ** (jnp.arange(0, NSA_DH, 2, dtype=F32) / NSA_DH))
    ang = pos.astype(F32)[..., None] * inv
    cos, sin = jnp.cos(ang), jnp.sin(ang)
    reps = LANES // NSA_DH
    return (jnp.tile(jnp.concatenate([cos, cos], -1), (1, 1, reps)),
            jnp.tile(jnp.concatenate([-sin, sin], -1), (1, 1, reps)))


_SPLITS = (DN_QKV, DN_HEADS * DN_DV, DN_HEADS, DN_HEADS, NSA_HEADS * NSA_DH) + (NSA_GROUPS * NSA_DH,) * 6
_OFF = np.concatenate([[0], np.cumsum(_SPLITS)])
_OFF_NG = int(_OFF[-1])
_OFF_MG = _OFF_NG + 3 * NSA_HEADS


def kernel(x, c, positions, w_ada, b_ada, norm_mix, norm_ffn, w_in, conv_w, a_log, dt_bias, dn_norm, cmp_pos, w_cmp1, w_cmp2, q_norm, k_norm, w_oa, w_ob, w_out, w1_dense, w3_dense, w2_dense, w_router, w1_moe, w3_moe, w2_moe):
    b, s, d = x.shape
    depth = w_in.shape[0]
    wdn = DN_QKV + DN_HEADS * DN_DV
    n_small = 2 * DN_HEADS + 3 * NSA_HEADS

    cos_f, sin_s = _rope_tables(positions)
    n_cmp_pad = s // CMP_STRIDE
    cmp_end = jnp.minimum(jnp.arange(n_cmp_pad) * CMP_STRIDE + CMP_LEN - 1, s - 1)
    cos_c, sin_c = _rope_tables(positions[:, cmp_end])

    mod = _ada_mod(c, w_ada, b_ada)

    off_nq = int(_OFF[4])
    w_main = jnp.concatenate([w_in[:, :, 0:wdn], w_in[:, :, _OFF_MG:_OFF_MG + 2 * d],
                              w_in[:, :, off_nq:_OFF_NG]], -1).astype(BF16)
    w_small = jnp.concatenate([w_in[:, :, wdn:wdn + 2 * DN_HEADS], w_in[:, :, _OFF_NG:_OFF_MG],
                               jnp.zeros((depth, d, LANES - n_small), F32)], -1).astype(BF16)
    nsa_col0 = wdn + 2 * d
    n_main = w_main.shape[2]

    w_oa16, w_ob16, w_out16 = w_oa.astype(BF16), w_ob.astype(BF16), w_out.astype(BF16)
    w1d, w3d, w2d = w1_dense.astype(BF16), w3_dense.astype(BF16), w2_dense.astype(BF16)
    w1m, w3m, w2m = w1_moe.astype(BF16), w3_moe.astype(BF16), w2_moe.astype(BF16)

    for l in range(depth):
        sh_m, sc_m, g_m, sh_f, sc_f, g_f = [m.reshape(b, 1, d) for m in jnp.split(mod[l], 6, -1)]
        nw_m = norm_mix[l].reshape(1, d)
        proj = _norm_mod_matmul(x, nw_m, sc_m, sh_m, w_main[l], BF16, tm=1024, tn=n_main // 3)
        small = _norm_mod_matmul(x, nw_m, sc_m, sh_m, w_small[l], F32, tm=1024, tn=LANES)
        a_t = jnp.swapaxes(small[:, :, DN_HEADS:2 * DN_HEADS], 1, 2)
        o_a = _deltanet(proj, small, a_t, conv_w[l], a_log[l], dt_bias[l], dn_norm[l], ts=512)
        qn, ksn, vs, kwn, vw, ck, cv = _nsa_prep(proj, nsa_col0, cos_f, sin_s, q_norm[l],
                                                 k_norm[l, 1], k_norm[l, 2], ts=512)
        kc, vc = _compress(ck, cv, cmp_pos[l], w_cmp1[l], w_cmp2[l], k_norm[l, 0], cos_c, sin_c)
        gates_t = jnp.swapaxes(small[:, :, 2 * DN_HEADS:n_small], 1, 2).reshape(b, NSA_GROUPS, 3 * NSA_HPG, s)
        o_b = _nsa_attention(qn, kc, vc, ksn, vs, kwn, vw, gates_t, tq=256, tk=1024)
        x = _mix_out(x, o_a, o_b, proj, wdn // (2 * d), g_m, w_oa16[l], w_ob16[l], w_out16[l], tm=512)
        nw_f = norm_ffn[l].reshape(1, d)
        if l % 2 == 0:
            x = _dense_ffn(x, nw_f, sc_f, sh_f, g_f, w1d[l // 2], w3d[l // 2], w2d[l // 2], tm=1024, tf=512)
        else:
            x = _moe_ffn(x, nw_f, sc_f, sh_f, g_f, w_router[l // 2], w1m[l // 2], w3m[l // 2],
                         w2m[l // 2], tm=1024, tf=896)
    return x
```

```python
import functools

import jax
import jax.numpy as jnp
import numpy as np
from jax import lax
from jax.experimental import pallas as pl
from jax.experimental.pallas import tpu as pltpu

F32 = jnp.float32
BF16 = jnp.bfloat16

DN_HEADS = 8
DN_DK = 64
DN_DV = 64
DN_CHUNK = 64
CONV_W = 4
DN_QKV = DN_HEADS * (2 * DN_DK + DN_DV)
NSA_HEADS = 8
NSA_GROUPS = 2
NSA_HPG = NSA_HEADS // NSA_GROUPS
NSA_DH = 64
CMP_LEN = 32
CMP_STRIDE = 16
SEL_LEN = 64
SEL_SHIFT = 6
SEL_TOP = 16
WINDOW = 512
ROPE_THETA = 10000.0
N_EXPERTS = 8
EPS = 1e-6
NEG = -1e30
FORCE = 1e6
SEL_BIAS = 1e30

LANES = 128
VMEM_LIMIT = 56 * 1024 * 1024


def _sigmoid(x):
    return 1.0 / (1.0 + jnp.exp(-x))


def _silu(x):
    return x * _sigmoid(x)


def _softplus(x):
    return jnp.maximum(x, 0.0) + jnp.log(1.0 + jnp.exp(-jnp.abs(x)))


def _dot(a, b):
    return jnp.dot(a, b, preferred_element_type=F32)


def _dot_nt(a, b):
    return lax.dot_general(a, b, (((1,), (1,)), ((), ())), preferred_element_type=F32)


def _dot_tn(a, b):
    return lax.dot_general(a, b, (((0,), (0,)), ((), ())), preferred_element_type=F32)


def _norm_mod(x, nw, sc, sh):
    y = x * lax.rsqrt(jnp.mean(x * x, -1, keepdims=True) + EPS) * nw
    return y * (1.0 + sc) + sh


def _params(sem):
    return pltpu.CompilerParams(dimension_semantics=sem, vmem_limit_bytes=VMEM_LIMIT)


def _mod_body(c_ref, w_ref, b_ref, o_ref):
    c = c_ref[...]
    o_ref[0] = _dot(_silu(c).astype(BF16), w_ref[0].astype(BF16)) + b_ref[0]


def _ada_mod(c, w_ada, b_ada):
    n_layers, d, n = w_ada.shape
    b = c.shape[0]
    tn = n // 4
    return pl.pallas_call(
        _mod_body,
        grid=(n_layers, n // tn),
        in_specs=[pl.BlockSpec((b, d), lambda l, j: (0, 0)),
                  pl.BlockSpec((1, d, tn), lambda l, j: (l, 0, j)),
                  pl.BlockSpec((1, 1, tn), lambda l, j: (l, 0, j))],
        out_specs=pl.BlockSpec((1, b, tn), lambda l, j: (l, 0, j)),
        out_shape=jax.ShapeDtypeStruct((n_layers, b, n), F32),
        compiler_params=_params(("parallel", "parallel")),
        name="ada_mod",
    )(c, w_ada, b_ada.reshape(n_layers, 1, n))


def _nm_mm_body(x_ref, nw_ref, sc_ref, sh_ref, w_ref, o_ref, h_ref):
    @pl.when(pl.program_id(2) == 0)
    def _():
        h_ref[...] = _norm_mod(x_ref[0], nw_ref[...], sc_ref[0], sh_ref[0]).astype(BF16)

    o_ref[0] = _dot(h_ref[...], w_ref[...]).astype(o_ref.dtype)


def _norm_mod_matmul(x, nw, sc, sh, w, out_dtype, tm, tn):
    b, s, d = x.shape
    n = w.shape[1]
    return pl.pallas_call(
        _nm_mm_body,
        grid=(b, s // tm, n // tn),
        in_specs=[pl.BlockSpec((1, tm, d), lambda bi, i, j: (bi, i, 0)),
                  pl.BlockSpec((1, d), lambda bi, i, j: (0, 0)),
                  pl.BlockSpec((1, 1, d), lambda bi, i, j: (bi, 0, 0)),
                  pl.BlockSpec((1, 1, d), lambda bi, i, j: (bi, 0, 0)),
                  pl.BlockSpec((d, tn), lambda bi, i, j: (0, j))],
        out_specs=pl.BlockSpec((1, tm, tn), lambda bi, i, j: (bi, i, j)),
        out_shape=jax.ShapeDtypeStruct((b, s, n), out_dtype),
        scratch_shapes=[pltpu.VMEM((tm, d), BF16)],
        compiler_params=_params(("parallel", "parallel", "arbitrary")),
        name="in_proj",
    )(x, nw, sc, sh, w)


def _dn_body(x_ref, sm_ref, at_ref, cw_ref, alog_ref, dtb_ref, alogt_ref, dtbt_ref, dnw_ref, o_ref,
             buf_ref, act_ref, gcn_ref, beta_ref, gct_ref, state_ref, *, ts):
    nc = ts // DN_CHUNK
    c64 = DN_CHUNK

    @pl.when(pl.program_id(1) == 0)
    def _():
        buf_ref[0:8, :] = jnp.zeros((8, DN_QKV), F32)
        state_ref[...] = jnp.zeros_like(state_ref)

    for sl in range(DN_QKV // LANES):
        cols = slice(sl * LANES, (sl + 1) * LANES)
        buf_ref[8:ts + 8, cols] = x_ref[0, :, cols].astype(F32)
        y = cw_ref[0:1, cols] * buf_ref[5:5 + ts, cols]
        for j in range(1, CONV_W):
            y = y + cw_ref[j:j + 1, cols] * buf_ref[5 + j:5 + j + ts, cols]
        buf_ref[0:8, cols] = buf_ref[ts:ts + 8, cols]
        act_ref[:, :, cols] = _silu(y).reshape(nc, c64, LANES)

    sm = sm_ref[0]
    beta_ref[...] = _sigmoid(sm).reshape(nc, c64, LANES)
    g = -jnp.exp(alog_ref[...]) * _softplus(sm + dtb_ref[...])
    row = lax.broadcasted_iota(jnp.int32, (ts, LANES), 0) & (c64 - 1)
    for sft in (1, 2, 4, 8, 16, 32):
        g = g + jnp.where(row >= sft, pltpu.roll(g, sft, 0), 0.0)
    gcn_ref[...] = g.reshape(nc, c64, LANES)
    gt = -jnp.exp(alogt_ref[...]) * _softplus(at_ref[0] + dtbt_ref[...])
    lane = lax.broadcasted_iota(jnp.int32, (DN_HEADS, ts), 1) & (c64 - 1)
    for sft in (1, 2, 4, 8, 16, 32):
        gt = gt + jnp.where(lane >= sft, pltpu.roll(gt, sft, 1), 0.0)
    for c in range(nc):
        gct_ref[c] = gt[:, c * c64:(c + 1) * c64]

    ri = lax.broadcasted_iota(jnp.int32, (c64, c64), 0)
    ci = lax.broadcasted_iota(jnp.int32, (c64, c64), 1)
    tril = ri >= ci
    strict = ri > ci
    eye = jnp.where(ri == ci, 1.0, 0.0).astype(F32)
    dnw = dnw_ref[...]

    hs = range(DN_HEADS)
    grp = 2 if nc % 2 == 0 else 1

    def chunk_group(cg, carry):
        items = [(j, h) for j in range(grp) for h in hs]
        n = range(len(items))
        cs = [cg * grp + j for j in range(grp)]
        gcn = [gcn_ref[c] for c in cs]
        bet = [beta_ref[c] for c in cs]
        gct = [gct_ref[c] for c in cs]
        q = [act_ref[cs[j], :, h * DN_DK:(h + 1) * DN_DK] for j, h in items]
        k = [act_ref[cs[j], :, (DN_HEADS + h) * DN_DK:(DN_HEADS + h + 1) * DN_DK] for j, h in items]
        v = [act_ref[cs[j], :, 2 * DN_HEADS * DN_DK + h * DN_DV:2 * DN_HEADS * DN_DK + (h + 1) * DN_DV]
             for j, h in items]
        q = [x * lax.rsqrt(jnp.sum(x * x, -1, keepdims=True) + EPS) * (DN_DK ** -0.5) for x in q]
        k = [x * lax.rsqrt(jnp.sum(x * x, -1, keepdims=True) + EPS) for x in k]
        bcol = [bet[j][:, h:h + 1] for j, h in items]
        gcol = [gcn[j][:, DN_HEADS + h:DN_HEADS + h + 1] for j, h in items]
        grow = [gct[j][h:h + 1, :] for j, h in items]
        decay = [jnp.where(tril, jnp.exp(jnp.where(tril, gcol[i] - grow[i], 0.0)), 0.0) for i in n]
        eg = [jnp.exp(x) for x in gcol]
        glast = [x[c64 - 1:c64, :] for x in gcol]
        kb = [k[i] * bcol[i] for i in n]
        k16 = [x.astype(BF16) for x in k]
        kk = [_dot_nt(kb[i].astype(BF16), k16[i]) for i in n]
        qk = [_dot_nt(q[i].astype(BF16), k16[i]) for i in n]
        a16 = [jnp.where(tril, qk[i] * decay[i], 0.0).astype(BF16) for i in n]
        m = [jnp.where(strict, -(kk[i] * decay[i]), 0.0) for i in n]
        p = [eye + x for x in m]
        for _ in range(5):
            m = [_dot(x, x) for x in m]
            p = [p[i] + _dot(m[i], p[i]) for i in n]
        tinv = [x.astype(BF16) for x in p]
        u = [_dot(tinv[i], (v[i] * bcol[i]).astype(BF16)) for i in n]
        w16 = [_dot(tinv[i], (kb[i] * eg[i]).astype(BF16)).astype(BF16) for i in n]
        qe16 = [(q[i] * eg[i]).astype(BF16) for i in n]
        kd16 = [(k[i] * jnp.exp(glast[i] - gcol[i])).astype(BF16) for i in n]
        egl = [jnp.exp(x) for x in glast]
        for j in range(grp):
            idx = [j * DN_HEADS + h for h in hs]
            r0 = pl.multiple_of(cs[j] * c64, c64)
            st = [state_ref[h] for h in hs]
            st16 = [x.astype(BF16) for x in st]
            ws = [_dot(w16[idx[h]], st16[h]) for h in hs]
            vn16 = [(u[idx[h]] - ws[h]).astype(BF16) for h in hs]
            qs = [_dot(qe16[idx[h]], st16[h]) for h in hs]
            av = [_dot(a16[idx[h]], vn16[h]) for h in hs]
            kv = [_dot_tn(kd16[idx[h]], vn16[h]) for h in hs]
            for h in hs:
                state_ref[h] = st[h] * egl[idx[h]] + kv[h]
                o = qs[h] + av[h]
                on = o * lax.rsqrt(jnp.mean(o * o, -1, keepdims=True) + EPS) * dnw
                z = x_ref[0, pl.ds(r0, c64), DN_QKV + h * DN_DV:DN_QKV + (h + 1) * DN_DV].astype(F32)
                o_ref[0, pl.ds(r0, c64), h * DN_DV:(h + 1) * DN_DV] = (on * _silu(z)).astype(o_ref.dtype)
        return carry

    lax.fori_loop(0, nc // grp, chunk_group, 0)


def _deltanet(proj, small, a_t, conv_w, a_log, dt_bias, dn_norm, ts):
    b, s, _ = proj.shape
    wdn = DN_QKV + DN_HEADS * DN_DV
    pad = jnp.zeros((LANES - 2 * DN_HEADS,), F32)
    alog_row = jnp.concatenate([jnp.zeros((DN_HEADS,), F32), a_log, pad]).reshape(1, LANES)
    dtb_row = jnp.concatenate([jnp.zeros((DN_HEADS,), F32), dt_bias, pad]).reshape(1, LANES)
    nc = ts // DN_CHUNK
    return pl.pallas_call(
        functools.partial(_dn_body, ts=ts),
        grid=(b, s // ts),
        in_specs=[pl.BlockSpec((1, ts, wdn), lambda bi, i: (bi, i, 0)),
                  pl.BlockSpec((1, ts, LANES), lambda bi, i: (bi, i, 0)),
                  pl.BlockSpec((1, DN_HEADS, ts), lambda bi, i: (bi, 0, i)),
                  pl.BlockSpec((CONV_W, DN_QKV), lambda bi, i: (0, 0)),
                  pl.BlockSpec((1, LANES), lambda bi, i: (0, 0)),
                  pl.BlockSpec((1, LANES), lambda bi, i: (0, 0)),
                  pl.BlockSpec((DN_HEADS, 1), lambda bi, i: (0, 0)),
                  pl.BlockSpec((DN_HEADS, 1), lambda bi, i: (0, 0)),
                  pl.BlockSpec((1, DN_DV), lambda bi, i: (0, 0))],
        out_specs=pl.BlockSpec((1, ts, DN_HEADS * DN_DV), lambda bi, i: (bi, i, 0)),
        out_shape=jax.ShapeDtypeStruct((b, s, DN_HEADS * DN_DV), BF16),
        scratch_shapes=[pltpu.VMEM((ts + 8, DN_QKV), F32),
                        pltpu.VMEM((nc, DN_CHUNK, DN_QKV), F32),
                        pltpu.VMEM((nc, DN_CHUNK, LANES), F32),
                        pltpu.VMEM((nc, DN_CHUNK, LANES), F32),
                        pltpu.VMEM((nc, DN_HEADS, DN_CHUNK), F32),
                        pltpu.VMEM((DN_HEADS, DN_DK, DN_DV), F32)],
        compiler_params=_params(("parallel", "arbitrary")),
        name="deltanet",
    )(proj, small, a_t, conv_w, alog_row, dtb_row, a_log.reshape(DN_HEADS, 1),
      dt_bias.reshape(DN_HEADS, 1), dn_norm.reshape(1, DN_DV))


def _seg_ones():
    r = lax.broadcasted_iota(jnp.int32, (LANES, LANES), 0) // NSA_DH
    c = lax.broadcasted_iota(jnp.int32, (LANES, LANES), 1) // NSA_DH
    return jnp.where(r == c, 1.0, 0.0).astype(F32)


V_ROWS = 80
LOG2E = 1.4426950408889634


def _eye16(n):
    r = lax.broadcasted_iota(jnp.int32, (n, n), 0)
    c = lax.broadcasted_iota(jnp.int32, (n, n), 1)
    return jnp.where(r == c, 1.0, 0.0).astype(BF16)


def _transpose16(x16, eye):
    return _dot_tn(x16, eye).astype(BF16)


def _norm_rope(x, w, cos_f, sin_s, seg):
    ms = _dot(x * x, seg) * (1.0 / NSA_DH)
    y = x * lax.rsqrt(ms + EPS) * w
    half = NSA_DH // 2
    lane = lax.broadcasted_iota(jnp.int32, y.shape, 1) & (NSA_DH - 1)
    partner = jnp.where(lane < half, pltpu.roll(y, LANES - half, 1), pltpu.roll(y, half, 1))
    return y * cos_f + partner * sin_s


def _nsa_prep_body(q_ref, ck_ref, cv_ref, sk_ref, sv_ref, wk_ref, wv_ref, cos_ref, sin_ref,
                   qw_ref, skw_ref, wkw_ref,
                   qo_ref, sko_ref, svo_ref, wko_ref, wvo_ref, cko_ref, cvo_ref):
    seg = _seg_ones()
    cos_f = cos_ref[0]
    sin_s = sin_ref[0]
    ts = cos_f.shape[0]
    eye = _eye16(ts)
    for sl in range(NSA_HEADS * NSA_DH // LANES):
        x = q_ref[0, :, sl * LANES:(sl + 1) * LANES].astype(F32)
        y = (_norm_rope(x, qw_ref[...], cos_f, sin_s, seg) * (NSA_DH ** -0.5 * LOG2E)).astype(BF16)
        y_t = _transpose16(y, eye)
        for half in range(2):
            h = 2 * sl + half
            qo_ref[0, h // NSA_HPG, h % NSA_HPG, 0:NSA_DH, :] = y_t[half * NSA_DH:(half + 1) * NSA_DH]
            qo_ref[0, h // NSA_HPG, h % NSA_HPG, NSA_DH:LANES, :] = jnp.zeros((LANES - NSA_DH, ts), BF16)
    sk = _norm_rope(sk_ref[0].astype(F32), skw_ref[...], cos_f, sin_s, seg).astype(BF16)
    wk = _norm_rope(wk_ref[0].astype(F32), wkw_ref[...], cos_f, sin_s, seg).astype(BF16)
    sv_t = _transpose16(sv_ref[0], eye)
    wv_t = _transpose16(wv_ref[0], eye)
    ones_row = jnp.where(lax.broadcasted_iota(jnp.int32, (V_ROWS - NSA_DH, ts), 0) == 0, 1.0, 0.0).astype(BF16)
    blk = (pl.program_id(1) * ts + lax.broadcasted_iota(jnp.int32, (ts, LANES), 0)) >> SEL_SHIFT
    onehot = jnp.where(lax.broadcasted_iota(jnp.int32, (ts, LANES), 1) == blk, SEL_BIAS, 0.0).astype(BF16)
    for g in range(NSA_GROUPS):
        cols = slice(g * NSA_DH, (g + 1) * NSA_DH)
        sko_ref[0, g, :, 0:LANES] = onehot
        sko_ref[0, g, :, LANES:LANES + NSA_DH] = sk[:, cols]
        sko_ref[0, g, :, LANES + NSA_DH:2 * LANES] = jnp.zeros((ts, LANES - NSA_DH), BF16)
        wko_ref[0, g] = wk[:, cols]
        svo_ref[0, g, 0:NSA_DH, :] = sv_t[cols]
        svo_ref[0, g, NSA_DH:V_ROWS, :] = ones_row
        wvo_ref[0, g, 0:NSA_DH, :] = wv_t[cols]
        wvo_ref[0, g, NSA_DH:V_ROWS, :] = ones_row
    cko_ref[0] = ck_ref[0]
    cvo_ref[0] = cv_ref[0]


def _nsa_prep(proj, col0, cos_f, sin_s, q_norm, k_norm_s, k_norm_w, ts):
    b, s, _ = proj.shape
    assert s // SEL_LEN <= LANES
    qw = NSA_HEADS * NSA_DH
    qblk = col0 // qw
    k0 = (col0 + qw) // LANES

    def kspec(i):
        return pl.BlockSpec((1, ts, LANES), lambda bi, t, i=i: (bi, t, k0 + i))

    tile2 = lambda w: jnp.tile(w.reshape(1, NSA_DH), (1, LANES // NSA_DH))
    gshape = jax.ShapeDtypeStruct((b, NSA_GROUPS, s, NSA_DH), BF16)
    gspec = pl.BlockSpec((1, NSA_GROUPS, ts, NSA_DH), lambda bi, t: (bi, 0, t, 0))
    ashape = jax.ShapeDtypeStruct((b, NSA_GROUPS, s, 2 * LANES), BF16)
    aspec = pl.BlockSpec((1, NSA_GROUPS, ts, 2 * LANES), lambda bi, t: (bi, 0, t, 0))
    vshape = jax.ShapeDtypeStruct((b, NSA_GROUPS, V_ROWS, s), BF16)
    vspec = pl.BlockSpec((1, NSA_GROUPS, V_ROWS, ts), lambda bi, t: (bi, 0, 0, t))
    cspec = pl.BlockSpec((1, ts, LANES), lambda bi, t: (bi, t, 0))
    wspec = pl.BlockSpec((1, LANES), lambda bi, t: (0, 0))
    return pl.pallas_call(
        _nsa_prep_body,
        grid=(b, s // ts),
        in_specs=[pl.BlockSpec((1, ts, qw), lambda bi, t: (bi, t, qblk)),
                  kspec(0), kspec(1), kspec(2), kspec(3), kspec(4), kspec(5),
                  cspec, cspec, wspec, wspec, wspec],
        out_specs=[pl.BlockSpec((1, NSA_GROUPS, NSA_HPG, LANES, ts), lambda bi, t: (bi, 0, 0, 0, t)),
                   aspec, vspec, gspec, vspec, cspec, cspec],
        out_shape=[jax.ShapeDtypeStruct((b, NSA_GROUPS, NSA_HPG, LANES, s), BF16),
                   ashape, vshape, gshape, vshape,
                   jax.ShapeDtypeStruct((b, s, LANES), BF16),
                   jax.ShapeDtypeStruct((b, s, LANES), BF16)],
        compiler_params=_params(("parallel", "parallel")),
        name="nsa_prep",
    )(proj, proj, proj, proj, proj, proj, proj, cos_f, sin_s,
      tile2(q_norm), tile2(k_norm_s), tile2(k_norm_w))


def _compress_body(ck_ref, cv_ref, pos_ref, w1a_ref, w1b_ref, w2_ref, kw_ref, cos_ref, sin_ref,
                   ko_ref, vo_ref):
    n = ck_ref.shape[1]
    outs = []
    for which, x_ref in enumerate((ck_ref, cv_ref)):
        x = x_ref[0].astype(F32)
        lo = _dot((x + pos_ref[which, 0:1, :]).astype(BF16), w1a_ref[which])
        hi = _dot((x + pos_ref[which, 1:2, :]).astype(BF16), w1b_ref[which])
        h1 = _silu(lo + pltpu.roll(hi, n - 1, 0))
        outs.append(_dot(h1.astype(BF16), w2_ref[which]))
    kc = _norm_rope(outs[0], kw_ref[...], cos_ref[0], sin_ref[0], _seg_ones()).astype(BF16)
    vc_t = _transpose16(outs[1].astype(BF16), _eye16(n))
    for g in range(NSA_GROUPS):
        ko_ref[0, g] = kc[:, g * NSA_DH:(g + 1) * NSA_DH]
        vo_ref[0, g] = vc_t[g * NSA_DH:(g + 1) * NSA_DH]


def _compress(ck, cv, cmp_pos, w_cmp1, w_cmp2, k_norm_c, cos_c, sin_c):
    b, s, _ = ck.shape
    n = s // CMP_STRIDE
    width = CMP_STRIDE * LANES
    per_row = CMP_LEN // CMP_STRIDE
    eye_g = jnp.eye(NSA_GROUPS, dtype=F32)
    w1 = w_cmp1.reshape(2, per_row, CMP_STRIDE, NSA_DH, NSA_DH)
    w1 = jnp.einsum('khldo,gG->khlgdGo', w1, eye_g).reshape(2, per_row, width, LANES).astype(BF16)
    w2 = jnp.einsum('kdo,gG->kgdGo', w_cmp2, eye_g).reshape(2, LANES, LANES).astype(BF16)
    pos = jnp.broadcast_to(cmp_pos.reshape(2, per_row, CMP_STRIDE, 1, NSA_DH),
                           (2, per_row, CMP_STRIDE, NSA_GROUPS, NSA_DH)).reshape(2, per_row, width)
    kw = jnp.tile(k_norm_c.reshape(1, NSA_DH), (1, NSA_GROUPS))
    full = lambda shp: pl.BlockSpec(shp, lambda bi: (0,) * len(shp))
    bspec = pl.BlockSpec((1, n, width), lambda bi: (bi, 0, 0))
    tspec = pl.BlockSpec((1, n, LANES), lambda bi: (bi, 0, 0))
    ospec = pl.BlockSpec((1, NSA_GROUPS, n, NSA_DH), lambda bi: (bi, 0, 0, 0))
    oshape = jax.ShapeDtypeStruct((b, NSA_GROUPS, n, NSA_DH), BF16)
    return pl.pallas_call(
        _compress_body,
        grid=(b,),
        in_specs=[bspec, bspec, full((2, per_row, width)), full((2, width, LANES)),
                  full((2, width, LANES)), full((2, LANES, LANES)), full((1, LANES)), tspec, tspec],
        out_specs=[ospec, pl.BlockSpec((1, NSA_GROUPS, NSA_DH, n), lambda bi: (bi, 0, 0, 0))],
        out_shape=[oshape, jax.ShapeDtypeStruct((b, NSA_GROUPS, NSA_DH, n), BF16)],
        compiler_params=_params(("parallel",)),
        name="nsa_compress",
    )(ck.reshape(b, n, width), cv.reshape(b, n, width), pos, w1[:, 0], w1[:, 1], w2, kw, cos_c, sin_c)


def _nsa_body(qt_ref, kc_ref, vct_ref, ksa_ref, vst_ref, kw_ref, vwt_ref, gt_ref, o_ref,
              qa_ref, sa_ref, sb_ref, os_ref, *, tq, tk, n_sel):
    hp = NSA_HPG
    hs = range(hp)
    t0 = pl.program_id(2) * tq
    q_t = [qt_ref[0, 0, j, 0:NSA_DH, :] for j in hs]
    t_row = t0 + lax.broadcasted_iota(jnp.int32, (1, tq), 1)

    kc = kc_ref[0, 0]
    vc_t = vct_ref[0, 0]
    n_cmp = kc.shape[0]
    cmp_end = lax.broadcasted_iota(jnp.int32, (n_cmp, 1), 0) * CMP_STRIDE + (CMP_LEN - 1)
    bias_c = jnp.where(cmp_end <= t_row, 0.0, NEG)
    valid_c = jnp.where(t_row >= CMP_LEN - 1, 1.0, 0.0)
    cs = lax.broadcasted_iota(jnp.int32, (LANES, n_cmp), 1) * CMP_STRIDE
    bs = lax.broadcasted_iota(jnp.int32, (LANES, n_cmp), 0) * SEL_LEN
    overlap_t = jnp.where((cs < bs + SEL_LEN) & (cs + CMP_LEN > bs), 1.0, 0.0).astype(BF16)
    o_c = []
    imp = None
    s_c = [_dot(kc, q_t[j]) for j in hs]
    wl = WINDOW + tq
    w0 = pl.multiple_of(jnp.maximum(t0 - WINDOW, 0), tq)
    k_w = kw_ref[0, 0, pl.ds(w0, wl), :]
    for j in hs:
        sb_ref[j, 0:wl, :] = _dot(k_w, q_t[j])
    for j in hs:
        sc = s_c[j] + bias_c
        e_c = jnp.exp2(sc - jnp.max(sc, 0, keepdims=True))
        p16 = (e_c * (valid_c / jnp.sum(e_c, 0, keepdims=True))).astype(BF16)
        o_c.append(_dot(vc_t, p16))
        part = _dot(overlap_t, p16)
        imp = part if imp is None else imp + part

    jb = lax.broadcasted_iota(jnp.int32, (LANES, 1), 0)
    cur = t_row >> SEL_SHIFT
    forced = (jb == 0) | (jb == cur) | (jb == cur - 1)
    imp = jnp.where(forced, FORCE, jnp.where(jb * SEL_LEN <= t_row, imp, -FORCE))
    imp = jnp.where(jb < n_sel, imp, -jnp.inf)
    jbf = jb.astype(F32)
    for _ in range(min(SEL_TOP, n_sel)):
        mx = jnp.max(imp, 0, keepdims=True)
        first = jnp.min(jnp.where(imp == mx, jbf, float(LANES)), 0, keepdims=True)
        imp = jnp.where(jbf == first, -jnp.inf, imp)
    selm1_t = jnp.where((imp == -jnp.inf) & (jb < n_sel), 0.0, -1.0).astype(BF16)
    for j in hs:
        qa_ref[j, 0:LANES, :] = selm1_t
        qa_ref[j, LANES:2 * LANES, :] = qt_ref[0, 0, j]

    def scores(kt, s_ref):
        k_aug = ksa_ref[0, 0, pl.ds(pl.multiple_of(kt * tk, tk), tk), :]
        for j in hs:
            s_ref[j] = _dot(k_aug, qa_ref[j])

    def softmax_pv(kt, s_ref, state, diagonal):
        k0 = pl.multiple_of(kt * tk, tk)
        m, acc = list(state[0]), list(state[1])
        v_t = vst_ref[0, 0, :, pl.ds(k0, tk)]
        if diagonal:
            tok = k0 + lax.broadcasted_iota(jnp.int32, (tk, 1), 0)
            bias = jnp.where(tok <= t_row, 0.0, NEG)
        for j in hs:
            sj = s_ref[j] + bias if diagonal else s_ref[j]
            m_new = jnp.maximum(m[j], jnp.max(sj, 0, keepdims=True))
            alpha = jnp.exp2(m[j] - m_new)
            e16 = jnp.exp2((sj - m_new).astype(BF16))
            acc[j] = alpha * acc[j] + _dot(v_t, e16)
            m[j] = m_new
        return tuple(m), tuple(acc)

    def tile_pair(p, state):
        scores(2 * p + 1, sb_ref)
        state = softmax_pv(2 * p, sa_ref, state, False)
        scores(2 * p + 2, sa_ref)
        return softmax_pv(2 * p + 1, sb_ref, state, False)

    def finish(state):
        acc = state[1]
        for j in hs:
            os_ref[j] = acc[j][0:NSA_DH] / jnp.maximum(acc[j][NSA_DH:NSA_DH + 1], 1e-30)

    k_diag = t0 // tk
    state0 = (tuple(jnp.full((1, tq), NEG, F32) for _ in hs),
              tuple(jnp.zeros((V_ROWS, tq), F32) for _ in hs))
    scores(0, sa_ref)

    dist = t_row - (w0 + lax.broadcasted_iota(jnp.int32, (wl, 1), 0))
    bias_w = jnp.where((dist >= 0) & (dist < WINDOW), 0.0, NEG)
    vw_t = vwt_ref[0, 0, :, pl.ds(w0, wl)]
    o_w = []
    for j in hs:
        sw = sb_ref[j, 0:wl, :] + bias_w
        e16 = jnp.exp2((sw - jnp.max(sw, 0, keepdims=True)).astype(BF16))
        oa = _dot(vw_t, e16)
        o_w.append(oa[0:NSA_DH] / oa[NSA_DH:NSA_DH + 1])

    state = lax.fori_loop(0, k_diag // 2, tile_pair, state0)

    @pl.when(k_diag % 2 == 0)
    def _():
        finish(softmax_pv(k_diag, sa_ref, state, True))

    @pl.when(k_diag % 2 == 1)
    def _():
        scores(k_diag, sb_ref)
        finish(softmax_pv(k_diag, sb_ref, softmax_pv(k_diag - 1, sa_ref, state, False), True))

    o_s = [os_ref[j] for j in hs]

    gates = _sigmoid(gt_ref[0, 0])
    for j in range(hp):
        o = (gates[3 * j:3 * j + 1] * o_c[j] + gates[3 * j + 1:3 * j + 2] * o_s[j]
             + gates[3 * j + 2:3 * j + 3] * o_w[j])
        o_ref[0, j * NSA_DH:(j + 1) * NSA_DH, :] = o.astype(o_ref.dtype)


def _nsa_attention(q_t, kc, vc_t, ksa, vs_t, kw, vw_t, gates_t, tq, tk):
    b, g, hp, _, s = q_t.shape
    dh = NSA_DH
    n_cmp = kc.shape[2]
    assert s >= WINDOW + tq and WINDOW % tq == 0 and s % tk == 0 and tk % tq == 0 and tq % LANES == 0
    vt_spec = pl.BlockSpec((1, 1, V_ROWS, s), lambda bi, gi, i: (bi, gi, 0, 0))
    return pl.pallas_call(
        functools.partial(_nsa_body, tq=tq, tk=tk, n_sel=s // SEL_LEN),
        grid=(b, g, s // tq),
        in_specs=[pl.BlockSpec((1, 1, hp, LANES, tq), lambda bi, gi, i: (bi, gi, 0, 0, i)),
                  pl.BlockSpec((1, 1, n_cmp, dh), lambda bi, gi, i: (bi, gi, 0, 0)),
                  pl.BlockSpec((1, 1, dh, n_cmp), lambda bi, gi, i: (bi, gi, 0, 0)),
                  pl.BlockSpec((1, 1, s, 2 * LANES), lambda bi, gi, i: (bi, gi, 0, 0)),
                  vt_spec,
                  pl.BlockSpec((1, 1, s, dh), lambda bi, gi, i: (bi, gi, 0, 0)),
                  vt_spec,
                  pl.BlockSpec((1, 1, 3 * hp, tq), lambda bi, gi, i: (bi, gi, 0, i))],
        out_specs=pl.BlockSpec((1, hp * dh, tq), lambda bi, gi, i: (bi, gi, i)),
        out_shape=jax.ShapeDtypeStruct((b, g * hp * dh, s), BF16),
        scratch_shapes=[pltpu.VMEM((hp, 2 * LANES, tq), BF16), pltpu.VMEM((hp, tk, tq), F32),
                        pltpu.VMEM((hp, tk, tq), F32), pltpu.VMEM((hp, dh, tq), F32)],
        compiler_params=_params(("parallel", "parallel", "arbitrary")),
        name="nsa_attention",
    )(q_t, kc, vc_t, ksa, vs_t, kw, vw_t, gates_t)


def _mix_out_body(x_ref, oa_ref, ob_ref, mg_ref, gm_ref, woa_ref, wob_ref, wout_ref, o_ref):
    d = x_ref.shape[2]
    y_a = _dot(oa_ref[0], woa_ref[...])
    y_b = _dot_tn(ob_ref[0], wob_ref[...])
    merged = (_sigmoid(mg_ref[0, :, 0:d].astype(F32)) * y_a
              + _sigmoid(mg_ref[0, :, d:2 * d].astype(F32)) * y_b)
    o_ref[0] = x_ref[0] + gm_ref[0] * _dot(merged.astype(BF16), wout_ref[...])


def _mix_out(x, o_a, o_b, proj, mg_blk, g_m, w_oa, w_ob, w_out, tm):
    b, s, d = x.shape
    full = lambda a: pl.BlockSpec(a.shape, lambda bi, i: (0, 0))
    return pl.pallas_call(
        _mix_out_body,
        grid=(b, s // tm),
        in_specs=[pl.BlockSpec((1, tm, d), lambda bi, i: (bi, i, 0)),
                  pl.BlockSpec((1, tm, o_a.shape[2]), lambda bi, i: (bi, i, 0)),
                  pl.BlockSpec((1, o_b.shape[1], tm), lambda bi, i: (bi, 0, i)),
                  pl.BlockSpec((1, tm, 2 * d), lambda bi, i: (bi, i, mg_blk)),
                  pl.BlockSpec((1, 1, d), lambda bi, i: (bi, 0, 0)),
                  full(w_oa), full(w_ob), full(w_out)],
        out_specs=pl.BlockSpec((1, tm, d), lambda bi, i: (bi, i, 0)),
        out_shape=jax.ShapeDtypeStruct(x.shape, F32),
        compiler_params=_params(("parallel", "parallel")),
        name="mix_out",
    )(x, o_a, o_b, proj, g_m, w_oa, w_ob, w_out)


def _ffn_body(x_ref, nw_ref, sc_ref, sh_ref, gf_ref, w1_ref, w3_ref, w2_ref, o_ref, h_ref, acc_ref):
    f = pl.program_id(2)

    @pl.when(f == 0)
    def _():
        h_ref[...] = _norm_mod(x_ref[0], nw_ref[...], sc_ref[0], sh_ref[0]).astype(BF16)
        acc_ref[...] = jnp.zeros_like(acc_ref)

    h = h_ref[...]
    t = _silu(_dot(h, w1_ref[...])) * _dot(h, w3_ref[...])
    acc_ref[...] += _dot(t.astype(BF16), w2_ref[...])

    @pl.when(f == pl.num_programs(2) - 1)
    def _():
        o_ref[0] = x_ref[0] + gf_ref[0] * acc_ref[...]


def _dense_ffn(x, nw, sc, sh, g_f, w1, w3, w2, tm, tf):
    b, s, d = x.shape
    ff = w1.shape[1]
    vec = pl.BlockSpec((1, 1, d), lambda bi, i, f: (bi, 0, 0))
    return pl.pallas_call(
        _ffn_body,
        grid=(b, s // tm, ff // tf),
        in_specs=[pl.BlockSpec((1, tm, d), lambda bi, i, f: (bi, i, 0)),
                  pl.BlockSpec((1, d), lambda bi, i, f: (0, 0)), vec, vec, vec,
                  pl.BlockSpec((d, tf), lambda bi, i, f: (0, f)),
                  pl.BlockSpec((d, tf), lambda bi, i, f: (0, f)),
                  pl.BlockSpec((tf, d), lambda bi, i, f: (f, 0))],
        out_specs=pl.BlockSpec((1, tm, d), lambda bi, i, f: (bi, i, 0)),
        out_shape=jax.ShapeDtypeStruct(x.shape, F32),
        scratch_shapes=[pltpu.VMEM((tm, d), BF16), pltpu.VMEM((tm, d), F32)],
        compiler_params=_params(("parallel", "parallel", "arbitrary")),
        name="dense_ffn",
    )(x, nw, sc, sh, g_f, w1, w3, w2)


MOE_ROWS = 64
MOE_GROUP = 4
MOE_SCATTER = 512


def _moe_route_body(x_ref, nw_ref, sc_ref, sh_ref, wr_ref, h_ref, route_ref, meta_ref):
    tm = x_ref.shape[1]
    h16 = _norm_mod(x_ref[0], nw_ref[...], sc_ref[0], sh_ref[0]).astype(BF16)
    h_ref[0] = h16
    logits = _dot_nt(wr_ref[...], h16)
    ef = lax.broadcasted_iota(jnp.int32, (N_EXPERTS, tm), 0).astype(F32)
    m1 = jnp.max(logits, 0, keepdims=True)
    i1 = jnp.min(jnp.where(logits == m1, ef, float(N_EXPERTS)), 0, keepdims=True)
    rest = jnp.where(ef == i1, -jnp.inf, logits)
    m2 = jnp.max(rest, 0, keepdims=True)
    i2 = jnp.min(jnp.where(rest == m2, ef, float(N_EXPERTS)), 0, keepdims=True)
    e2 = jnp.exp(m2 - m1)
    oh1 = jnp.where(ef == i1, 1.0, 0.0)
    oh2 = jnp.where(ef == i2, 1.0, 0.0)
    member = oh1 + oh2
    lane = lax.broadcasted_iota(jnp.int32, (N_EXPERTS, tm), 1)
    csum = member
    sft = 1
    while sft < tm:
        csum = csum + jnp.where(lane >= sft, pltpu.roll(csum, sft, 1), 0.0)
        sft *= 2
    count = jnp.max(csum, 1, keepdims=True)
    nblk = jnp.floor((count + (MOE_ROWS - 1)) * (1.0 / MOE_ROWS))
    nblk_b = jnp.broadcast_to(nblk, (N_EXPERTS, LANES))
    row = lax.broadcasted_iota(jnp.int32, (N_EXPERTS, LANES), 0)
    bsum = nblk_b
    for sft in (1, 2, 4):
        bsum = bsum + jnp.where(row >= sft, pltpu.roll(bsum, sft, 0), 0.0)
    bstart = bsum - nblk_b
    slot = bstart[:, 0:1] * MOE_ROWS + (csum - member)
    rrow = lax.broadcasted_iota(jnp.int32, (8, tm), 0)
    route_ref[0] = jnp.where(
        rrow == 0, jnp.sum(oh1 * slot, 0, keepdims=True),
        jnp.where(rrow == 1, jnp.sum(oh2 * slot, 0, keepdims=True),
                  jnp.where(rrow == 2, 1.0 / (1.0 + e2), jnp.where(rrow == 3, e2 / (1.0 + e2), 0.0))))
    col = lax.broadcasted_iota(jnp.int32, (N_EXPERTS, LANES), 1)
    meta_ref[0] = jnp.where(col == 0, nblk_b, jnp.where(col == 1, bstart, 0.0)).astype(jnp.int32)


def _moe_group_body(nblk_ref, bstart_ref, x_ref, h_ref, route_ref, gf_ref, w1_ref, w3_ref, w2_ref, o_ref,
                    hb_ref, cw_ref, acc_ref, *, n_rows):
    i = pl.program_id(0)
    e = pl.program_id(1)
    f = pl.program_id(2)
    tm = x_ref.shape[1]
    nb = nblk_ref[i * N_EXPERTS + e]
    b0 = bstart_ref[i * N_EXPERTS + e]
    slot1 = route_ref[0, 0:1, :]
    slot2 = route_ref[0, 1:2, :]

    def hits(r0, rows):
        rr = (r0 + lax.broadcasted_iota(jnp.int32, (rows, 1), 0)).astype(F32)
        return rr == slot1, rr == slot2

    def expert_rows(blk, n_blk, first):
        r0 = pl.multiple_of((b0 + blk) * MOE_ROWS, MOE_ROWS)
        rows = pl.ds(r0, n_blk * MOE_ROWS)
        if first:
            hit1, hit2 = hits(r0, n_blk * MOE_ROWS)
            gather = jnp.where(hit1, 1.0, jnp.where(hit2, 1.0, 0.0)).astype(BF16)
            hb_ref[rows, :] = _dot(gather, h_ref[0]).astype(BF16)
            cw_ref[rows, :] = jnp.sum(jnp.where(hit1, route_ref[0, 2:3, :], 0.0)
                                      + jnp.where(hit2, route_ref[0, 3:4, :], 0.0), -1, keepdims=True)
        hb = hb_ref[rows, :]
        t = _silu(_dot(hb, w1_ref[0, 0])) * _dot(hb, w3_ref[0, 0]) * cw_ref[rows, :]
        y = _dot(t.astype(BF16), w2_ref[0])
        if first:
            acc_ref[rows, :] = y
        else:
            acc_ref[rows, :] += y

    def expert_all(first):
        lax.fori_loop(0, nb // MOE_GROUP, lambda k, c: (expert_rows(MOE_GROUP * k, MOE_GROUP, first), c)[1], 0)
        for rem in range(1, MOE_GROUP):
            @pl.when(nb % MOE_GROUP == rem)
            def _(rem=rem):
                expert_rows(nb - rem, rem, first)

    @pl.when(f == 0)
    def _():
        expert_all(True)

    @pl.when(f != 0)
    def _():
        expert_all(False)

    @pl.when((e == pl.num_programs(1) - 1) & (f == pl.num_programs(2) - 1))
    def _():
        def clear(k, c):
            acc_ref[pl.ds(pl.multiple_of(k * MOE_ROWS, MOE_ROWS), MOE_ROWS), :] = jnp.zeros(
                (MOE_ROWS, acc_ref.shape[1]), F32)
            return c

        lax.fori_loop(b0 + nb, n_rows // MOE_ROWS, clear, 0)
        chunk = MOE_SCATTER
        for kc in range(n_rows // chunk):
            hit1, hit2 = hits(kc * chunk, chunk)
            scatter = jnp.where(hit1, 1.0, jnp.where(hit2, 1.0, 0.0)).astype(BF16)
            y = _dot_tn(scatter, acc_ref[kc * chunk:(kc + 1) * chunk, :].astype(BF16))
            if kc == 0:
                o_ref[0] = y
            else:
                o_ref[0] += y
        o_ref[0] = x_ref[0] + gf_ref[0] * o_ref[0]


def _moe_ffn(x, nw, sc, sh, g_f, w_router, w1, w3, w2, tm, tf):
    b, s, d = x.shape
    n_e, _, ff = w1.shape
    assert n_e == N_EXPERTS
    tiles_b = s // tm
    nt = b * tiles_b
    vec = pl.BlockSpec((1, 1, d), lambda bi, i: (bi, 0, 0))
    h16, route, meta = pl.pallas_call(
        _moe_route_body,
        grid=(b, tiles_b),
        in_specs=[pl.BlockSpec((1, tm, d), lambda bi, i: (bi, i, 0)),
                  pl.BlockSpec((1, d), lambda bi, i: (0, 0)), vec, vec,
                  pl.BlockSpec((n_e, d), lambda bi, i: (0, 0))],
        out_specs=[pl.BlockSpec((1, tm, d), lambda bi, i: (bi, i, 0)),
                   pl.BlockSpec((1, 8, tm), lambda bi, i: (bi * tiles_b + i, 0, 0)),
                   pl.BlockSpec((1, n_e, LANES), lambda bi, i: (bi * tiles_b + i, 0, 0))],
        out_shape=[jax.ShapeDtypeStruct((b, s, d), BF16),
                   jax.ShapeDtypeStruct((nt, 8, tm), F32),
                   jax.ShapeDtypeStruct((nt, n_e, LANES), jnp.int32)],
        compiler_params=_params(("parallel", "parallel")),
        name="moe_route",
    )(x, nw, sc, sh, w_router.T.astype(BF16))
    chunked = lambda w: w.reshape(n_e, d, ff // tf, tf).transpose(0, 2, 1, 3)
    n_rows = -(-(2 * tm + n_e * (MOE_ROWS - 1)) // MOE_SCATTER) * MOE_SCATTER
    grid_spec = pltpu.PrefetchScalarGridSpec(
        num_scalar_prefetch=2,
        grid=(nt, n_e, ff // tf),
        in_specs=[pl.BlockSpec((1, tm, d), lambda i, e, f, nb, bs: (i, 0, 0)),
                  pl.BlockSpec((1, tm, d), lambda i, e, f, nb, bs: (i, 0, 0)),
                  pl.BlockSpec((1, 8, tm), lambda i, e, f, nb, bs: (i, 0, 0)),
                  pl.BlockSpec((1, 1, d), lambda i, e, f, nb, bs: (i // tiles_b, 0, 0)),
                  pl.BlockSpec((1, 1, d, tf), lambda i, e, f, nb, bs: (e, f, 0, 0)),
                  pl.BlockSpec((1, 1, d, tf), lambda i, e, f, nb, bs: (e, f, 0, 0)),
                  pl.BlockSpec((1, tf, d), lambda i, e, f, nb, bs: (e, f, 0))],
        out_specs=pl.BlockSpec((1, tm, d), lambda i, e, f, nb, bs: (i, 0, 0)),
        scratch_shapes=[pltpu.VMEM((n_rows, d), BF16), pltpu.VMEM((n_rows, 1), F32),
                        pltpu.VMEM((n_rows, d), F32)])
    out = pl.pallas_call(
        functools.partial(_moe_group_body, n_rows=n_rows),
        grid_spec=grid_spec,
        out_shape=jax.ShapeDtypeStruct((nt, tm, d), F32),
        compiler_params=_params(("parallel", "arbitrary", "arbitrary")),
        name="moe_group",
    )(meta[:, :, 0].reshape(-1), meta[:, :, 1].reshape(-1),
      x.reshape(nt, tm, d), h16.reshape(nt, tm, d), route, g_f, chunked(w1), chunked(w3), w2)
    return out.reshape(b, s, d)


def _rope_tables(pos):
    inv = 1.0 / (ROPE_THETA ** (jnp.arange(0, NSA_DH, 2, dtype=F32) / NSA_DH))
    ang = pos.astype(F32)[..., None] * inv
    cos, sin = jnp.cos(ang), jnp.sin(ang)
    reps = LANES // NSA_DH
    return (jnp.tile(jnp.concatenate([cos, cos], -1), (1, 1, reps)),
            jnp.tile(jnp.concatenate([-sin, sin], -1), (1, 1, reps)))


_SPLITS = (DN_QKV, DN_HEADS * DN_DV, DN_HEADS, DN_HEADS, NSA_HEADS * NSA_DH) + (NSA_GROUPS * NSA_DH,) * 6
_OFF = np.concatenate([[0], np.cumsum(_SPLITS)])
_OFF_NG = int(_OFF[-1])
_OFF_MG = _OFF_NG + 3 * NSA_HEADS


def kernel(x, c, positions, w_ada, b_ada, norm_mix, norm_ffn, w_in, conv_w, a_log, dt_bias, dn_norm, cmp_pos, w_cmp1, w_cmp2, q_norm, k_norm, w_oa, w_ob, w_out, w1_dense, w3_dense, w2_dense, w_router, w1_moe, w3_moe, w2_moe):
    b, s, d = x.shape
    depth = w_in.shape[0]
    wdn = DN_QKV + DN_HEADS * DN_DV
    n_small = 2 * DN_HEADS + 3 * NSA_HEADS

    cos_f, sin_s = _rope_tables(positions)
    n_cmp_pad = s // CMP_STRIDE
    cmp_end = jnp.minimum(jnp.arange(n_cmp_pad) * CMP_STRIDE + CMP_LEN - 1, s - 1)
    cos_c, sin_c = _rope_tables(positions[:, cmp_end])

    mod = _ada_mod(c, w_ada, b_ada)

    off_nq = int(_OFF[4])
    w_main = jnp.concatenate([w_in[:, :, 0:wdn], w_in[:, :, _OFF_MG:_OFF_MG + 2 * d],
                              w_in[:, :, off_nq:_OFF_NG]], -1).astype(BF16)
    w_small = jnp.concatenate([w_in[:, :, wdn:wdn + 2 * DN_HEADS], w_in[:, :, _OFF_NG:_OFF_MG],
                               jnp.zeros((depth, d, LANES - n_small), F32)], -1).astype(BF16)
    nsa_col0 = wdn + 2 * d
    n_main = w_main.shape[2]

    w_oa16, w_ob16, w_out16 = w_oa.astype(BF16), w_ob.astype(BF16), w_out.astype(BF16)
    w1d, w3d, w2d = w1_dense.astype(BF16), w3_dense.astype(BF16), w2_dense.astype(BF16)
    w1m, w3m, w2m = w1_moe.astype(BF16), w3_moe.astype(BF16), w2_moe.astype(BF16)

    for l in range(depth):
        sh_m, sc_m, g_m, sh_f, sc_f, g_f = [m.reshape(b, 1, d) for m in jnp.split(mod[l], 6, -1)]
        nw_m = norm_mix[l].reshape(1, d)
        proj = _norm_mod_matmul(x, nw_m, sc_m, sh_m, w_main[l], BF16, tm=1024, tn=n_main // 3)
        small = _norm_mod_matmul(x, nw_m, sc_m, sh_m, w_small[l], F32, tm=1024, tn=LANES)
        a_t = jnp.swapaxes(small[:, :, DN_HEADS:2 * DN_HEADS], 1, 2)
        o_a = _deltanet(proj, small, a_t, conv_w[l], a_log[l], dt_bias[l], dn_norm[l], ts=512)
        qn, ksn, vs, kwn, vw, ck, cv = _nsa_prep(proj, nsa_col0, cos_f, sin_s, q_norm[l],
                                                 k_norm[l, 1], k_norm[l, 2], ts=512)
        kc, vc = _compress(ck, cv, cmp_pos[l], w_cmp1[l], w_cmp2[l], k_norm[l, 0], cos_c, sin_c)
        gates_t = jnp.swapaxes(small[:, :, 2 * DN_HEADS:n_small], 1, 2).reshape(b, NSA_GROUPS, 3 * NSA_HPG, s)
        o_b = _nsa_attention(qn, kc, vc, ksn, vs, kwn, vw, gates_t, tq=256, tk=1024)
        x = _mix_out(x, o_a, o_b, proj, wdn // (2 * d), g_m, w_oa16[l], w_ob16[l], w_out16[l], tm=512)
        nw_f = norm_ffn[l].reshape(1, d)
        if l % 2 == 0:
            x = _dense_ffn(x, nw_f, sc_f, sh_f, g_f, w1d[l // 2], w3d[l // 2], w2d[l // 2], tm=1024, tf=512)
        else:
            x = _moe_ffn(x, nw_f, sc_f, sh_f, g_f, w_router[l // 2], w1m[l // 2], w3m[l // 2],
                         w2m[l // 2], tm=1024, tf=896)
    return x
```

```python
import functools

import jax
import jax.numpy as jnp
import numpy as np
from jax import lax
from jax.experimental import pallas as pl
from jax.experimental.pallas import tpu as pltpu

F32 = jnp.float32
BF16 = jnp.bfloat16

DN_HEADS = 8
DN_DK = 64
DN_DV = 64
DN_CHUNK = 64
CONV_W = 4
DN_QKV = DN_HEADS * (2 * DN_DK + DN_DV)
NSA_HEADS = 8
NSA_GROUPS = 2
NSA_HPG = NSA_HEADS // NSA_GROUPS
NSA_DH = 64
CMP_LEN = 32
CMP_STRIDE = 16
SEL_LEN = 64
SEL_SHIFT = 6
SEL_TOP = 16
WINDOW = 512
ROPE_THETA = 10000.0
N_EXPERTS = 8
EPS = 1e-6
NEG = -1e30
FORCE = 1e6
SEL_BIAS = 1e30

LANES = 128
VMEM_LIMIT = 56 * 1024 * 1024


def _sigmoid(x):
    return 1.0 / (1.0 + jnp.exp(-x))


def _silu(x):
    return x * _sigmoid(x)


def _softplus(x):
    return jnp.maximum(x, 0.0) + jnp.log(1.0 + jnp.exp(-jnp.abs(x)))


def _dot(a, b):
    return jnp.dot(a, b, preferred_element_type=F32)


def _dot_nt(a, b):
    return lax.dot_general(a, b, (((1,), (1,)), ((), ())), preferred_element_type=F32)


def _dot_tn(a, b):
    return lax.dot_general(a, b, (((0,), (0,)), ((), ())), preferred_element_type=F32)


def _norm_mod(x, nw, sc, sh):
    y = x * lax.rsqrt(jnp.mean(x * x, -1, keepdims=True) + EPS) * nw
    return y * (1.0 + sc) + sh


def _params(sem):
    return pltpu.CompilerParams(dimension_semantics=sem, vmem_limit_bytes=VMEM_LIMIT)


def _mod_body(c_ref, w_ref, b_ref, o_ref):
    c = c_ref[...]
    o_ref[0] = _dot(_silu(c).astype(BF16), w_ref[0].astype(BF16)) + b_ref[0]


def _ada_mod(c, w_ada, b_ada):
    n_layers, d, n = w_ada.shape
    b = c.shape[0]
    tn = n // 4
    return pl.pallas_call(
        _mod_body,
        grid=(n_layers, n // tn),
        in_specs=[pl.BlockSpec((b, d), lambda l, j: (0, 0)),
                  pl.BlockSpec((1, d, tn), lambda l, j: (l, 0, j)),
                  pl.BlockSpec((1, 1, tn), lambda l, j: (l, 0, j))],
        out_specs=pl.BlockSpec((1, b, tn), lambda l, j: (l, 0, j)),
        out_shape=jax.ShapeDtypeStruct((n_layers, b, n), F32),
        compiler_params=_params(("parallel", "parallel")),
        name="ada_mod",
    )(c, w_ada, b_ada.reshape(n_layers, 1, n))


def _nm_mm_body(x_ref, nw_ref, sc_ref, sh_ref, w_ref, ws_ref, o_ref, os_ref, h_ref):
    @pl.when(pl.program_id(2) == 0)
    def _():
        h = _norm_mod(x_ref[0], nw_ref[...], sc_ref[0], sh_ref[0]).astype(BF16)
        h_ref[...] = h
        os_ref[0] = _dot(h, ws_ref[...])

    o_ref[0] = _dot(h_ref[...], w_ref[...]).astype(o_ref.dtype)


def _norm_mod_matmul(x, nw, sc, sh, w, w_small, tm, tn):
    b, s, d = x.shape
    n = w.shape[1]
    ns = w_small.shape[1]
    return pl.pallas_call(
        _nm_mm_body,
        grid=(b, s // tm, n // tn),
        in_specs=[pl.BlockSpec((1, tm, d), lambda bi, i, j: (bi, i, 0)),
                  pl.BlockSpec((1, d), lambda bi, i, j: (0, 0)),
                  pl.BlockSpec((1, 1, d), lambda bi, i, j: (bi, 0, 0)),
                  pl.BlockSpec((1, 1, d), lambda bi, i, j: (bi, 0, 0)),
                  pl.BlockSpec((d, tn), lambda bi, i, j: (0, j)),
                  pl.BlockSpec((d, ns), lambda bi, i, j: (0, 0))],
        out_specs=[pl.BlockSpec((1, tm, tn), lambda bi, i, j: (bi, i, j)),
                   pl.BlockSpec((1, tm, ns), lambda bi, i, j: (bi, i, 0))],
        out_shape=[jax.ShapeDtypeStruct((b, s, n), BF16), jax.ShapeDtypeStruct((b, s, ns), F32)],
        scratch_shapes=[pltpu.VMEM((tm, d), BF16)],
        compiler_params=_params(("parallel", "parallel", "arbitrary")),
        name="in_proj",
    )(x, nw, sc, sh, w, w_small)


def _dn_body(x_ref, sm_ref, at_ref, cw_ref, alog_ref, dtb_ref, alogt_ref, dtbt_ref, dnw_ref, o_ref,
             buf_ref, act_ref, gcn_ref, beta_ref, gct_ref, state_ref, *, ts):
    nc = ts // DN_CHUNK
    c64 = DN_CHUNK

    @pl.when(pl.program_id(1) == 0)
    def _():
        buf_ref[0:8, :] = jnp.zeros((8, DN_QKV), F32)
        state_ref[...] = jnp.zeros_like(state_ref)

    for sl in range(DN_QKV // LANES):
        cols = slice(sl * LANES, (sl + 1) * LANES)
        buf_ref[8:ts + 8, cols] = x_ref[0, :, cols].astype(F32)
        y = cw_ref[0:1, cols] * buf_ref[5:5 + ts, cols]
        for j in range(1, CONV_W):
            y = y + cw_ref[j:j + 1, cols] * buf_ref[5 + j:5 + j + ts, cols]
        buf_ref[0:8, cols] = buf_ref[ts:ts + 8, cols]
        act_ref[:, :, cols] = _silu(y).reshape(nc, c64, LANES)

    sm = sm_ref[0]
    beta_ref[...] = _sigmoid(sm).reshape(nc, c64, LANES)
    g = -jnp.exp(alog_ref[...]) * _softplus(sm + dtb_ref[...])
    row = lax.broadcasted_iota(jnp.int32, (ts, LANES), 0) & (c64 - 1)
    for sft in (1, 2, 4, 8, 16, 32):
        g = g + jnp.where(row >= sft, pltpu.roll(g, sft, 0), 0.0)
    gcn_ref[...] = g.reshape(nc, c64, LANES)
    gt = -jnp.exp(alogt_ref[...]) * _softplus(at_ref[0] + dtbt_ref[...])
    lane = lax.broadcasted_iota(jnp.int32, (DN_HEADS, ts), 1) & (c64 - 1)
    for sft in (1, 2, 4, 8, 16, 32):
        gt = gt + jnp.where(lane >= sft, pltpu.roll(gt, sft, 1), 0.0)
    for c in range(nc):
        gct_ref[c] = gt[:, c * c64:(c + 1) * c64]

    ri = lax.broadcasted_iota(jnp.int32, (c64, c64), 0)
    ci = lax.broadcasted_iota(jnp.int32, (c64, c64), 1)
    tril = ri >= ci
    strict = ri > ci
    eye = jnp.where(ri == ci, 1.0, 0.0).astype(F32)
    dnw = dnw_ref[...]

    hs = range(DN_HEADS)
    grp = 2 if nc % 2 == 0 else 1

    def chunk_group(cg, carry):
        items = [(j, h) for j in range(grp) for h in hs]
        n = range(len(items))
        cs = [cg * grp + j for j in range(grp)]
        gcn = [gcn_ref[c] for c in cs]
        bet = [beta_ref[c] for c in cs]
        gct = [gct_ref[c] for c in cs]
        q = [act_ref[cs[j], :, h * DN_DK:(h + 1) * DN_DK] for j, h in items]
        k = [act_ref[cs[j], :, (DN_HEADS + h) * DN_DK:(DN_HEADS + h + 1) * DN_DK] for j, h in items]
        v = [act_ref[cs[j], :, 2 * DN_HEADS * DN_DK + h * DN_DV:2 * DN_HEADS * DN_DK + (h + 1) * DN_DV]
             for j, h in items]
        q = [x * lax.rsqrt(jnp.sum(x * x, -1, keepdims=True) + EPS) * (DN_DK ** -0.5) for x in q]
        k = [x * lax.rsqrt(jnp.sum(x * x, -1, keepdims=True) + EPS) for x in k]
        bcol = [bet[j][:, h:h + 1] for j, h in items]
        gcol = [gcn[j][:, DN_HEADS + h:DN_HEADS + h + 1] for j, h in items]
        grow = [gct[j][h:h + 1, :] for j, h in items]
        decay = [jnp.where(tril, jnp.exp(jnp.where(tril, gcol[i] - grow[i], 0.0)), 0.0) for i in n]
        eg = [jnp.exp(x) for x in gcol]
        glast = [x[c64 - 1:c64, :] for x in gcol]
        kb = [k[i] * bcol[i] for i in n]
        k16 = [x.astype(BF16) for x in k]
        kk = [_dot_nt(kb[i].astype(BF16), k16[i]) for i in n]
        qk = [_dot_nt(q[i].astype(BF16), k16[i]) for i in n]
        a16 = [jnp.where(tril, qk[i] * decay[i], 0.0).astype(BF16) for i in n]
        m = [jnp.where(strict, -(kk[i] * decay[i]), 0.0) for i in n]
        p = [eye + x for x in m]
        for _ in range(5):
            m = [_dot(x, x) for x in m]
            p = [p[i] + _dot(m[i], p[i]) for i in n]
        tinv = [x.astype(BF16) for x in p]
        u = [_dot(tinv[i], (v[i] * bcol[i]).astype(BF16)) for i in n]
        w16 = [_dot(tinv[i], (kb[i] * eg[i]).astype(BF16)).astype(BF16) for i in n]
        qe16 = [(q[i] * eg[i]).astype(BF16) for i in n]
        kd16 = [(k[i] * jnp.exp(glast[i] - gcol[i])).astype(BF16) for i in n]
        egl = [jnp.exp(x) for x in glast]
        for j in range(grp):
            idx = [j * DN_HEADS + h for h in hs]
            r0 = pl.multiple_of(cs[j] * c64, c64)
            st = [state_ref[h] for h in hs]
            st16 = [x.astype(BF16) for x in st]
            ws = [_dot(w16[idx[h]], st16[h]) for h in hs]
            vn16 = [(u[idx[h]] - ws[h]).astype(BF16) for h in hs]
            qs = [_dot(qe16[idx[h]], st16[h]) for h in hs]
            av = [_dot(a16[idx[h]], vn16[h]) for h in hs]
            kv = [_dot_tn(kd16[idx[h]], vn16[h]) for h in hs]
            for h in hs:
                state_ref[h] = st[h] * egl[idx[h]] + kv[h]
                o = qs[h] + av[h]
                on = o * lax.rsqrt(jnp.mean(o * o, -1, keepdims=True) + EPS) * dnw
                z = x_ref[0, pl.ds(r0, c64), DN_QKV + h * DN_DV:DN_QKV + (h + 1) * DN_DV].astype(F32)
                o_ref[0, pl.ds(r0, c64), h * DN_DV:(h + 1) * DN_DV] = (on * _silu(z)).astype(o_ref.dtype)
        return carry

    lax.fori_loop(0, nc // grp, chunk_group, 0)


def _deltanet(proj, small, a_t, conv_w, a_log, dt_bias, dn_norm, ts):
    b, s, _ = proj.shape
    wdn = DN_QKV + DN_HEADS * DN_DV
    pad = jnp.zeros((LANES - 2 * DN_HEADS,), F32)
    alog_row = jnp.concatenate([jnp.zeros((DN_HEADS,), F32), a_log, pad]).reshape(1, LANES)
    dtb_row = jnp.concatenate([jnp.zeros((DN_HEADS,), F32), dt_bias, pad]).reshape(1, LANES)
    nc = ts // DN_CHUNK
    return pl.pallas_call(
        functools.partial(_dn_body, ts=ts),
        grid=(b, s // ts),
        in_specs=[pl.BlockSpec((1, ts, wdn), lambda bi, i: (bi, i, 0)),
                  pl.BlockSpec((1, ts, LANES), lambda bi, i: (bi, i, 0)),
                  pl.BlockSpec((1, DN_HEADS, ts), lambda bi, i: (bi, 0, i)),
                  pl.BlockSpec((CONV_W, DN_QKV), lambda bi, i: (0, 0)),
                  pl.BlockSpec((1, LANES), lambda bi, i: (0, 0)),
                  pl.BlockSpec((1, LANES), lambda bi, i: (0, 0)),
                  pl.BlockSpec((DN_HEADS, 1), lambda bi, i: (0, 0)),
                  pl.BlockSpec((DN_HEADS, 1), lambda bi, i: (0, 0)),
                  pl.BlockSpec((1, DN_DV), lambda bi, i: (0, 0))],
        out_specs=pl.BlockSpec((1, ts, DN_HEADS * DN_DV), lambda bi, i: (bi, i, 0)),
        out_shape=jax.ShapeDtypeStruct((b, s, DN_HEADS * DN_DV), BF16),
        scratch_shapes=[pltpu.VMEM((ts + 8, DN_QKV), F32),
                        pltpu.VMEM((nc, DN_CHUNK, DN_QKV), F32),
                        pltpu.VMEM((nc, DN_CHUNK, LANES), F32),
                        pltpu.VMEM((nc, DN_CHUNK, LANES), F32),
                        pltpu.VMEM((nc, DN_HEADS, DN_CHUNK), F32),
                        pltpu.VMEM((DN_HEADS, DN_DK, DN_DV), F32)],
        compiler_params=_params(("parallel", "arbitrary")),
        name="deltanet",
    )(proj, small, a_t, conv_w, alog_row, dtb_row, a_log.reshape(DN_HEADS, 1),
      dt_bias.reshape(DN_HEADS, 1), dn_norm.reshape(1, DN_DV))


def _seg_ones():
    r = lax.broadcasted_iota(jnp.int32, (LANES, LANES), 0) // NSA_DH
    c = lax.broadcasted_iota(jnp.int32, (LANES, LANES), 1) // NSA_DH
    return jnp.where(r == c, 1.0, 0.0).astype(F32)


V_ROWS = 80
LOG2E = 1.4426950408889634


def _eye16(n):
    r = lax.broadcasted_iota(jnp.int32, (n, n), 0)
    c = lax.broadcasted_iota(jnp.int32, (n, n), 1)
    return jnp.where(r == c, 1.0, 0.0).astype(BF16)


def _transpose16(x16, eye):
    return _dot_tn(x16, eye).astype(BF16)


def _norm_rope(x, w, cos_f, sin_s, seg):
    ms = _dot(x * x, seg) * (1.0 / NSA_DH)
    y = x * lax.rsqrt(ms + EPS) * w
    half = NSA_DH // 2
    lane = lax.broadcasted_iota(jnp.int32, y.shape, 1) & (NSA_DH - 1)
    partner = jnp.where(lane < half, pltpu.roll(y, LANES - half, 1), pltpu.roll(y, half, 1))
    return y * cos_f + partner * sin_s


def _nsa_prep_body(q_ref, ck_ref, cv_ref, sk_ref, sv_ref, wk_ref, wv_ref, cos_ref, sin_ref,
                   qw_ref, skw_ref, wkw_ref,
                   qo_ref, sko_ref, svo_ref, wko_ref, wvo_ref, cko_ref, cvo_ref):
    seg = _seg_ones()
    cos_f = cos_ref[0]
    sin_s = sin_ref[0]
    ts = cos_f.shape[0]
    eye = _eye16(ts)
    for sl in range(NSA_HEADS * NSA_DH // LANES):
        x = q_ref[0, :, sl * LANES:(sl + 1) * LANES].astype(F32)
        y = (_norm_rope(x, qw_ref[...], cos_f, sin_s, seg) * (NSA_DH ** -0.5 * LOG2E)).astype(BF16)
        y_t = _transpose16(y, eye)
        for half in range(2):
            h = 2 * sl + half
            qo_ref[0, h // NSA_HPG, h % NSA_HPG, 0:NSA_DH, :] = y_t[half * NSA_DH:(half + 1) * NSA_DH]
            qo_ref[0, h // NSA_HPG, h % NSA_HPG, NSA_DH:LANES, :] = jnp.zeros((LANES - NSA_DH, ts), BF16)
    sk = _norm_rope(sk_ref[0].astype(F32), skw_ref[...], cos_f, sin_s, seg).astype(BF16)
    wk = _norm_rope(wk_ref[0].astype(F32), wkw_ref[...], cos_f, sin_s, seg).astype(BF16)
    sv_t = _transpose16(sv_ref[0], eye)
    wv_t = _transpose16(wv_ref[0], eye)
    ones_row = jnp.where(lax.broadcasted_iota(jnp.int32, (V_ROWS - NSA_DH, ts), 0) == 0, 1.0, 0.0).astype(BF16)
    blk = (pl.program_id(1) * ts + lax.broadcasted_iota(jnp.int32, (ts, LANES), 0)) >> SEL_SHIFT
    onehot = jnp.where(lax.broadcasted_iota(jnp.int32, (ts, LANES), 1) == blk, SEL_BIAS, 0.0).astype(BF16)
    for g in range(NSA_GROUPS):
        cols = slice(g * NSA_DH, (g + 1) * NSA_DH)
        sko_ref[0, g, :, 0:LANES] = onehot
        sko_ref[0, g, :, LANES:LANES + NSA_DH] = sk[:, cols]
        sko_ref[0, g, :, LANES + NSA_DH:2 * LANES] = jnp.zeros((ts, LANES - NSA_DH), BF16)
        wko_ref[0, g] = wk[:, cols]
        svo_ref[0, g, 0:NSA_DH, :] = sv_t[cols]
        svo_ref[0, g, NSA_DH:V_ROWS, :] = ones_row
        wvo_ref[0, g, 0:NSA_DH, :] = wv_t[cols]
        wvo_ref[0, g, NSA_DH:V_ROWS, :] = ones_row
    cko_ref[0] = ck_ref[0]
    cvo_ref[0] = cv_ref[0]


def _nsa_prep(proj, col0, cos_f, sin_s, q_norm, k_norm_s, k_norm_w, ts):
    b, s, _ = proj.shape
    assert s // SEL_LEN <= LANES
    qw = NSA_HEADS * NSA_DH
    qblk = col0 // qw
    k0 = (col0 + qw) // LANES

    def kspec(i):
        return pl.BlockSpec((1, ts, LANES), lambda bi, t, i=i: (bi, t, k0 + i))

    tile2 = lambda w: jnp.tile(w.reshape(1, NSA_DH), (1, LANES // NSA_DH))
    gshape = jax.ShapeDtypeStruct((b, NSA_GROUPS, s, NSA_DH), BF16)
    gspec = pl.BlockSpec((1, NSA_GROUPS, ts, NSA_DH), lambda bi, t: (bi, 0, t, 0))
    ashape = jax.ShapeDtypeStruct((b, NSA_GROUPS, s, 2 * LANES), BF16)
    aspec = pl.BlockSpec((1, NSA_GROUPS, ts, 2 * LANES), lambda bi, t: (bi, 0, t, 0))
    vshape = jax.ShapeDtypeStruct((b, NSA_GROUPS, V_ROWS, s), BF16)
    vspec = pl.BlockSpec((1, NSA_GROUPS, V_ROWS, ts), lambda bi, t: (bi, 0, 0, t))
    cspec = pl.BlockSpec((1, ts, LANES), lambda bi, t: (bi, t, 0))
    wspec = pl.BlockSpec((1, LANES), lambda bi, t: (0, 0))
    return pl.pallas_call(
        _nsa_prep_body,
        grid=(b, s // ts),
        in_specs=[pl.BlockSpec((1, ts, qw), lambda bi, t: (bi, t, qblk)),
                  kspec(0), kspec(1), kspec(2), kspec(3), kspec(4), kspec(5),
                  cspec, cspec, wspec, wspec, wspec],
        out_specs=[pl.BlockSpec((1, NSA_GROUPS, NSA_HPG, LANES, ts), lambda bi, t: (bi, 0, 0, 0, t)),
                   aspec, vspec, gspec, vspec, cspec, cspec],
        out_shape=[jax.ShapeDtypeStruct((b, NSA_GROUPS, NSA_HPG, LANES, s), BF16),
                   ashape, vshape, gshape, vshape,
                   jax.ShapeDtypeStruct((b, s, LANES), BF16),
                   jax.ShapeDtypeStruct((b, s, LANES), BF16)],
        compiler_params=_params(("parallel", "parallel")),
        name="nsa_prep",
    )(proj, proj, proj, proj, proj, proj, proj, cos_f, sin_s,
      tile2(q_norm), tile2(k_norm_s), tile2(k_norm_w))


def _compress_body(ck_ref, cv_ref, pos_ref, w1a_ref, w1b_ref, w2_ref, kw_ref, cos_ref, sin_ref,
                   ko_ref, vo_ref):
    n = ck_ref.shape[1]
    outs = []
    for which, x_ref in enumerate((ck_ref, cv_ref)):
        x = x_ref[0].astype(F32)
        lo = _dot((x + pos_ref[which, 0:1, :]).astype(BF16), w1a_ref[which])
        hi = _dot((x + pos_ref[which, 1:2, :]).astype(BF16), w1b_ref[which])
        h1 = _silu(lo + pltpu.roll(hi, n - 1, 0))
        outs.append(_dot(h1.astype(BF16), w2_ref[which]))
    kc = _norm_rope(outs[0], kw_ref[...], cos_ref[0], sin_ref[0], _seg_ones()).astype(BF16)
    vc_t = _transpose16(outs[1].astype(BF16), _eye16(n))
    for g in range(NSA_GROUPS):
        ko_ref[0, g] = kc[:, g * NSA_DH:(g + 1) * NSA_DH]
        vo_ref[0, g] = vc_t[g * NSA_DH:(g + 1) * NSA_DH]


def _compress(ck, cv, cmp_pos, w_cmp1, w_cmp2, k_norm_c, cos_c, sin_c):
    b, s, _ = ck.shape
    n = s // CMP_STRIDE
    width = CMP_STRIDE * LANES
    per_row = CMP_LEN // CMP_STRIDE
    eye_g = jnp.eye(NSA_GROUPS, dtype=F32)
    w1 = w_cmp1.reshape(2, per_row, CMP_STRIDE, NSA_DH, NSA_DH)
    w1 = jnp.einsum('khldo,gG->khlgdGo', w1, eye_g).reshape(2, per_row, width, LANES).astype(BF16)
    w2 = jnp.einsum('kdo,gG->kgdGo', w_cmp2, eye_g).reshape(2, LANES, LANES).astype(BF16)
    pos = jnp.broadcast_to(cmp_pos.reshape(2, per_row, CMP_STRIDE, 1, NSA_DH),
                           (2, per_row, CMP_STRIDE, NSA_GROUPS, NSA_DH)).reshape(2, per_row, width)
    kw = jnp.tile(k_norm_c.reshape(1, NSA_DH), (1, NSA_GROUPS))
    full = lambda shp: pl.BlockSpec(shp, lambda bi: (0,) * len(shp))
    bspec = pl.BlockSpec((1, n, width), lambda bi: (bi, 0, 0))
    tspec = pl.BlockSpec((1, n, LANES), lambda bi: (bi, 0, 0))
    ospec = pl.BlockSpec((1, NSA_GROUPS, n, NSA_DH), lambda bi: (bi, 0, 0, 0))
    oshape = jax.ShapeDtypeStruct((b, NSA_GROUPS, n, NSA_DH), BF16)
    return pl.pallas_call(
        _compress_body,
        grid=(b,),
        in_specs=[bspec, bspec, full((2, per_row, width)), full((2, width, LANES)),
                  full((2, width, LANES)), full((2, LANES, LANES)), full((1, LANES)), tspec, tspec],
        out_specs=[ospec, pl.BlockSpec((1, NSA_GROUPS, NSA_DH, n), lambda bi: (bi, 0, 0, 0))],
        out_shape=[oshape, jax.ShapeDtypeStruct((b, NSA_GROUPS, NSA_DH, n), BF16)],
        compiler_params=_params(("parallel",)),
        name="nsa_compress",
    )(ck.reshape(b, n, width), cv.reshape(b, n, width), pos, w1[:, 0], w1[:, 1], w2, kw, cos_c, sin_c)


def _nsa_body(qt_ref, kc_ref, vct_ref, ksa_ref, vst_ref, kw_ref, vwt_ref, gt_ref, o_ref,
              qa_ref, sa_ref, sb_ref, os_ref, *, tq, tk, n_sel):
    hp = NSA_HPG
    hs = range(hp)
    t0 = pl.program_id(2) * tq
    q_t = [qt_ref[0, 0, j, 0:NSA_DH, :] for j in hs]
    t_row = t0 + lax.broadcasted_iota(jnp.int32, (1, tq), 1)

    kc = kc_ref[0, 0]
    vc_t = vct_ref[0, 0]
    n_cmp = kc.shape[0]
    cmp_end = lax.broadcasted_iota(jnp.int32, (n_cmp, 1), 0) * CMP_STRIDE + (CMP_LEN - 1)
    bias_c = jnp.where(cmp_end <= t_row, 0.0, NEG)
    valid_c = jnp.where(t_row >= CMP_LEN - 1, 1.0, 0.0)
    cs = lax.broadcasted_iota(jnp.int32, (LANES, n_cmp), 1) * CMP_STRIDE
    bs = lax.broadcasted_iota(jnp.int32, (LANES, n_cmp), 0) * SEL_LEN
    overlap_t = jnp.where((cs < bs + SEL_LEN) & (cs + CMP_LEN > bs), 1.0, 0.0).astype(BF16)
    o_c = []
    imp = None
    s_c = [_dot(kc, q_t[j]) for j in hs]
    wl = WINDOW + tq
    w0 = pl.multiple_of(jnp.maximum(t0 - WINDOW, 0), tq)
    k_w = kw_ref[0, 0, pl.ds(w0, wl), :]
    for j in hs:
        sb_ref[j, 0:wl, :] = _dot(k_w, q_t[j])
    for j in hs:
        sc = s_c[j] + bias_c
        e_c = jnp.exp2(sc - jnp.max(sc, 0, keepdims=True))
        p16 = (e_c * (valid_c / jnp.sum(e_c, 0, keepdims=True))).astype(BF16)
        o_c.append(_dot(vc_t, p16))
        part = _dot(overlap_t, p16)
        imp = part if imp is None else imp + part

    jb = lax.broadcasted_iota(jnp.int32, (LANES, 1), 0)
    cur = t_row >> SEL_SHIFT
    forced = (jb == 0) | (jb == cur) | (jb == cur - 1)
    imp = jnp.where(forced, FORCE, jnp.where(jb * SEL_LEN <= t_row, imp, -FORCE))
    imp = jnp.where(jb < n_sel, imp, -jnp.inf)
    jbf = jb.astype(F32)
    for _ in range(min(SEL_TOP, n_sel)):
        mx = jnp.max(imp, 0, keepdims=True)
        first = jnp.min(jnp.where(imp == mx, jbf, float(LANES)), 0, keepdims=True)
        imp = jnp.where(jbf == first, -jnp.inf, imp)
    selm1_t = jnp.where((imp == -jnp.inf) & (jb < n_sel), 0.0, -1.0).astype(BF16)
    for j in hs:
        qa_ref[j, 0:LANES, :] = selm1_t
        qa_ref[j, LANES:2 * LANES, :] = qt_ref[0, 0, j]

    def scores(kt, s_ref):
        k_aug = ksa_ref[0, 0, pl.ds(pl.multiple_of(kt * tk, tk), tk), :]
        for j in hs:
            s_ref[j] = _dot(k_aug, qa_ref[j])

    def softmax_pv(kt, s_ref, state, diagonal):
        k0 = pl.multiple_of(kt * tk, tk)
        m, acc = list(state[0]), list(state[1])
        v_t = vst_ref[0, 0, :, pl.ds(k0, tk)]
        if diagonal:
            tok = k0 + lax.broadcasted_iota(jnp.int32, (tk, 1), 0)
            bias = jnp.where(tok <= t_row, 0.0, NEG)
        for j in hs:
            sj = s_ref[j] + bias if diagonal else s_ref[j]
            m_new = jnp.maximum(m[j], jnp.max(sj, 0, keepdims=True))
            alpha = jnp.exp2(m[j] - m_new)
            e16 = jnp.exp2((sj - m_new).astype(BF16))
            acc[j] = alpha * acc[j] + _dot(v_t, e16)
            m[j] = m_new
        return tuple(m), tuple(acc)

    def tile_pair(p, state):
        scores(2 * p + 1, sb_ref)
        state = softmax_pv(2 * p, sa_ref, state, False)
        scores(2 * p + 2, sa_ref)
        return softmax_pv(2 * p + 1, sb_ref, state, False)

    def finish(state):
        acc = state[1]
        for j in hs:
            os_ref[j] = acc[j][0:NSA_DH] / jnp.maximum(acc[j][NSA_DH:NSA_DH + 1], 1e-30)

    k_diag = t0 // tk
    state0 = (tuple(jnp.full((1, tq), NEG, F32) for _ in hs),
              tuple(jnp.zeros((V_ROWS, tq), F32) for _ in hs))
    scores(0, sa_ref)

    dist = t_row - (w0 + lax.broadcasted_iota(jnp.int32, (wl, 1), 0))
    bias_w = jnp.where((dist >= 0) & (dist < WINDOW), 0.0, NEG)
    vw_t = vwt_ref[0, 0, :, pl.ds(w0, wl)]
    o_w = []
    for j in hs:
        sw = sb_ref[j, 0:wl, :] + bias_w
        e16 = jnp.exp2((sw - jnp.max(sw, 0, keepdims=True)).astype(BF16))
        oa = _dot(vw_t, e16)
        o_w.append(oa[0:NSA_DH] / oa[NSA_DH:NSA_DH + 1])

    state = lax.fori_loop(0, k_diag // 2, tile_pair, state0)

    @pl.when(k_diag % 2 == 0)
    def _():
        finish(softmax_pv(k_diag, sa_ref, state, True))

    @pl.when(k_diag % 2 == 1)
    def _():
        scores(k_diag, sb_ref)
        finish(softmax_pv(k_diag, sb_ref, softmax_pv(k_diag - 1, sa_ref, state, False), True))

    o_s = [os_ref[j] for j in hs]

    gates = _sigmoid(gt_ref[0, 0])
    for j in range(hp):
        o = (gates[3 * j:3 * j + 1] * o_c[j] + gates[3 * j + 1:3 * j + 2] * o_s[j]
             + gates[3 * j + 2:3 * j + 3] * o_w[j])
        o_ref[0, j * NSA_DH:(j + 1) * NSA_DH, :] = o.astype(o_ref.dtype)


def _nsa_attention(q_t, kc, vc_t, ksa, vs_t, kw, vw_t, gates_t, tq, tk):
    b, g, hp, _, s = q_t.shape
    dh = NSA_DH
    n_cmp = kc.shape[2]
    assert s >= WINDOW + tq and WINDOW % tq == 0 and s % tk == 0 and tk % tq == 0 and tq % LANES == 0
    vt_spec = pl.BlockSpec((1, 1, V_ROWS, s), lambda bi, gi, i: (bi, gi, 0, 0))
    return pl.pallas_call(
        functools.partial(_nsa_body, tq=tq, tk=tk, n_sel=s // SEL_LEN),
        grid=(b, g, s // tq),
        in_specs=[pl.BlockSpec((1, 1, hp, LANES, tq), lambda bi, gi, i: (bi, gi, 0, 0, i)),
                  pl.BlockSpec((1, 1, n_cmp, dh), lambda bi, gi, i: (bi, gi, 0, 0)),
                  pl.BlockSpec((1, 1, dh, n_cmp), lambda bi, gi, i: (bi, gi, 0, 0)),
                  pl.BlockSpec((1, 1, s, 2 * LANES), lambda bi, gi, i: (bi, gi, 0, 0)),
                  vt_spec,
                  pl.BlockSpec((1, 1, s, dh), lambda bi, gi, i: (bi, gi, 0, 0)),
                  vt_spec,
                  pl.BlockSpec((1, 1, 3 * hp, tq), lambda bi, gi, i: (bi, gi, 0, i))],
        out_specs=pl.BlockSpec((1, hp * dh, tq), lambda bi, gi, i: (bi, gi, i)),
        out_shape=jax.ShapeDtypeStruct((b, g * hp * dh, s), BF16),
        scratch_shapes=[pltpu.VMEM((hp, 2 * LANES, tq), BF16), pltpu.VMEM((hp, tk, tq), F32),
                        pltpu.VMEM((hp, tk, tq), F32), pltpu.VMEM((hp, dh, tq), F32)],
        compiler_params=_params(("parallel", "parallel", "arbitrary")),
        name="nsa_attention",
    )(q_t, kc, vc_t, ksa, vs_t, kw, vw_t, gates_t)


def _mix_out_body(x_ref, oa_ref, ob_ref, mg_ref, gm_ref, woa_ref, wob_ref, wout_ref, o_ref):
    d = x_ref.shape[2]
    y_a = _dot(oa_ref[0], woa_ref[...])
    y_b = _dot_tn(ob_ref[0], wob_ref[...])
    merged = (_sigmoid(mg_ref[0, :, 0:d].astype(F32)) * y_a
              + _sigmoid(mg_ref[0, :, d:2 * d].astype(F32)) * y_b)
    o_ref[0] = x_ref[0] + gm_ref[0] * _dot(merged.astype(BF16), wout_ref[...])


def _mix_out(x, o_a, o_b, proj, mg_blk, g_m, w_oa, w_ob, w_out, tm):
    b, s, d = x.shape
    full = lambda a: pl.BlockSpec(a.shape, lambda bi, i: (0, 0))
    return pl.pallas_call(
        _mix_out_body,
        grid=(b, s // tm),
        in_specs=[pl.BlockSpec((1, tm, d), lambda bi, i: (bi, i, 0)),
                  pl.BlockSpec((1, tm, o_a.shape[2]), lambda bi, i: (bi, i, 0)),
                  pl.BlockSpec((1, o_b.shape[1], tm), lambda bi, i: (bi, 0, i)),
                  pl.BlockSpec((1, tm, 2 * d), lambda bi, i: (bi, i, mg_blk)),
                  pl.BlockSpec((1, 1, d), lambda bi, i: (bi, 0, 0)),
                  full(w_oa), full(w_ob), full(w_out)],
        out_specs=pl.BlockSpec((1, tm, d), lambda bi, i: (bi, i, 0)),
        out_shape=jax.ShapeDtypeStruct(x.shape, F32),
        compiler_params=_params(("parallel", "parallel")),
        name="mix_out",
    )(x, o_a, o_b, proj, g_m, w_oa, w_ob, w_out)


def _ffn_body(x_ref, nw_ref, sc_ref, sh_ref, gf_ref, w1_ref, w3_ref, w2_ref, o_ref, h_ref, acc_ref):
    f = pl.program_id(2)

    @pl.when(f == 0)
    def _():
        h_ref[...] = _norm_mod(x_ref[0], nw_ref[...], sc_ref[0], sh_ref[0]).astype(BF16)
        acc_ref[...] = jnp.zeros_like(acc_ref)

    h = h_ref[...]
    t = _silu(_dot(h, w1_ref[...])) * _dot(h, w3_ref[...])
    acc_ref[...] += _dot(t.astype(BF16), w2_ref[...])

    @pl.when(f == pl.num_programs(2) - 1)
    def _():
        o_ref[0] = x_ref[0] + gf_ref[0] * acc_ref[...]


def _dense_ffn(x, nw, sc, sh, g_f, w1, w3, w2, tm, tf):
    b, s, d = x.shape
    ff = w1.shape[1]
    vec = pl.BlockSpec((1, 1, d), lambda bi, i, f: (bi, 0, 0))
    return pl.pallas_call(
        _ffn_body,
        grid=(b, s // tm, ff // tf),
        in_specs=[pl.BlockSpec((1, tm, d), lambda bi, i, f: (bi, i, 0)),
                  pl.BlockSpec((1, d), lambda bi, i, f: (0, 0)), vec, vec, vec,
                  pl.BlockSpec((d, tf), lambda bi, i, f: (0, f)),
                  pl.BlockSpec((d, tf), lambda bi, i, f: (0, f)),
                  pl.BlockSpec((tf, d), lambda bi, i, f: (f, 0))],
        out_specs=pl.BlockSpec((1, tm, d), lambda bi, i, f: (bi, i, 0)),
        out_shape=jax.ShapeDtypeStruct(x.shape, F32),
        scratch_shapes=[pltpu.VMEM((tm, d), BF16), pltpu.VMEM((tm, d), F32)],
        compiler_params=_params(("parallel", "parallel", "arbitrary")),
        name="dense_ffn",
    )(x, nw, sc, sh, g_f, w1, w3, w2)


MOE_ROWS = 64
MOE_GROUP = 4
MOE_SCATTER = 512


def _moe_route_body(x_ref, nw_ref, sc_ref, sh_ref, wr_ref, h_ref, route_ref, meta_ref):
    tm = x_ref.shape[1]
    h16 = _norm_mod(x_ref[0], nw_ref[...], sc_ref[0], sh_ref[0]).astype(BF16)
    h_ref[0] = h16
    logits = _dot_nt(wr_ref[...], h16)
    ef = lax.broadcasted_iota(jnp.int32, (N_EXPERTS, tm), 0).astype(F32)
    m1 = jnp.max(logits, 0, keepdims=True)
    i1 = jnp.min(jnp.where(logits == m1, ef, float(N_EXPERTS)), 0, keepdims=True)
    rest = jnp.where(ef == i1, -jnp.inf, logits)
    m2 = jnp.max(rest, 0, keepdims=True)
    i2 = jnp.min(jnp.where(rest == m2, ef, float(N_EXPERTS)), 0, keepdims=True)
    e2 = jnp.exp(m2 - m1)
    oh1 = jnp.where(ef == i1, 1.0, 0.0)
    oh2 = jnp.where(ef == i2, 1.0, 0.0)
    member = oh1 + oh2
    lane = lax.broadcasted_iota(jnp.int32, (N_EXPERTS, tm), 1)
    csum = member
    sft = 1
    while sft < tm:
        csum = csum + jnp.where(lane >= sft, pltpu.roll(csum, sft, 1), 0.0)
        sft *= 2
    count = jnp.max(csum, 1, keepdims=True)
    nblk = jnp.floor((count + (MOE_ROWS - 1)) * (1.0 / MOE_ROWS))
    nblk_b = jnp.broadcast_to(nblk, (N_EXPERTS, LANES))
    row = lax.broadcasted_iota(jnp.int32, (N_EXPERTS, LANES), 0)
    bsum = nblk_b
    for sft in (1, 2, 4):
        bsum = bsum + jnp.where(row >= sft, pltpu.roll(bsum, sft, 0), 0.0)
    bstart = bsum - nblk_b
    slot = bstart[:, 0:1] * MOE_ROWS + (csum - member)
    rrow = lax.broadcasted_iota(jnp.int32, (8, tm), 0)
    route_ref[0] = jnp.where(
        rrow == 0, jnp.sum(oh1 * slot, 0, keepdims=True),
        jnp.where(rrow == 1, jnp.sum(oh2 * slot, 0, keepdims=True),
                  jnp.where(rrow == 2, 1.0 / (1.0 + e2), jnp.where(rrow == 3, e2 / (1.0 + e2), 0.0))))
    col = lax.broadcasted_iota(jnp.int32, (N_EXPERTS, LANES), 1)
    meta_ref[0] = jnp.where(col == 0, nblk_b, jnp.where(col == 1, bstart, 0.0)).astype(jnp.int32)


def _moe_group_body(nblk_ref, bstart_ref, x_ref, h_ref, route_ref, gf_ref, w1_ref, w3_ref, w2_ref, o_ref,
                    hb_ref, cw_ref, acc_ref, *, n_rows):
    i = pl.program_id(0)
    e = pl.program_id(1)
    f = pl.program_id(2)
    tm = x_ref.shape[1]
    nb = nblk_ref[i * N_EXPERTS + e]
    b0 = bstart_ref[i * N_EXPERTS + e]
    slot1 = route_ref[0, 0:1, :]
    slot2 = route_ref[0, 1:2, :]

    def hits(r0, rows):
        rr = (r0 + lax.broadcasted_iota(jnp.int32, (rows, 1), 0)).astype(F32)
        return rr == slot1, rr == slot2

    def expert_rows(blk, n_blk, first):
        r0 = pl.multiple_of((b0 + blk) * MOE_ROWS, MOE_ROWS)
        rows = pl.ds(r0, n_blk * MOE_ROWS)
        if first:
            hit1, hit2 = hits(r0, n_blk * MOE_ROWS)
            gather = jnp.where(hit1, 1.0, jnp.where(hit2, 1.0, 0.0)).astype(BF16)
            hb_ref[rows, :] = _dot(gather, h_ref[0]).astype(BF16)
            cw_ref[rows, :] = jnp.sum(jnp.where(hit1, route_ref[0, 2:3, :], 0.0)
                                      + jnp.where(hit2, route_ref[0, 3:4, :], 0.0), -1, keepdims=True)
        hb = hb_ref[rows, :]
        t = _silu(_dot(hb, w1_ref[0, 0])) * _dot(hb, w3_ref[0, 0]) * cw_ref[rows, :]
        y = _dot(t.astype(BF16), w2_ref[0])
        if first:
            acc_ref[rows, :] = y
        else:
            acc_ref[rows, :] += y

    def expert_all(first):
        lax.fori_loop(0, nb // MOE_GROUP, lambda k, c: (expert_rows(MOE_GROUP * k, MOE_GROUP, first), c)[1], 0)
        for rem in range(1, MOE_GROUP):
            @pl.when(nb % MOE_GROUP == rem)
            def _(rem=rem):
                expert_rows(nb - rem, rem, first)

    @pl.when(f == 0)
    def _():
        expert_all(True)

    @pl.when(f != 0)
    def _():
        expert_all(False)

    @pl.when((e == pl.num_programs(1) - 1) & (f == pl.num_programs(2) - 1))
    def _():
        def clear(k, c):
            acc_ref[pl.ds(pl.multiple_of(k * MOE_ROWS, MOE_ROWS), MOE_ROWS), :] = jnp.zeros(
                (MOE_ROWS, acc_ref.shape[1]), F32)
            return c

        lax.fori_loop(b0 + nb, n_rows // MOE_ROWS, clear, 0)
        chunk = MOE_SCATTER
        for kc in range(n_rows // chunk):
            hit1, hit2 = hits(kc * chunk, chunk)
            scatter = jnp.where(hit1, 1.0, jnp.where(hit2, 1.0, 0.0)).astype(BF16)
            y = _dot_tn(scatter, acc_ref[kc * chunk:(kc + 1) * chunk, :].astype(BF16))
            if kc == 0:
                o_ref[0] = y
            else:
                o_ref[0] += y
        o_ref[0] = x_ref[0] + gf_ref[0] * o_ref[0]


def _moe_ffn(x, nw, sc, sh, g_f, w_router, w1, w3, w2, tm, tf):
    b, s, d = x.shape
    n_e, _, ff = w1.shape
    assert n_e == N_EXPERTS
    tiles_b = s // tm
    nt = b * tiles_b
    vec = pl.BlockSpec((1, 1, d), lambda bi, i: (bi, 0, 0))
    h16, route, meta = pl.pallas_call(
        _moe_route_body,
        grid=(b, tiles_b),
        in_specs=[pl.BlockSpec((1, tm, d), lambda bi, i: (bi, i, 0)),
                  pl.BlockSpec((1, d), lambda bi, i: (0, 0)), vec, vec,
                  pl.BlockSpec((n_e, d), lambda bi, i: (0, 0))],
        out_specs=[pl.BlockSpec((1, tm, d), lambda bi, i: (bi, i, 0)),
                   pl.BlockSpec((1, 8, tm), lambda bi, i: (bi * tiles_b + i, 0, 0)),
                   pl.BlockSpec((1, n_e, LANES), lambda bi, i: (bi * tiles_b + i, 0, 0))],
        out_shape=[jax.ShapeDtypeStruct((b, s, d), BF16),
                   jax.ShapeDtypeStruct((nt, 8, tm), F32),
                   jax.ShapeDtypeStruct((nt, n_e, LANES), jnp.int32)],
        compiler_params=_params(("parallel", "parallel")),
        name="moe_route",
    )(x, nw, sc, sh, w_router.T.astype(BF16))
    chunked = lambda w: w.reshape(n_e, d, ff // tf, tf).transpose(0, 2, 1, 3).astype(BF16)
    n_rows = -(-(2 * tm + n_e * (MOE_ROWS - 1)) // MOE_SCATTER) * MOE_SCATTER
    grid_spec = pltpu.PrefetchScalarGridSpec(
        num_scalar_prefetch=2,
        grid=(nt, n_e, ff // tf),
        in_specs=[pl.BlockSpec((1, tm, d), lambda i, e, f, nb, bs: (i, 0, 0)),
                  pl.BlockSpec((1, tm, d), lambda i, e, f, nb, bs: (i, 0, 0)),
                  pl.BlockSpec((1, 8, tm), lambda i, e, f, nb, bs: (i, 0, 0)),
                  pl.BlockSpec((1, 1, d), lambda i, e, f, nb, bs: (i // tiles_b, 0, 0)),
                  pl.BlockSpec((1, 1, d, tf), lambda i, e, f, nb, bs: (e, f, 0, 0)),
                  pl.BlockSpec((1, 1, d, tf), lambda i, e, f, nb, bs: (e, f, 0, 0)),
                  pl.BlockSpec((1, tf, d), lambda i, e, f, nb, bs: (e, f, 0))],
        out_specs=pl.BlockSpec((1, tm, d), lambda i, e, f, nb, bs: (i, 0, 0)),
        scratch_shapes=[pltpu.VMEM((n_rows, d), BF16), pltpu.VMEM((n_rows, 1), F32),
                        pltpu.VMEM((n_rows, d), F32)])
    out = pl.pallas_call(
        functools.partial(_moe_group_body, n_rows=n_rows),
        grid_spec=grid_spec,
        out_shape=jax.ShapeDtypeStruct((nt, tm, d), F32),
        compiler_params=_params(("parallel", "arbitrary", "arbitrary")),
        name="moe_group",
    )(meta[:, :, 0].reshape(-1), meta[:, :, 1].reshape(-1),
      x.reshape(nt, tm, d), h16.reshape(nt, tm, d), route, g_f, chunked(w1), chunked(w3), w2)
    return out.reshape(b, s, d)


def _rope_tables(pos):
    inv = 1.0 / (ROPE_THETA ** (jnp.arange(0, NSA_DH, 2, dtype=F32) / NSA_DH))
    ang = pos.astype(F32)[..., None] * inv
    cos, sin = jnp.cos(ang), jnp.sin(ang)
    reps = LANES // NSA_DH
    return (jnp.tile(jnp.concatenate([cos, cos], -1), (1, 1, reps)),
            jnp.tile(jnp.concatenate([-sin, sin], -1), (1, 1, reps)))


_SPLITS = (DN_QKV, DN_HEADS * DN_DV, DN_HEADS, DN_HEADS, NSA_HEADS * NSA_DH) + (NSA_GROUPS * NSA_DH,) * 6
_OFF = np.concatenate([[0], np.cumsum(_SPLITS)])
_OFF_NG = int(_OFF[-1])
_OFF_MG = _OFF_NG + 3 * NSA_HEADS


def kernel(x, c, positions, w_ada, b_ada, norm_mix, norm_ffn, w_in, conv_w, a_log, dt_bias, dn_norm, cmp_pos, w_cmp1, w_cmp2, q_norm, k_norm, w_oa, w_ob, w_out, w1_dense, w3_dense, w2_dense, w_router, w1_moe, w3_moe, w2_moe):
    b, s, d = x.shape
    depth = w_in.shape[0]
    wdn = DN_QKV + DN_HEADS * DN_DV
    n_small = 2 * DN_HEADS + 3 * NSA_HEADS

    cos_f, sin_s = _rope_tables(positions)
    n_cmp_pad = s // CMP_STRIDE
    cmp_end = jnp.minimum(jnp.arange(n_cmp_pad) * CMP_STRIDE + CMP_LEN - 1, s - 1)
    cos_c, sin_c = _rope_tables(positions[:, cmp_end])

    mod = _ada_mod(c, w_ada, b_ada)

    off_nq = int(_OFF[4])
    w_main = jnp.concatenate([w_in[:, :, 0:wdn], w_in[:, :, _OFF_MG:_OFF_MG + 2 * d],
                              w_in[:, :, off_nq:_OFF_NG]], -1).astype(BF16)
    w_small = jnp.concatenate([w_in[:, :, wdn:wdn + 2 * DN_HEADS], w_in[:, :, _OFF_NG:_OFF_MG],
                               jnp.zeros((depth, d, LANES - n_small), F32)], -1).astype(BF16)
    nsa_col0 = wdn + 2 * d
    n_main = w_main.shape[2]

    w_oa16, w_ob16, w_out16 = w_oa.astype(BF16), w_ob.astype(BF16), w_out.astype(BF16)
    w1d, w3d, w2d = w1_dense.astype(BF16), w3_dense.astype(BF16), w2_dense.astype(BF16)
    w1m, w3m, w2m = w1_moe, w3_moe, w2_moe.astype(BF16)

    for l in range(depth):
        sh_m, sc_m, g_m, sh_f, sc_f, g_f = [m.reshape(b, 1, d) for m in jnp.split(mod[l], 6, -1)]
        nw_m = norm_mix[l].reshape(1, d)
        proj, small = _norm_mod_matmul(x, nw_m, sc_m, sh_m, w_main[l], w_small[l], tm=1024, tn=n_main // 3)
        a_t = jnp.swapaxes(small[:, :, DN_HEADS:2 * DN_HEADS], 1, 2)
        o_a = _deltanet(proj, small, a_t, conv_w[l], a_log[l], dt_bias[l], dn_norm[l], ts=512)
        qn, ksn, vs, kwn, vw, ck, cv = _nsa_prep(proj, nsa_col0, cos_f, sin_s, q_norm[l],
                                                 k_norm[l, 1], k_norm[l, 2], ts=512)
        kc, vc = _compress(ck, cv, cmp_pos[l], w_cmp1[l], w_cmp2[l], k_norm[l, 0], cos_c, sin_c)
        gates_t = jnp.swapaxes(small[:, :, 2 * DN_HEADS:n_small], 1, 2).reshape(b, NSA_GROUPS, 3 * NSA_HPG, s)
        o_b = _nsa_attention(qn, kc, vc, ksn, vs, kwn, vw, gates_t, tq=512, tk=1024)
        x = _mix_out(x, o_a, o_b, proj, wdn // (2 * d), g_m, w_oa16[l], w_ob16[l], w_out16[l], tm=512)
        nw_f = norm_ffn[l].reshape(1, d)
        if l % 2 == 0:
            x = _dense_ffn(x, nw_f, sc_f, sh_f, g_f, w1d[l // 2], w3d[l // 2], w2d[l // 2], tm=1024, tf=512)
        else:
            x = _moe_ffn(x, nw_f, sc_f, sh_f, g_f, w_router[l // 2], w1m[l // 2], w3m[l // 2],
                         w2m[l // 2], tm=1024, tf=896)
    return x
```

```python
import functools

import jax
import jax.numpy as jnp
import numpy as np
from jax import lax
from jax.experimental import pallas as pl
from jax.experimental.pallas import tpu as pltpu

F32 = jnp.float32
BF16 = jnp.bfloat16

DN_HEADS = 8
DN_DK = 64
DN_DV = 64
DN_CHUNK = 64
CONV_W = 4
DN_QKV = DN_HEADS * (2 * DN_DK + DN_DV)
NSA_HEADS = 8
NSA_GROUPS = 2
NSA_HPG = NSA_HEADS // NSA_GROUPS
NSA_DH = 64
CMP_LEN = 32
CMP_STRIDE = 16
SEL_LEN = 64
SEL_SHIFT = 6
SEL_TOP = 16
WINDOW = 512
ROPE_THETA = 10000.0
N_EXPERTS = 8
EPS = 1e-6
NEG = -1e30
FORCE = 1e6
SEL_BIAS = 1e30

LANES = 128
VMEM_LIMIT = 56 * 1024 * 1024


def _sigmoid(x):
    return 1.0 / (1.0 + jnp.exp(-x))


def _silu(x):
    return x * _sigmoid(x)


def _softplus(x):
    return jnp.maximum(x, 0.0) + jnp.log(1.0 + jnp.exp(-jnp.abs(x)))


def _dot(a, b):
    return jnp.dot(a, b, preferred_element_type=F32)


def _dot_nt(a, b):
    return lax.dot_general(a, b, (((1,), (1,)), ((), ())), preferred_element_type=F32)


def _dot_tn(a, b):
    return lax.dot_general(a, b, (((0,), (0,)), ((), ())), preferred_element_type=F32)


def _norm_mod(x, nw, sc, sh):
    y = x * lax.rsqrt(jnp.mean(x * x, -1, keepdims=True) + EPS) * nw
    return y * (1.0 + sc) + sh


def _params(sem):
    return pltpu.CompilerParams(dimension_semantics=sem, vmem_limit_bytes=VMEM_LIMIT)


def _mod_body(c_ref, w_ref, b_ref, o_ref):
    c = c_ref[...]
    o_ref[0] = _dot(_silu(c).astype(BF16), w_ref[0].astype(BF16)) + b_ref[0]


def _ada_mod(c, w_ada, b_ada):
    n_layers, d, n = w_ada.shape
    b = c.shape[0]
    tn = n // 4
    return pl.pallas_call(
        _mod_body,
        grid=(n_layers, n // tn),
        in_specs=[pl.BlockSpec((b, d), lambda l, j: (0, 0)),
                  pl.BlockSpec((1, d, tn), lambda l, j: (l, 0, j)),
                  pl.BlockSpec((1, 1, tn), lambda l, j: (l, 0, j))],
        out_specs=pl.BlockSpec((1, b, tn), lambda l, j: (l, 0, j)),
        out_shape=jax.ShapeDtypeStruct((n_layers, b, n), F32),
        compiler_params=_params(("parallel", "parallel")),
        name="ada_mod",
    )(c, w_ada, b_ada.reshape(n_layers, 1, n))


def _nm_mm_body(x_ref, nw_ref, sc_ref, sh_ref, w_ref, ws_ref, o_ref, os_ref, h_ref):
    @pl.when(pl.program_id(2) == 0)
    def _():
        h = _norm_mod(x_ref[0], nw_ref[...], sc_ref[0], sh_ref[0]).astype(BF16)
        h_ref[...] = h
        os_ref[0] = _dot(h, ws_ref[...])

    o_ref[0] = _dot(h_ref[...], w_ref[...]).astype(o_ref.dtype)


def _norm_mod_matmul(x, nw, sc, sh, w, w_small, tm, tn):
    b, s, d = x.shape
    n = w.shape[1]
    ns = w_small.shape[1]
    return pl.pallas_call(
        _nm_mm_body,
        grid=(b, s // tm, n // tn),
        in_specs=[pl.BlockSpec((1, tm, d), lambda bi, i, j: (bi, i, 0)),
                  pl.BlockSpec((1, d), lambda bi, i, j: (0, 0)),
                  pl.BlockSpec((1, 1, d), lambda bi, i, j: (bi, 0, 0)),
                  pl.BlockSpec((1, 1, d), lambda bi, i, j: (bi, 0, 0)),
                  pl.BlockSpec((d, tn), lambda bi, i, j: (0, j)),
                  pl.BlockSpec((d, ns), lambda bi, i, j: (0, 0))],
        out_specs=[pl.BlockSpec((1, tm, tn), lambda bi, i, j: (bi, i, j)),
                   pl.BlockSpec((1, tm, ns), lambda bi, i, j: (bi, i, 0))],
        out_shape=[jax.ShapeDtypeStruct((b, s, n), BF16), jax.ShapeDtypeStruct((b, s, ns), F32)],
        scratch_shapes=[pltpu.VMEM((tm, d), BF16)],
        compiler_params=_params(("parallel", "parallel", "arbitrary")),
        name="in_proj",
    )(x, nw, sc, sh, w, w_small)


def _dn_body(x_ref, sm_ref, at_ref, cw_ref, alog_ref, dtb_ref, alogt_ref, dtbt_ref, dnw_ref, o_ref,
             buf_ref, act_ref, gcn_ref, beta_ref, gct_ref, state_ref, *, ts):
    nc = ts // DN_CHUNK
    c64 = DN_CHUNK

    @pl.when(pl.program_id(1) == 0)
    def _():
        buf_ref[0:8, :] = jnp.zeros((8, DN_QKV), F32)
        state_ref[...] = jnp.zeros_like(state_ref)

    for sl in range(DN_QKV // LANES):
        cols = slice(sl * LANES, (sl + 1) * LANES)
        buf_ref[8:ts + 8, cols] = x_ref[0, :, cols].astype(F32)
        y = cw_ref[0:1, cols] * buf_ref[5:5 + ts, cols]
        for j in range(1, CONV_W):
            y = y + cw_ref[j:j + 1, cols] * buf_ref[5 + j:5 + j + ts, cols]
        buf_ref[0:8, cols] = buf_ref[ts:ts + 8, cols]
        act_ref[:, :, cols] = _silu(y).reshape(nc, c64, LANES)

    sm = sm_ref[0]
    beta_ref[...] = _sigmoid(sm).reshape(nc, c64, LANES)
    g = -jnp.exp(alog_ref[...]) * _softplus(sm + dtb_ref[...])
    row = lax.broadcasted_iota(jnp.int32, (ts, LANES), 0) & (c64 - 1)
    for sft in (1, 2, 4, 8, 16, 32):
        g = g + jnp.where(row >= sft, pltpu.roll(g, sft, 0), 0.0)
    gcn_ref[...] = g.reshape(nc, c64, LANES)
    gt = -jnp.exp(alogt_ref[...]) * _softplus(at_ref[0] + dtbt_ref[...])
    lane = lax.broadcasted_iota(jnp.int32, (DN_HEADS, ts), 1) & (c64 - 1)
    for sft in (1, 2, 4, 8, 16, 32):
        gt = gt + jnp.where(lane >= sft, pltpu.roll(gt, sft, 1), 0.0)
    for c in range(nc):
        gct_ref[c] = gt[:, c * c64:(c + 1) * c64]

    ri = lax.broadcasted_iota(jnp.int32, (c64, c64), 0)
    ci = lax.broadcasted_iota(jnp.int32, (c64, c64), 1)
    tril = ri >= ci
    strict = ri > ci
    eye = jnp.where(ri == ci, 1.0, 0.0).astype(F32)
    ones = jnp.ones((c64, c64), F32)
    dnw = dnw_ref[...]

    hs = range(DN_HEADS)
    grp = 2 if nc % 2 == 0 else 1

    def chunk_group(cg, carry):
        items = [(j, h) for j in range(grp) for h in hs]
        n = range(len(items))
        cs = [cg * grp + j for j in range(grp)]
        gcn = [gcn_ref[c] for c in cs]
        bet = [beta_ref[c] for c in cs]
        gct = [gct_ref[c] for c in cs]
        q = [act_ref[cs[j], :, h * DN_DK:(h + 1) * DN_DK] for j, h in items]
        k = [act_ref[cs[j], :, (DN_HEADS + h) * DN_DK:(DN_HEADS + h + 1) * DN_DK] for j, h in items]
        v = [act_ref[cs[j], :, 2 * DN_HEADS * DN_DK + h * DN_DV:2 * DN_HEADS * DN_DK + (h + 1) * DN_DV]
             for j, h in items]
        q = [x * lax.rsqrt(_dot(x * x, ones) + EPS) * (DN_DK ** -0.5) for x in q]
        k = [x * lax.rsqrt(_dot(x * x, ones) + EPS) for x in k]
        bcol = [bet[j][:, h:h + 1] for j, h in items]
        gcol = [gcn[j][:, DN_HEADS + h:DN_HEADS + h + 1] for j, h in items]
        grow = [gct[j][h:h + 1, :] for j, h in items]
        decay = [jnp.where(tril, jnp.exp(jnp.where(tril, gcol[i] - grow[i], 0.0)), 0.0) for i in n]
        eg = [jnp.exp(x) for x in gcol]
        glast = [x[c64 - 1:c64, :] for x in gcol]
        kb = [k[i] * bcol[i] for i in n]
        k16 = [x.astype(BF16) for x in k]
        kk = [_dot_nt(kb[i].astype(BF16), k16[i]) for i in n]
        qk = [_dot_nt(q[i].astype(BF16), k16[i]) for i in n]
        a16 = [jnp.where(tril, qk[i] * decay[i], 0.0).astype(BF16) for i in n]
        m = [jnp.where(strict, -(kk[i] * decay[i]), 0.0) for i in n]
        p = [eye + x for x in m]
        for _ in range(5):
            m = [_dot(x, x) for x in m]
            p = [p[i] + _dot(m[i], p[i]) for i in n]
        tinv = [x.astype(BF16) for x in p]
        u = [_dot(tinv[i], (v[i] * bcol[i]).astype(BF16)) for i in n]
        w16 = [_dot(tinv[i], (kb[i] * eg[i]).astype(BF16)).astype(BF16) for i in n]
        qe16 = [(q[i] * eg[i]).astype(BF16) for i in n]
        kd16 = [(k[i] * jnp.exp(glast[i] - gcol[i])).astype(BF16) for i in n]
        egl = [jnp.exp(x) for x in glast]
        for j in range(grp):
            idx = [j * DN_HEADS + h for h in hs]
            r0 = pl.multiple_of(cs[j] * c64, c64)
            st = [state_ref[h] for h in hs]
            st16 = [x.astype(BF16) for x in st]
            ws = [_dot(w16[idx[h]], st16[h]) for h in hs]
            vn16 = [(u[idx[h]] - ws[h]).astype(BF16) for h in hs]
            qs = [_dot(qe16[idx[h]], st16[h]) for h in hs]
            av = [_dot(a16[idx[h]], vn16[h]) for h in hs]
            kv = [_dot_tn(kd16[idx[h]], vn16[h]) for h in hs]
            for h in hs:
                state_ref[h] = st[h] * egl[idx[h]] + kv[h]
                o = qs[h] + av[h]
                on = o * lax.rsqrt(jnp.mean(o * o, -1, keepdims=True) + EPS) * dnw
                z = x_ref[0, pl.ds(r0, c64), DN_QKV + h * DN_DV:DN_QKV + (h + 1) * DN_DV].astype(F32)
                o_ref[0, pl.ds(r0, c64), h * DN_DV:(h + 1) * DN_DV] = (on * _silu(z)).astype(o_ref.dtype)
        return carry

    lax.fori_loop(0, nc // grp, chunk_group, 0)


def _deltanet(proj, small, a_t, conv_w, a_log, dt_bias, dn_norm, ts):
    b, s, _ = proj.shape
    wdn = DN_QKV + DN_HEADS * DN_DV
    pad = jnp.zeros((LANES - 2 * DN_HEADS,), F32)
    alog_row = jnp.concatenate([jnp.zeros((DN_HEADS,), F32), a_log, pad]).reshape(1, LANES)
    dtb_row = jnp.concatenate([jnp.zeros((DN_HEADS,), F32), dt_bias, pad]).reshape(1, LANES)
    nc = ts // DN_CHUNK
    return pl.pallas_call(
        functools.partial(_dn_body, ts=ts),
        grid=(b, s // ts),
        in_specs=[pl.BlockSpec((1, ts, wdn), lambda bi, i: (bi, i, 0)),
                  pl.BlockSpec((1, ts, LANES), lambda bi, i: (bi, i, 0)),
                  pl.BlockSpec((1, DN_HEADS, ts), lambda bi, i: (bi, 0, i)),
                  pl.BlockSpec((CONV_W, DN_QKV), lambda bi, i: (0, 0)),
                  pl.BlockSpec((1, LANES), lambda bi, i: (0, 0)),
                  pl.BlockSpec((1, LANES), lambda bi, i: (0, 0)),
                  pl.BlockSpec((DN_HEADS, 1), lambda bi, i: (0, 0)),
                  pl.BlockSpec((DN_HEADS, 1), lambda bi, i: (0, 0)),
                  pl.BlockSpec((1, DN_DV), lambda bi, i: (0, 0))],
        out_specs=pl.BlockSpec((1, ts, DN_HEADS * DN_DV), lambda bi, i: (bi, i, 0)),
        out_shape=jax.ShapeDtypeStruct((b, s, DN_HEADS * DN_DV), BF16),
        scratch_shapes=[pltpu.VMEM((ts + 8, DN_QKV), F32),
                        pltpu.VMEM((nc, DN_CHUNK, DN_QKV), F32),
                        pltpu.VMEM((nc, DN_CHUNK, LANES), F32),
                        pltpu.VMEM((nc, DN_CHUNK, LANES), F32),
                        pltpu.VMEM((nc, DN_HEADS, DN_CHUNK), F32),
                        pltpu.VMEM((DN_HEADS, DN_DK, DN_DV), F32)],
        compiler_params=_params(("parallel", "arbitrary")),
        name="deltanet",
    )(proj, small, a_t, conv_w, alog_row, dtb_row, a_log.reshape(DN_HEADS, 1),
      dt_bias.reshape(DN_HEADS, 1), dn_norm.reshape(1, DN_DV))


def _seg_ones():
    r = lax.broadcasted_iota(jnp.int32, (LANES, LANES), 0) // NSA_DH
    c = lax.broadcasted_iota(jnp.int32, (LANES, LANES), 1) // NSA_DH
    return jnp.where(r == c, 1.0, 0.0).astype(F32)


V_ROWS = 80
LOG2E = 1.4426950408889634


def _eye16(n):
    r = lax.broadcasted_iota(jnp.int32, (n, n), 0)
    c = lax.broadcasted_iota(jnp.int32, (n, n), 1)
    return jnp.where(r == c, 1.0, 0.0).astype(BF16)


def _transpose16(x16, eye):
    return _dot_tn(x16, eye).astype(BF16)


def _norm_rope(x, w, cos_f, sin_s, seg):
    ms = _dot(x * x, seg) * (1.0 / NSA_DH)
    y = x * lax.rsqrt(ms + EPS) * w
    half = NSA_DH // 2
    lane = lax.broadcasted_iota(jnp.int32, y.shape, 1) & (NSA_DH - 1)
    partner = jnp.where(lane < half, pltpu.roll(y, LANES - half, 1), pltpu.roll(y, half, 1))
    return y * cos_f + partner * sin_s


def _nsa_prep_body(q_ref, ck_ref, cv_ref, sk_ref, sv_ref, wk_ref, wv_ref, cos_ref, sin_ref,
                   qw_ref, skw_ref, wkw_ref,
                   qo_ref, sko_ref, svo_ref, wko_ref, wvo_ref, cko_ref, cvo_ref):
    seg = _seg_ones()
    cos_f = cos_ref[0]
    sin_s = sin_ref[0]
    ts = cos_f.shape[0]
    eye = _eye16(ts)
    for sl in range(NSA_HEADS * NSA_DH // LANES):
        x = q_ref[0, :, sl * LANES:(sl + 1) * LANES].astype(F32)
        y = (_norm_rope(x, qw_ref[...], cos_f, sin_s, seg) * (NSA_DH ** -0.5 * LOG2E)).astype(BF16)
        y_t = _transpose16(y, eye)
        for half in range(2):
            h = 2 * sl + half
            qo_ref[0, h // NSA_HPG, h % NSA_HPG, 0:NSA_DH, :] = y_t[half * NSA_DH:(half + 1) * NSA_DH]
            qo_ref[0, h // NSA_HPG, h % NSA_HPG, NSA_DH:LANES, :] = jnp.zeros((LANES - NSA_DH, ts), BF16)
    sk = _norm_rope(sk_ref[0].astype(F32), skw_ref[...], cos_f, sin_s, seg).astype(BF16)
    wk = _norm_rope(wk_ref[0].astype(F32), wkw_ref[...], cos_f, sin_s, seg).astype(BF16)
    sv_t = _transpose16(sv_ref[0], eye)
    wv_t = _transpose16(wv_ref[0], eye)
    ones_row = jnp.where(lax.broadcasted_iota(jnp.int32, (V_ROWS - NSA_DH, ts), 0) == 0, 1.0, 0.0).astype(BF16)
    blk = (pl.program_id(1) * ts + lax.broadcasted_iota(jnp.int32, (ts, LANES), 0)) >> SEL_SHIFT
    onehot = jnp.where(lax.broadcasted_iota(jnp.int32, (ts, LANES), 1) == blk, SEL_BIAS, 0.0).astype(BF16)
    for g in range(NSA_GROUPS):
        cols = slice(g * NSA_DH, (g + 1) * NSA_DH)
        sko_ref[0, g, :, 0:LANES] = onehot
        sko_ref[0, g, :, LANES:LANES + NSA_DH] = sk[:, cols]
        sko_ref[0, g, :, LANES + NSA_DH:2 * LANES] = jnp.zeros((ts, LANES - NSA_DH), BF16)
        wko_ref[0, g] = wk[:, cols]
        svo_ref[0, g, 0:NSA_DH, :] = sv_t[cols]
        svo_ref[0, g, NSA_DH:V_ROWS, :] = ones_row
        wvo_ref[0, g, 0:NSA_DH, :] = wv_t[cols]
        wvo_ref[0, g, NSA_DH:V_ROWS, :] = ones_row
    cko_ref[0] = ck_ref[0]
    cvo_ref[0] = cv_ref[0]


def _nsa_prep(proj, col0, cos_f, sin_s, q_norm, k_norm_s, k_norm_w, ts):
    b, s, _ = proj.shape
    assert s // SEL_LEN <= LANES
    qw = NSA_HEADS * NSA_DH
    qblk = col0 // qw
    k0 = (col0 + qw) // LANES

    def kspec(i):
        return pl.BlockSpec((1, ts, LANES), lambda bi, t, i=i: (bi, t, k0 + i))

    tile2 = lambda w: jnp.tile(w.reshape(1, NSA_DH), (1, LANES // NSA_DH))
    gshape = jax.ShapeDtypeStruct((b, NSA_GROUPS, s, NSA_DH), BF16)
    gspec = pl.BlockSpec((1, NSA_GROUPS, ts, NSA_DH), lambda bi, t: (bi, 0, t, 0))
    ashape = jax.ShapeDtypeStruct((b, NSA_GROUPS, s, 2 * LANES), BF16)
    aspec = pl.BlockSpec((1, NSA_GROUPS, ts, 2 * LANES), lambda bi, t: (bi, 0, t, 0))
    vshape = jax.ShapeDtypeStruct((b, NSA_GROUPS, V_ROWS, s), BF16)
    vspec = pl.BlockSpec((1, NSA_GROUPS, V_ROWS, ts), lambda bi, t: (bi, 0, 0, t))
    cspec = pl.BlockSpec((1, ts, LANES), lambda bi, t: (bi, t, 0))
    wspec = pl.BlockSpec((1, LANES), lambda bi, t: (0, 0))
    return pl.pallas_call(
        _nsa_prep_body,
        grid=(b, s // ts),
        in_specs=[pl.BlockSpec((1, ts, qw), lambda bi, t: (bi, t, qblk)),
                  kspec(0), kspec(1), kspec(2), kspec(3), kspec(4), kspec(5),
                  cspec, cspec, wspec, wspec, wspec],
        out_specs=[pl.BlockSpec((1, NSA_GROUPS, NSA_HPG, LANES, ts), lambda bi, t: (bi, 0, 0, 0, t)),
                   aspec, vspec, gspec, vspec, cspec, cspec],
        out_shape=[jax.ShapeDtypeStruct((b, NSA_GROUPS, NSA_HPG, LANES, s), BF16),
                   ashape, vshape, gshape, vshape,
                   jax.ShapeDtypeStruct((b, s, LANES), BF16),
                   jax.ShapeDtypeStruct((b, s, LANES), BF16)],
        compiler_params=_params(("parallel", "parallel")),
        name="nsa_prep",
    )(proj, proj, proj, proj, proj, proj, proj, cos_f, sin_s,
      tile2(q_norm), tile2(k_norm_s), tile2(k_norm_w))


def _compress_body(ck_ref, cv_ref, pos_ref, w1a_ref, w1b_ref, w2_ref, kw_ref, cos_ref, sin_ref,
                   ko_ref, vo_ref):
    n = ck_ref.shape[1]
    outs = []
    for which, x_ref in enumerate((ck_ref, cv_ref)):
        x = x_ref[0].astype(F32)
        lo = _dot((x + pos_ref[which, 0:1, :]).astype(BF16), w1a_ref[which])
        hi = _dot((x + pos_ref[which, 1:2, :]).astype(BF16), w1b_ref[which])
        h1 = _silu(lo + pltpu.roll(hi, n - 1, 0))
        outs.append(_dot(h1.astype(BF16), w2_ref[which]))
    kc = _norm_rope(outs[0], kw_ref[...], cos_ref[0], sin_ref[0], _seg_ones()).astype(BF16)
    vc_t = _transpose16(outs[1].astype(BF16), _eye16(n))
    for g in range(NSA_GROUPS):
        ko_ref[0, g] = kc[:, g * NSA_DH:(g + 1) * NSA_DH]
        vo_ref[0, g] = vc_t[g * NSA_DH:(g + 1) * NSA_DH]


def _compress(ck, cv, cmp_pos, w_cmp1, w_cmp2, k_norm_c, cos_c, sin_c):
    b, s, _ = ck.shape
    n = s // CMP_STRIDE
    width = CMP_STRIDE * LANES
    per_row = CMP_LEN // CMP_STRIDE
    eye_g = jnp.eye(NSA_GROUPS, dtype=F32)
    w1 = w_cmp1.reshape(2, per_row, CMP_STRIDE, NSA_DH, NSA_DH)
    w1 = jnp.einsum('khldo,gG->khlgdGo', w1, eye_g).reshape(2, per_row, width, LANES).astype(BF16)
    w2 = jnp.einsum('kdo,gG->kgdGo', w_cmp2, eye_g).reshape(2, LANES, LANES).astype(BF16)
    pos = jnp.broadcast_to(cmp_pos.reshape(2, per_row, CMP_STRIDE, 1, NSA_DH),
                           (2, per_row, CMP_STRIDE, NSA_GROUPS, NSA_DH)).reshape(2, per_row, width)
    kw = jnp.tile(k_norm_c.reshape(1, NSA_DH), (1, NSA_GROUPS))
    full = lambda shp: pl.BlockSpec(shp, lambda bi: (0,) * len(shp))
    bspec = pl.BlockSpec((1, n, width), lambda bi: (bi, 0, 0))
    tspec = pl.BlockSpec((1, n, LANES), lambda bi: (bi, 0, 0))
    ospec = pl.BlockSpec((1, NSA_GROUPS, n, NSA_DH), lambda bi: (bi, 0, 0, 0))
    oshape = jax.ShapeDtypeStruct((b, NSA_GROUPS, n, NSA_DH), BF16)
    return pl.pallas_call(
        _compress_body,
        grid=(b,),
        in_specs=[bspec, bspec, full((2, per_row, width)), full((2, width, LANES)),
                  full((2, width, LANES)), full((2, LANES, LANES)), full((1, LANES)), tspec, tspec],
        out_specs=[ospec, pl.BlockSpec((1, NSA_GROUPS, NSA_DH, n), lambda bi: (bi, 0, 0, 0))],
        out_shape=[oshape, jax.ShapeDtypeStruct((b, NSA_GROUPS, NSA_DH, n), BF16)],
        compiler_params=_params(("parallel",)),
        name="nsa_compress",
    )(ck.reshape(b, n, width), cv.reshape(b, n, width), pos, w1[:, 0], w1[:, 1], w2, kw, cos_c, sin_c)


def _nsa_body(qt_ref, kc_ref, vct_ref, ksa_ref, vst_ref, kw_ref, vwt_ref, gt_ref, o_ref,
              qa_ref, sa_ref, sb_ref, os_ref, *, tq, tk, n_sel):
    hp = NSA_HPG
    hs = range(hp)
    t0 = pl.program_id(2) * tq
    q_t = [qt_ref[0, 0, j, 0:NSA_DH, :] for j in hs]
    t_row = t0 + lax.broadcasted_iota(jnp.int32, (1, tq), 1)

    kc = kc_ref[0, 0]
    vc_t = vct_ref[0, 0]
    n_cmp = kc.shape[0]
    cmp_end = lax.broadcasted_iota(jnp.int32, (n_cmp, 1), 0) * CMP_STRIDE + (CMP_LEN - 1)
    bias_c = jnp.where(cmp_end <= t_row, 0.0, NEG)
    valid_c = jnp.where(t_row >= CMP_LEN - 1, 1.0, 0.0)
    cs = lax.broadcasted_iota(jnp.int32, (LANES, n_cmp), 1) * CMP_STRIDE
    bs = lax.broadcasted_iota(jnp.int32, (LANES, n_cmp), 0) * SEL_LEN
    overlap_t = jnp.where((cs < bs + SEL_LEN) & (cs + CMP_LEN > bs), 1.0, 0.0).astype(BF16)
    o_c = []
    imp = None
    s_c = [_dot(kc, q_t[j]) for j in hs]
    wl = WINDOW + tq
    w0 = pl.multiple_of(jnp.maximum(t0 - WINDOW, 0), tq)
    k_w = kw_ref[0, 0, pl.ds(w0, wl), :]
    for j in hs:
        sb_ref[j, 0:wl, :] = _dot(k_w, q_t[j])
    for j in hs:
        sc = s_c[j] + bias_c
        e_c = jnp.exp2(sc - jnp.max(sc, 0, keepdims=True))
        p16 = (e_c * (valid_c / jnp.sum(e_c, 0, keepdims=True))).astype(BF16)
        o_c.append(_dot(vc_t, p16))
        part = _dot(overlap_t, p16)
        imp = part if imp is None else imp + part

    jb = lax.broadcasted_iota(jnp.int32, (LANES, 1), 0)
    cur = t_row >> SEL_SHIFT
    forced = (jb == 0) | (jb == cur) | (jb == cur - 1)
    imp = jnp.where(forced, FORCE, jnp.where(jb * SEL_LEN <= t_row, imp, -FORCE))
    imp = jnp.where(jb < n_sel, imp, -jnp.inf)
    jbf = jb.astype(F32)
    for _ in range(min(SEL_TOP, n_sel)):
        mx = jnp.max(imp, 0, keepdims=True)
        first = jnp.min(jnp.where(imp == mx, jbf, float(LANES)), 0, keepdims=True)
        imp = jnp.where(jbf == first, -jnp.inf, imp)
    selm1_t = jnp.where((imp == -jnp.inf) & (jb < n_sel), 0.0, -1.0).astype(BF16)
    for j in hs:
        qa_ref[j, 0:LANES, :] = selm1_t
        qa_ref[j, LANES:2 * LANES, :] = qt_ref[0, 0, j]

    def scores(kt, s_ref):
        k_aug = ksa_ref[0, 0, pl.ds(pl.multiple_of(kt * tk, tk), tk), :]
        for j in hs:
            s_ref[j] = _dot(k_aug, qa_ref[j])

    def softmax_pv(kt, s_ref, state, diagonal):
        k0 = pl.multiple_of(kt * tk, tk)
        m, acc = list(state[0]), list(state[1])
        v_t = vst_ref[0, 0, :, pl.ds(k0, tk)]
        if diagonal:
            tok = k0 + lax.broadcasted_iota(jnp.int32, (tk, 1), 0)
            bias = jnp.where(tok <= t_row, 0.0, NEG)
        for j in hs:
            sj = s_ref[j] + bias if diagonal else s_ref[j]
            m_new = jnp.maximum(m[j], jnp.max(sj, 0, keepdims=True))
            alpha = jnp.exp2(m[j] - m_new)
            e16 = jnp.exp2((sj - m_new).astype(BF16))
            acc[j] = alpha * acc[j] + _dot(v_t, e16)
            m[j] = m_new
        return tuple(m), tuple(acc)

    def tile_pair(p, state):
        scores(2 * p + 1, sb_ref)
        state = softmax_pv(2 * p, sa_ref, state, False)
        scores(2 * p + 2, sa_ref)
        return softmax_pv(2 * p + 1, sb_ref, state, False)

    def finish(state):
        acc = state[1]
        for j in hs:
            os_ref[j] = acc[j][0:NSA_DH] / jnp.maximum(acc[j][NSA_DH:NSA_DH + 1], 1e-30)

    k_diag = t0 // tk
    state0 = (tuple(jnp.full((1, tq), NEG, F32) for _ in hs),
              tuple(jnp.zeros((V_ROWS, tq), F32) for _ in hs))
    scores(0, sa_ref)

    dist = t_row - (w0 + lax.broadcasted_iota(jnp.int32, (wl, 1), 0))
    bias_w = jnp.where((dist >= 0) & (dist < WINDOW), 0.0, NEG)
    vw_t = vwt_ref[0, 0, :, pl.ds(w0, wl)]
    o_w = []
    for j in hs:
        sw = sb_ref[j, 0:wl, :] + bias_w
        e16 = jnp.exp2((sw - jnp.max(sw, 0, keepdims=True)).astype(BF16))
        oa = _dot(vw_t, e16)
        o_w.append(oa[0:NSA_DH] / oa[NSA_DH:NSA_DH + 1])

    state = lax.fori_loop(0, k_diag // 2, tile_pair, state0)

    @pl.when(k_diag % 2 == 0)
    def _():
        finish(softmax_pv(k_diag, sa_ref, state, True))

    @pl.when(k_diag % 2 == 1)
    def _():
        scores(k_diag, sb_ref)
        finish(softmax_pv(k_diag, sb_ref, softmax_pv(k_diag - 1, sa_ref, state, False), True))

    o_s = [os_ref[j] for j in hs]

    gates = _sigmoid(gt_ref[0, 0])
    for j in range(hp):
        o = (gates[3 * j:3 * j + 1] * o_c[j] + gates[3 * j + 1:3 * j + 2] * o_s[j]
             + gates[3 * j + 2:3 * j + 3] * o_w[j])
        o_ref[0, j * NSA_DH:(j + 1) * NSA_DH, :] = o.astype(o_ref.dtype)


def _nsa_attention(q_t, kc, vc_t, ksa, vs_t, kw, vw_t, gates_t, tq, tk):
    b, g, hp, _, s = q_t.shape
    dh = NSA_DH
    n_cmp = kc.shape[2]
    assert s >= WINDOW + tq and WINDOW % tq == 0 and s % tk == 0 and tk % tq == 0 and tq % LANES == 0
    vt_spec = pl.BlockSpec((1, 1, V_ROWS, s), lambda bi, gi, i: (bi, gi, 0, 0))
    return pl.pallas_call(
        functools.partial(_nsa_body, tq=tq, tk=tk, n_sel=s // SEL_LEN),
        grid=(b, g, s // tq),
        in_specs=[pl.BlockSpec((1, 1, hp, LANES, tq), lambda bi, gi, i: (bi, gi, 0, 0, i)),
                  pl.BlockSpec((1, 1, n_cmp, dh), lambda bi, gi, i: (bi, gi, 0, 0)),
                  pl.BlockSpec((1, 1, dh, n_cmp), lambda bi, gi, i: (bi, gi, 0, 0)),
                  pl.BlockSpec((1, 1, s, 2 * LANES), lambda bi, gi, i: (bi, gi, 0, 0)),
                  vt_spec,
                  pl.BlockSpec((1, 1, s, dh), lambda bi, gi, i: (bi, gi, 0, 0)),
                  vt_spec,
                  pl.BlockSpec((1, 1, 3 * hp, tq), lambda bi, gi, i: (bi, gi, 0, i))],
        out_specs=pl.BlockSpec((1, hp * dh, tq), lambda bi, gi, i: (bi, gi, i)),
        out_shape=jax.ShapeDtypeStruct((b, g * hp * dh, s), BF16),
        scratch_shapes=[pltpu.VMEM((hp, 2 * LANES, tq), BF16), pltpu.VMEM((hp, tk, tq), F32),
                        pltpu.VMEM((hp, tk, tq), F32), pltpu.VMEM((hp, dh, tq), F32)],
        compiler_params=_params(("parallel", "parallel", "arbitrary")),
        name="nsa_attention",
    )(q_t, kc, vc_t, ksa, vs_t, kw, vw_t, gates_t)


def _mix_out_body(x_ref, oa_ref, ob_ref, mg_ref, gm_ref, woa_ref, wob_ref, wout_ref, o_ref):
    d = x_ref.shape[2]
    y_a = _dot(oa_ref[0], woa_ref[...])
    y_b = _dot_tn(ob_ref[0], wob_ref[...])
    merged = (_sigmoid(mg_ref[0, :, 0:d].astype(F32)) * y_a
              + _sigmoid(mg_ref[0, :, d:2 * d].astype(F32)) * y_b)
    o_ref[0] = x_ref[0] + gm_ref[0] * _dot(merged.astype(BF16), wout_ref[...])


def _mix_out(x, o_a, o_b, proj, mg_blk, g_m, w_oa, w_ob, w_out, tm):
    b, s, d = x.shape
    full = lambda a: pl.BlockSpec(a.shape, lambda bi, i: (0, 0))
    return pl.pallas_call(
        _mix_out_body,
        grid=(b, s // tm),
        in_specs=[pl.BlockSpec((1, tm, d), lambda bi, i: (bi, i, 0)),
                  pl.BlockSpec((1, tm, o_a.shape[2]), lambda bi, i: (bi, i, 0)),
                  pl.BlockSpec((1, o_b.shape[1], tm), lambda bi, i: (bi, 0, i)),
                  pl.BlockSpec((1, tm, 2 * d), lambda bi, i: (bi, i, mg_blk)),
                  pl.BlockSpec((1, 1, d), lambda bi, i: (bi, 0, 0)),
                  full(w_oa), full(w_ob), full(w_out)],
        out_specs=pl.BlockSpec((1, tm, d), lambda bi, i: (bi, i, 0)),
        out_shape=jax.ShapeDtypeStruct(x.shape, F32),
        compiler_params=_params(("parallel", "parallel")),
        name="mix_out",
    )(x, o_a, o_b, proj, g_m, w_oa, w_ob, w_out)


def _ffn_body(x_ref, nw_ref, sc_ref, sh_ref, gf_ref, w1_ref, w3_ref, w2_ref, o_ref, h_ref, acc_ref):
    f = pl.program_id(2)

    @pl.when(f == 0)
    def _():
        h_ref[...] = _norm_mod(x_ref[0], nw_ref[...], sc_ref[0], sh_ref[0]).astype(BF16)
        acc_ref[...] = jnp.zeros_like(acc_ref)

    h = h_ref[...]
    t = _silu(_dot(h, w1_ref[...])) * _dot(h, w3_ref[...])
    acc_ref[...] += _dot(t.astype(BF16), w2_ref[...])

    @pl.when(f == pl.num_programs(2) - 1)
    def _():
        o_ref[0] = x_ref[0] + gf_ref[0] * acc_ref[...]


def _dense_ffn(x, nw, sc, sh, g_f, w1, w3, w2, tm, tf):
    b, s, d = x.shape
    ff = w1.shape[1]
    vec = pl.BlockSpec((1, 1, d), lambda bi, i, f: (bi, 0, 0))
    return pl.pallas_call(
        _ffn_body,
        grid=(b, s // tm, ff // tf),
        in_specs=[pl.BlockSpec((1, tm, d), lambda bi, i, f: (bi, i, 0)),
                  pl.BlockSpec((1, d), lambda bi, i, f: (0, 0)), vec, vec, vec,
                  pl.BlockSpec((d, tf), lambda bi, i, f: (0, f)),
                  pl.BlockSpec((d, tf), lambda bi, i, f: (0, f)),
                  pl.BlockSpec((tf, d), lambda bi, i, f: (f, 0))],
        out_specs=pl.BlockSpec((1, tm, d), lambda bi, i, f: (bi, i, 0)),
        out_shape=jax.ShapeDtypeStruct(x.shape, F32),
        scratch_shapes=[pltpu.VMEM((tm, d), BF16), pltpu.VMEM((tm, d), F32)],
        compiler_params=_params(("parallel", "parallel", "arbitrary")),
        name="dense_ffn",
    )(x, nw, sc, sh, g_f, w1, w3, w2)


MOE_ROWS = 64
MOE_GROUP = 4
MOE_SCATTER = 512


def _moe_route_body(x_ref, nw_ref, sc_ref, sh_ref, wr_ref, h_ref, route_ref, meta_ref):
    tm = x_ref.shape[1]
    h16 = _norm_mod(x_ref[0], nw_ref[...], sc_ref[0], sh_ref[0]).astype(BF16)
    h_ref[0] = h16
    logits = _dot_nt(wr_ref[...], h16)
    ef = lax.broadcasted_iota(jnp.int32, (N_EXPERTS, tm), 0).astype(F32)
    m1 = jnp.max(logits, 0, keepdims=True)
    i1 = jnp.min(jnp.where(logits == m1, ef, float(N_EXPERTS)), 0, keepdims=True)
    rest = jnp.where(ef == i1, -jnp.inf, logits)
    m2 = jnp.max(rest, 0, keepdims=True)
    i2 = jnp.min(jnp.where(rest == m2, ef, float(N_EXPERTS)), 0, keepdims=True)
    e2 = jnp.exp(m2 - m1)
    oh1 = jnp.where(ef == i1, 1.0, 0.0)
    oh2 = jnp.where(ef == i2, 1.0, 0.0)
    member = oh1 + oh2
    lane = lax.broadcasted_iota(jnp.int32, (N_EXPERTS, tm), 1)
    csum = member
    sft = 1
    while sft < tm:
        csum = csum + jnp.where(lane >= sft, pltpu.roll(csum, sft, 1), 0.0)
        sft *= 2
    count = jnp.max(csum, 1, keepdims=True)
    nblk = jnp.floor((count + (MOE_ROWS - 1)) * (1.0 / MOE_ROWS))
    nblk_b = jnp.broadcast_to(nblk, (N_EXPERTS, LANES))
    row = lax.broadcasted_iota(jnp.int32, (N_EXPERTS, LANES), 0)
    bsum = nblk_b
    for sft in (1, 2, 4):
        bsum = bsum + jnp.where(row >= sft, pltpu.roll(bsum, sft, 0), 0.0)
    bstart = bsum - nblk_b
    slot = bstart[:, 0:1] * MOE_ROWS + (csum - member)
    rrow = lax.broadcasted_iota(jnp.int32, (8, tm), 0)
    route_ref[0] = jnp.where(
        rrow == 0, jnp.sum(oh1 * slot, 0, keepdims=True),
        jnp.where(rrow == 1, jnp.sum(oh2 * slot, 0, keepdims=True),
                  jnp.where(rrow == 2, 1.0 / (1.0 + e2), jnp.where(rrow == 3, e2 / (1.0 + e2), 0.0))))
    col = lax.broadcasted_iota(jnp.int32, (N_EXPERTS, LANES), 1)
    meta_ref[0] = jnp.where(col == 0, nblk_b, jnp.where(col == 1, bstart, 0.0)).astype(jnp.int32)


def _moe_group_body(nblk_ref, bstart_ref, x_ref, h_ref, route_ref, gf_ref, w1_ref, w3_ref, w2_ref, o_ref,
                    hb_ref, cw_ref, acc_ref, *, n_rows):
    i = pl.program_id(0)
    e = pl.program_id(1)
    f = pl.program_id(2)
    tm = x_ref.shape[1]
    nb = nblk_ref[i * N_EXPERTS + e]
    b0 = bstart_ref[i * N_EXPERTS + e]
    slot1 = route_ref[0, 0:1, :]
    slot2 = route_ref[0, 1:2, :]

    def hits(r0, rows):
        rr = (r0 + lax.broadcasted_iota(jnp.int32, (rows, 1), 0)).astype(F32)
        return rr == slot1, rr == slot2

    def expert_rows(blk, n_blk, first):
        r0 = pl.multiple_of((b0 + blk) * MOE_ROWS, MOE_ROWS)
        rows = pl.ds(r0, n_blk * MOE_ROWS)
        if first:
            hit1, hit2 = hits(r0, n_blk * MOE_ROWS)
            gather = jnp.where(hit1, 1.0, jnp.where(hit2, 1.0, 0.0)).astype(BF16)
            hb_ref[rows, :] = _dot(gather, h_ref[0]).astype(BF16)
            cw_ref[rows, :] = jnp.sum(jnp.where(hit1, route_ref[0, 2:3, :], 0.0)
                                      + jnp.where(hit2, route_ref[0, 3:4, :], 0.0), -1, keepdims=True)
        hb = hb_ref[rows, :]
        t = _silu(_dot(hb, w1_ref[0, 0])) * _dot(hb, w3_ref[0, 0]) * cw_ref[rows, :]
        y = _dot(t.astype(BF16), w2_ref[0])
        if first:
            acc_ref[rows, :] = y
        else:
            acc_ref[rows, :] += y

    def expert_all(first):
        lax.fori_loop(0, nb // MOE_GROUP, lambda k, c: (expert_rows(MOE_GROUP * k, MOE_GROUP, first), c)[1], 0)
        for rem in range(1, MOE_GROUP):
            @pl.when(nb % MOE_GROUP == rem)
            def _(rem=rem):
                expert_rows(nb - rem, rem, first)

    @pl.when(f == 0)
    def _():
        expert_all(True)

    @pl.when(f != 0)
    def _():
        expert_all(False)

    @pl.when((e == pl.num_programs(1) - 1) & (f == pl.num_programs(2) - 1))
    def _():
        def clear(k, c):
            acc_ref[pl.ds(pl.multiple_of(k * MOE_ROWS, MOE_ROWS), MOE_ROWS), :] = jnp.zeros(
                (MOE_ROWS, acc_ref.shape[1]), F32)
            return c

        lax.fori_loop(b0 + nb, n_rows // MOE_ROWS, clear, 0)
        chunk = MOE_SCATTER
        for kc in range(n_rows // chunk):
            hit1, hit2 = hits(kc * chunk, chunk)
            scatter = jnp.where(hit1, 1.0, jnp.where(hit2, 1.0, 0.0)).astype(BF16)
            y = _dot_tn(scatter, acc_ref[kc * chunk:(kc + 1) * chunk, :].astype(BF16))
            if kc == 0:
                o_ref[0] = y
            else:
                o_ref[0] += y
        o_ref[0] = x_ref[0] + gf_ref[0] * o_ref[0]


def _moe_ffn(x, nw, sc, sh, g_f, w_router, w1, w3, w2, tm, tf):
    b, s, d = x.shape
    n_e, _, ff = w1.shape
    assert n_e == N_EXPERTS
    tiles_b = s // tm
    nt = b * tiles_b
    vec = pl.BlockSpec((1, 1, d), lambda bi, i: (bi, 0, 0))
    h16, route, meta = pl.pallas_call(
        _moe_route_body,
        grid=(b, tiles_b),
        in_specs=[pl.BlockSpec((1, tm, d), lambda bi, i: (bi, i, 0)),
                  pl.BlockSpec((1, d), lambda bi, i: (0, 0)), vec, vec,
                  pl.BlockSpec((n_e, d), lambda bi, i: (0, 0))],
        out_specs=[pl.BlockSpec((1, tm, d), lambda bi, i: (bi, i, 0)),
                   pl.BlockSpec((1, 8, tm), lambda bi, i: (bi * tiles_b + i, 0, 0)),
                   pl.BlockSpec((1, n_e, LANES), lambda bi, i: (bi * tiles_b + i, 0, 0))],
        out_shape=[jax.ShapeDtypeStruct((b, s, d), BF16),
                   jax.ShapeDtypeStruct((nt, 8, tm), F32),
                   jax.ShapeDtypeStruct((nt, n_e, LANES), jnp.int32)],
        compiler_params=_params(("parallel", "parallel")),
        name="moe_route",
    )(x, nw, sc, sh, w_router.T.astype(BF16))
    chunked = lambda w: w.reshape(n_e, d, ff // tf, tf).transpose(0, 2, 1, 3).astype(BF16)
    n_rows = -(-(2 * tm + n_e * (MOE_ROWS - 1)) // MOE_SCATTER) * MOE_SCATTER
    grid_spec = pltpu.PrefetchScalarGridSpec(
        num_scalar_prefetch=2,
        grid=(nt, n_e, ff // tf),
        in_specs=[pl.BlockSpec((1, tm, d), lambda i, e, f, nb, bs: (i, 0, 0)),
                  pl.BlockSpec((1, tm, d), lambda i, e, f, nb, bs: (i, 0, 0)),
                  pl.BlockSpec((1, 8, tm), lambda i, e, f, nb, bs: (i, 0, 0)),
                  pl.BlockSpec((1, 1, d), lambda i, e, f, nb, bs: (i // tiles_b, 0, 0)),
                  pl.BlockSpec((1, 1, d, tf), lambda i, e, f, nb, bs: (e, f, 0, 0)),
                  pl.BlockSpec((1, 1, d, tf), lambda i, e, f, nb, bs: (e, f, 0, 0)),
                  pl.BlockSpec((1, tf, d), lambda i, e, f, nb, bs: (e, f, 0))],
        out_specs=pl.BlockSpec((1, tm, d), lambda i, e, f, nb, bs: (i, 0, 0)),
        scratch_shapes=[pltpu.VMEM((n_rows, d), BF16), pltpu.VMEM((n_rows, 1), F32),
                        pltpu.VMEM((n_rows, d), F32)])
    out = pl.pallas_call(
        functools.partial(_moe_group_body, n_rows=n_rows),
        grid_spec=grid_spec,
        out_shape=jax.ShapeDtypeStruct((nt, tm, d), F32),
        compiler_params=_params(("parallel", "arbitrary", "arbitrary")),
        name="moe_group",
    )(meta[:, :, 0].reshape(-1), meta[:, :, 1].reshape(-1),
      x.reshape(nt, tm, d), h16.reshape(nt, tm, d), route, g_f, chunked(w1), chunked(w3), w2)
    return out.reshape(b, s, d)


def _rope_tables(pos):
    inv = 1.0 / (ROPE_THETA ** (jnp.arange(0, NSA_DH, 2, dtype=F32) / NSA_DH))
    ang = pos.astype(F32)[..., None] * inv
    cos, sin = jnp.cos(ang), jnp.sin(ang)
    reps = LANES // NSA_DH
    return (jnp.tile(jnp.concatenate([cos, cos], -1), (1, 1, reps)),
            jnp.tile(jnp.concatenate([-sin, sin], -1), (1, 1, reps)))


_SPLITS = (DN_QKV, DN_HEADS * DN_DV, DN_HEADS, DN_HEADS, NSA_HEADS * NSA_DH) + (NSA_GROUPS * NSA_DH,) * 6
_OFF = np.concatenate([[0], np.cumsum(_SPLITS)])
_OFF_NG = int(_OFF[-1])
_OFF_MG = _OFF_NG + 3 * NSA_HEADS


def kernel(x, c, positions, w_ada, b_ada, norm_mix, norm_ffn, w_in, conv_w, a_log, dt_bias, dn_norm, cmp_pos, w_cmp1, w_cmp2, q_norm, k_norm, w_oa, w_ob, w_out, w1_dense, w3_dense, w2_dense, w_router, w1_moe, w3_moe, w2_moe):
    b, s, d = x.shape
    depth = w_in.shape[0]
    wdn = DN_QKV + DN_HEADS * DN_DV
    n_small = 2 * DN_HEADS + 3 * NSA_HEADS

    cos_f, sin_s = _rope_tables(positions)
    n_cmp_pad = s // CMP_STRIDE
    cmp_end = jnp.minimum(jnp.arange(n_cmp_pad) * CMP_STRIDE + CMP_LEN - 1, s - 1)
    cos_c, sin_c = _rope_tables(positions[:, cmp_end])

    mod = _ada_mod(c, w_ada, b_ada)

    off_nq = int(_OFF[4])
    w_main = jnp.concatenate([w_in[:, :, 0:wdn], w_in[:, :, _OFF_MG:_OFF_MG + 2 * d],
                              w_in[:, :, off_nq:_OFF_NG]], -1).astype(BF16)
    w_small = jnp.concatenate([w_in[:, :, wdn:wdn + 2 * DN_HEADS], w_in[:, :, _OFF_NG:_OFF_MG],
                               jnp.zeros((depth, d, LANES - n_small), F32)], -1).astype(BF16)
    nsa_col0 = wdn + 2 * d
    n_main = w_main.shape[2]

    w_oa16, w_ob16, w_out16 = w_oa.astype(BF16), w_ob.astype(BF16), w_out.astype(BF16)
    w1d, w3d, w2d = w1_dense.astype(BF16), w3_dense.astype(BF16), w2_dense.astype(BF16)
    w1m, w3m, w2m = w1_moe, w3_moe, w2_moe.astype(BF16)

    for l in range(depth):
        sh_m, sc_m, g_m, sh_f, sc_f, g_f = [m.reshape(b, 1, d) for m in jnp.split(mod[l], 6, -1)]
        nw_m = norm_mix[l].reshape(1, d)
        proj, small = _norm_mod_matmul(x, nw_m, sc_m, sh_m, w_main[l], w_small[l], tm=1024, tn=n_main // 3)
        a_t = jnp.swapaxes(small[:, :, DN_HEADS:2 * DN_HEADS], 1, 2)
        o_a = _deltanet(proj, small, a_t, conv_w[l], a_log[l], dt_bias[l], dn_norm[l], ts=512)
        qn, ksn, vs, kwn, vw, ck, cv = _nsa_prep(proj, nsa_col0, cos_f, sin_s, q_norm[l],
                                                 k_norm[l, 1], k_norm[l, 2], ts=512)
        kc, vc = _compress(ck, cv, cmp_pos[l], w_cmp1[l], w_cmp2[l], k_norm[l, 0], cos_c, sin_c)
        gates_t = jnp.swapaxes(small[:, :, 2 * DN_HEADS:n_small], 1, 2).reshape(b, NSA_GROUPS, 3 * NSA_HPG, s)
        o_b = _nsa_attention(qn, kc, vc, ksn, vs, kwn, vw, gates_t, tq=512, tk=1024)
        x = _mix_out(x, o_a, o_b, proj, wdn // (2 * d), g_m, w_oa16[l], w_ob16[l], w_out16[l], tm=512)
        nw_f = norm_ffn[l].reshape(1, d)
        if l % 2 == 0:
            x = _dense_ffn(x, nw_f, sc_f, sh_f, g_f, w1d[l // 2], w3d[l // 2], w2d[l // 2], tm=1024, tf=512)
        else:
            x = _moe_ffn(x, nw_f, sc_f, sh_f, g_f, w_router[l // 2], w1m[l // 2], w3m[l // 2],
                         w2m[l // 2], tm=1024, tf=896)
    return x
```

```python
import functools

import jax
import jax.numpy as jnp
import numpy as np
from jax import lax
from jax.experimental import pallas as pl
from jax.experimental.pallas import tpu as pltpu

F32 = jnp.float32
BF16 = jnp.bfloat16

DN_HEADS = 8
DN_DK = 64
DN_DV = 64
DN_CHUNK = 64
CONV_W = 4
DN_QKV = DN_HEADS * (2 * DN_DK + DN_DV)
NSA_HEADS = 8
NSA_GROUPS = 2
NSA_HPG = NSA_HEADS // NSA_GROUPS
NSA_DH = 64
CMP_LEN = 32
CMP_STRIDE = 16
SEL_LEN = 64
SEL_SHIFT = 6
SEL_TOP = 16
WINDOW = 512
ROPE_THETA = 10000.0
N_EXPERTS = 8
EPS = 1e-6
NEG = -1e30
FORCE = 1e6
SEL_BIAS = 1e30

LANES = 128
VMEM_LIMIT = 56 * 1024 * 1024


def _sigmoid(x):
    return 1.0 / (1.0 + jnp.exp(-x))


def _silu(x):
    return x * _sigmoid(x)


def _softplus(x):
    return jnp.maximum(x, 0.0) + jnp.log(1.0 + jnp.exp(-jnp.abs(x)))


def _dot(a, b):
    return jnp.dot(a, b, preferred_element_type=F32)


def _dot_nt(a, b):
    return lax.dot_general(a, b, (((1,), (1,)), ((), ())), preferred_element_type=F32)


def _dot_tn(a, b):
    return lax.dot_general(a, b, (((0,), (0,)), ((), ())), preferred_element_type=F32)


def _norm_mod(x, nw, sc, sh):
    y = x * lax.rsqrt(jnp.mean(x * x, -1, keepdims=True) + EPS) * nw
    return y * (1.0 + sc) + sh


def _params(sem):
    return pltpu.CompilerParams(dimension_semantics=sem, vmem_limit_bytes=VMEM_LIMIT)


def _mod_body(c_ref, w_ref, b_ref, o_ref):
    c = c_ref[...]
    o_ref[0] = _dot(_silu(c).astype(BF16), w_ref[0].astype(BF16)) + b_ref[0]


def _ada_mod(c, w_ada, b_ada):
    n_layers, d, n = w_ada.shape
    b = c.shape[0]
    tn = n // 4
    return pl.pallas_call(
        _mod_body,
        grid=(n_layers, n // tn),
        in_specs=[pl.BlockSpec((b, d), lambda l, j: (0, 0)),
                  pl.BlockSpec((1, d, tn), lambda l, j: (l, 0, j)),
                  pl.BlockSpec((1, 1, tn), lambda l, j: (l, 0, j))],
        out_specs=pl.BlockSpec((1, b, tn), lambda l, j: (l, 0, j)),
        out_shape=jax.ShapeDtypeStruct((n_layers, b, n), F32),
        compiler_params=_params(("parallel", "parallel")),
        name="ada_mod",
    )(c, w_ada, b_ada.reshape(n_layers, 1, n))


def _nm_mm_body(x_ref, nw_ref, sc_ref, sh_ref, w_ref, ws_ref, o_ref, os_ref, h_ref):
    @pl.when(pl.program_id(2) == 0)
    def _():
        h = _norm_mod(x_ref[0], nw_ref[...], sc_ref[0], sh_ref[0]).astype(BF16)
        h_ref[...] = h
        os_ref[0] = _dot(h, ws_ref[...])

    o_ref[0] = _dot(h_ref[...], w_ref[...]).astype(o_ref.dtype)


def _norm_mod_matmul(x, nw, sc, sh, w, w_small, tm, tn):
    b, s, d = x.shape
    n = w.shape[1]
    ns = w_small.shape[1]
    return pl.pallas_call(
        _nm_mm_body,
        grid=(b, s // tm, n // tn),
        in_specs=[pl.BlockSpec((1, tm, d), lambda bi, i, j: (bi, i, 0)),
                  pl.BlockSpec((1, d), lambda bi, i, j: (0, 0)),
                  pl.BlockSpec((1, 1, d), lambda bi, i, j: (bi, 0, 0)),
                  pl.BlockSpec((1, 1, d), lambda bi, i, j: (bi, 0, 0)),
                  pl.BlockSpec((d, tn), lambda bi, i, j: (0, j)),
                  pl.BlockSpec((d, ns), lambda bi, i, j: (0, 0))],
        out_specs=[pl.BlockSpec((1, tm, tn), lambda bi, i, j: (bi, i, j)),
                   pl.BlockSpec((1, tm, ns), lambda bi, i, j: (bi, i, 0))],
        out_shape=[jax.ShapeDtypeStruct((b, s, n), BF16), jax.ShapeDtypeStruct((b, s, ns), F32)],
        scratch_shapes=[pltpu.VMEM((tm, d), BF16)],
        compiler_params=_params(("parallel", "parallel", "arbitrary")),
        name="in_proj",
    )(x, nw, sc, sh, w, w_small)


def _dn_body(x_ref, sm_ref, at_ref, cw_ref, alog_ref, dtb_ref, alogt_ref, dtbt_ref, dnw_ref, o_ref,
             buf_ref, act_ref, gcn_ref, beta_ref, gct_ref, state_ref, *, ts):
    nc = ts // DN_CHUNK
    c64 = DN_CHUNK

    @pl.when(pl.program_id(1) == 0)
    def _():
        buf_ref[0:8, :] = jnp.zeros((8, DN_QKV), F32)
        state_ref[...] = jnp.zeros_like(state_ref)

    for sl in range(DN_QKV // LANES):
        cols = slice(sl * LANES, (sl + 1) * LANES)
        buf_ref[8:ts + 8, cols] = x_ref[0, :, cols].astype(F32)
        y = cw_ref[0:1, cols] * buf_ref[5:5 + ts, cols]
        for j in range(1, CONV_W):
            y = y + cw_ref[j:j + 1, cols] * buf_ref[5 + j:5 + j + ts, cols]
        buf_ref[0:8, cols] = buf_ref[ts:ts + 8, cols]
        act_ref[:, :, cols] = _silu(y).reshape(nc, c64, LANES)

    sm = sm_ref[0]
    beta_ref[...] = _sigmoid(sm).reshape(nc, c64, LANES)
    g = -jnp.exp(alog_ref[...]) * _softplus(sm + dtb_ref[...])
    row = lax.broadcasted_iota(jnp.int32, (ts, LANES), 0) & (c64 - 1)
    for sft in (1, 2, 4, 8, 16, 32):
        g = g + jnp.where(row >= sft, pltpu.roll(g, sft, 0), 0.0)
    gcn_ref[...] = g.reshape(nc, c64, LANES)
    gt = -jnp.exp(alogt_ref[...]) * _softplus(at_ref[0] + dtbt_ref[...])
    lane = lax.broadcasted_iota(jnp.int32, (DN_HEADS, ts), 1) & (c64 - 1)
    for sft in (1, 2, 4, 8, 16, 32):
        gt = gt + jnp.where(lane >= sft, pltpu.roll(gt, sft, 1), 0.0)
    for c in range(nc):
        gct_ref[c] = gt[:, c * c64:(c + 1) * c64]

    ri = lax.broadcasted_iota(jnp.int32, (c64, c64), 0)
    ci = lax.broadcasted_iota(jnp.int32, (c64, c64), 1)
    tril = ri >= ci
    strict = ri > ci
    eye = jnp.where(ri == ci, 1.0, 0.0).astype(F32)
    ones = jnp.ones((c64, c64), F32)
    dnw = dnw_ref[...]

    hs = range(DN_HEADS)
    grp = 2 if nc % 2 == 0 else 1

    def chunk_group(cg, carry):
        items = [(j, h) for j in range(grp) for h in hs]
        n = range(len(items))
        cs = [cg * grp + j for j in range(grp)]
        gcn = [gcn_ref[c] for c in cs]
        bet = [beta_ref[c] for c in cs]
        gct = [gct_ref[c] for c in cs]
        q = [act_ref[cs[j], :, h * DN_DK:(h + 1) * DN_DK] for j, h in items]
        k = [act_ref[cs[j], :, (DN_HEADS + h) * DN_DK:(DN_HEADS + h + 1) * DN_DK] for j, h in items]
        v = [act_ref[cs[j], :, 2 * DN_HEADS * DN_DK + h * DN_DV:2 * DN_HEADS * DN_DK + (h + 1) * DN_DV]
             for j, h in items]
        q = [x * lax.rsqrt(_dot(x * x, ones) + EPS) * (DN_DK ** -0.5) for x in q]
        k = [x * lax.rsqrt(_dot(x * x, ones) + EPS) for x in k]
        bcol = [bet[j][:, h:h + 1] for j, h in items]
        gcol = [gcn[j][:, DN_HEADS + h:DN_HEADS + h + 1] for j, h in items]
        grow = [gct[j][h:h + 1, :] for j, h in items]
        decay = [jnp.where(tril, jnp.exp(jnp.where(tril, gcol[i] - grow[i], 0.0)), 0.0) for i in n]
        eg = [jnp.exp(x) for x in gcol]
        glast = [x[c64 - 1:c64, :] for x in gcol]
        kb = [k[i] * bcol[i] for i in n]
        k16 = [x.astype(BF16) for x in k]
        kk = [_dot_nt(kb[i].astype(BF16), k16[i]) for i in n]
        qk = [_dot_nt(q[i].astype(BF16), k16[i]) for i in n]
        a16 = [jnp.where(tril, qk[i] * decay[i], 0.0).astype(BF16) for i in n]
        m = [jnp.where(strict, -(kk[i] * decay[i]), 0.0) for i in n]
        p = [eye + x for x in m]
        for _ in range(5):
            m = [_dot(x, x) for x in m]
            p = [p[i] + _dot(m[i], p[i]) for i in n]
        tinv = [x.astype(BF16) for x in p]
        u = [_dot(tinv[i], (v[i] * bcol[i]).astype(BF16)) for i in n]
        w16 = [_dot(tinv[i], (kb[i] * eg[i]).astype(BF16)).astype(BF16) for i in n]
        qe16 = [(q[i] * eg[i]).astype(BF16) for i in n]
        kd16 = [(k[i] * jnp.exp(glast[i] - gcol[i])).astype(BF16) for i in n]
        egl = [jnp.exp(x) for x in glast]
        for j in range(grp):
            idx = [j * DN_HEADS + h for h in hs]
            r0 = pl.multiple_of(cs[j] * c64, c64)
            st = [state_ref[h] for h in hs]
            st16 = [x.astype(BF16) for x in st]
            ws = [_dot(w16[idx[h]], st16[h]) for h in hs]
            vn16 = [(u[idx[h]] - ws[h]).astype(BF16) for h in hs]
            qs = [_dot(qe16[idx[h]], st16[h]) for h in hs]
            av = [_dot(a16[idx[h]], vn16[h]) for h in hs]
            kv = [_dot_tn(kd16[idx[h]], vn16[h]) for h in hs]
            for h in hs:
                state_ref[h] = st[h] * egl[idx[h]] + kv[h]
                o = qs[h] + av[h]
                on = o * lax.rsqrt(jnp.mean(o * o, -1, keepdims=True) + EPS) * dnw
                z = x_ref[0, pl.ds(r0, c64), DN_QKV + h * DN_DV:DN_QKV + (h + 1) * DN_DV].astype(F32)
                o_ref[0, pl.ds(r0, c64), h * DN_DV:(h + 1) * DN_DV] = (on * _silu(z)).astype(o_ref.dtype)
        return carry

    lax.fori_loop(0, nc // grp, chunk_group, 0)


def _deltanet(proj, small, a_t, conv_w, a_log, dt_bias, dn_norm, ts):
    b, s, _ = proj.shape
    wdn = DN_QKV + DN_HEADS * DN_DV
    pad = jnp.zeros((LANES - 2 * DN_HEADS,), F32)
    alog_row = jnp.concatenate([jnp.zeros((DN_HEADS,), F32), a_log, pad]).reshape(1, LANES)
    dtb_row = jnp.concatenate([jnp.zeros((DN_HEADS,), F32), dt_bias, pad]).reshape(1, LANES)
    nc = ts // DN_CHUNK
    return pl.pallas_call(
        functools.partial(_dn_body, ts=ts),
        grid=(b, s // ts),
        in_specs=[pl.BlockSpec((1, ts, wdn), lambda bi, i: (bi, i, 0)),
                  pl.BlockSpec((1, ts, LANES), lambda bi, i: (bi, i, 0)),
                  pl.BlockSpec((1, DN_HEADS, ts), lambda bi, i: (bi, 0, i)),
                  pl.BlockSpec((CONV_W, DN_QKV), lambda bi, i: (0, 0)),
                  pl.BlockSpec((1, LANES), lambda bi, i: (0, 0)),
                  pl.BlockSpec((1, LANES), lambda bi, i: (0, 0)),
                  pl.BlockSpec((DN_HEADS, 1), lambda bi, i: (0, 0)),
                  pl.BlockSpec((DN_HEADS, 1), lambda bi, i: (0, 0)),
                  pl.BlockSpec((1, DN_DV), lambda bi, i: (0, 0))],
        out_specs=pl.BlockSpec((1, ts, DN_HEADS * DN_DV), lambda bi, i: (bi, i, 0)),
        out_shape=jax.ShapeDtypeStruct((b, s, DN_HEADS * DN_DV), BF16),
        scratch_shapes=[pltpu.VMEM((ts + 8, DN_QKV), F32),
                        pltpu.VMEM((nc, DN_CHUNK, DN_QKV), F32),
                        pltpu.VMEM((nc, DN_CHUNK, LANES), F32),
                        pltpu.VMEM((nc, DN_CHUNK, LANES), F32),
                        pltpu.VMEM((nc, DN_HEADS, DN_CHUNK), F32),
                        pltpu.VMEM((DN_HEADS, DN_DK, DN_DV), F32)],
        compiler_params=_params(("parallel", "arbitrary")),
        name="deltanet",
    )(proj, small, a_t, conv_w, alog_row, dtb_row, a_log.reshape(DN_HEADS, 1),
      dt_bias.reshape(DN_HEADS, 1), dn_norm.reshape(1, DN_DV))


def _seg_ones():
    r = lax.broadcasted_iota(jnp.int32, (LANES, LANES), 0) // NSA_DH
    c = lax.broadcasted_iota(jnp.int32, (LANES, LANES), 1) // NSA_DH
    return jnp.where(r == c, 1.0, 0.0).astype(F32)


V_ROWS = 80
LOG2E = 1.4426950408889634


def _eye16(n):
    r = lax.broadcasted_iota(jnp.int32, (n, n), 0)
    c = lax.broadcasted_iota(jnp.int32, (n, n), 1)
    return jnp.where(r == c, 1.0, 0.0).astype(BF16)


def _transpose16(x16, eye):
    return _dot_tn(x16, eye).astype(BF16)


def _norm_rope(x, w, cos_f, sin_s, seg):
    ms = _dot(x * x, seg) * (1.0 / NSA_DH)
    y = x * lax.rsqrt(ms + EPS) * w
    half = NSA_DH // 2
    lane = lax.broadcasted_iota(jnp.int32, y.shape, 1) & (NSA_DH - 1)
    partner = jnp.where(lane < half, pltpu.roll(y, LANES - half, 1), pltpu.roll(y, half, 1))
    return y * cos_f + partner * sin_s


def _nsa_prep_body(q_ref, ck_ref, cv_ref, sk_ref, sv_ref, wk_ref, wv_ref, cos_ref, sin_ref,
                   qw_ref, skw_ref, wkw_ref,
                   qo_ref, sko_ref, svo_ref, wko_ref, wvo_ref, cko_ref, cvo_ref):
    seg = _seg_ones()
    cos_f = cos_ref[0]
    sin_s = sin_ref[0]
    ts = cos_f.shape[0]
    eye = _eye16(ts)
    for sl in range(NSA_HEADS * NSA_DH // LANES):
        x = q_ref[0, :, sl * LANES:(sl + 1) * LANES].astype(F32)
        y = (_norm_rope(x, qw_ref[...], cos_f, sin_s, seg) * (NSA_DH ** -0.5 * LOG2E)).astype(BF16)
        y_t = _transpose16(y, eye)
        for half in range(2):
            h = 2 * sl + half
            qo_ref[0, h // NSA_HPG, h % NSA_HPG, 0:NSA_DH, :] = y_t[half * NSA_DH:(half + 1) * NSA_DH]
            qo_ref[0, h // NSA_HPG, h % NSA_HPG, NSA_DH:LANES, :] = jnp.zeros((LANES - NSA_DH, ts), BF16)
    sk = _norm_rope(sk_ref[0].astype(F32), skw_ref[...], cos_f, sin_s, seg).astype(BF16)
    wk = _norm_rope(wk_ref[0].astype(F32), wkw_ref[...], cos_f, sin_s, seg).astype(BF16)
    sv_t = _transpose16(sv_ref[0], eye)
    wv_t = _transpose16(wv_ref[0], eye)
    ones_row = jnp.where(lax.broadcasted_iota(jnp.int32, (V_ROWS - NSA_DH, ts), 0) == 0, 1.0, 0.0).astype(BF16)
    blk = (pl.program_id(1) * ts + lax.broadcasted_iota(jnp.int32, (ts, LANES), 0)) >> SEL_SHIFT
    onehot = jnp.where(lax.broadcasted_iota(jnp.int32, (ts, LANES), 1) == blk, SEL_BIAS, 0.0).astype(BF16)
    for g in range(NSA_GROUPS):
        cols = slice(g * NSA_DH, (g + 1) * NSA_DH)
        sko_ref[0, g, :, 0:LANES] = onehot
        sko_ref[0, g, :, LANES:LANES + NSA_DH] = sk[:, cols]
        sko_ref[0, g, :, LANES + NSA_DH:2 * LANES] = jnp.zeros((ts, LANES - NSA_DH), BF16)
        wko_ref[0, g] = wk[:, cols]
        svo_ref[0, g, 0:NSA_DH, :] = sv_t[cols]
        svo_ref[0, g, NSA_DH:V_ROWS, :] = ones_row
        wvo_ref[0, g, 0:NSA_DH, :] = wv_t[cols]
        wvo_ref[0, g, NSA_DH:V_ROWS, :] = ones_row
    cko_ref[0] = ck_ref[0]
    cvo_ref[0] = cv_ref[0]


def _nsa_prep(proj, col0, cos_f, sin_s, q_norm, k_norm_s, k_norm_w, ts):
    b, s, _ = proj.shape
    assert s // SEL_LEN <= LANES
    qw = NSA_HEADS * NSA_DH
    qblk = col0 // qw
    k0 = (col0 + qw) // LANES

    def kspec(i):
        return pl.BlockSpec((1, ts, LANES), lambda bi, t, i=i: (bi, t, k0 + i))

    tile2 = lambda w: jnp.tile(w.reshape(1, NSA_DH), (1, LANES // NSA_DH))
    gshape = jax.ShapeDtypeStruct((b, NSA_GROUPS, s, NSA_DH), BF16)
    gspec = pl.BlockSpec((1, NSA_GROUPS, ts, NSA_DH), lambda bi, t: (bi, 0, t, 0))
    ashape = jax.ShapeDtypeStruct((b, NSA_GROUPS, s, 2 * LANES), BF16)
    aspec = pl.BlockSpec((1, NSA_GROUPS, ts, 2 * LANES), lambda bi, t: (bi, 0, t, 0))
    vshape = jax.ShapeDtypeStruct((b, NSA_GROUPS, V_ROWS, s), BF16)
    vspec = pl.BlockSpec((1, NSA_GROUPS, V_ROWS, ts), lambda bi, t: (bi, 0, 0, t))
    cspec = pl.BlockSpec((1, ts, LANES), lambda bi, t: (bi, t, 0))
    wspec = pl.BlockSpec((1, LANES), lambda bi, t: (0, 0))
    return pl.pallas_call(
        _nsa_prep_body,
        grid=(b, s // ts),
        in_specs=[pl.BlockSpec((1, ts, qw), lambda bi, t: (bi, t, qblk)),
                  kspec(0), kspec(1), kspec(2), kspec(3), kspec(4), kspec(5),
                  cspec, cspec, wspec, wspec, wspec],
        out_specs=[pl.BlockSpec((1, NSA_GROUPS, NSA_HPG, LANES, ts), lambda bi, t: (bi, 0, 0, 0, t)),
                   aspec, vspec, gspec, vspec, cspec, cspec],
        out_shape=[jax.ShapeDtypeStruct((b, NSA_GROUPS, NSA_HPG, LANES, s), BF16),
                   ashape, vshape, gshape, vshape,
                   jax.ShapeDtypeStruct((b, s, LANES), BF16),
                   jax.ShapeDtypeStruct((b, s, LANES), BF16)],
        compiler_params=_params(("parallel", "parallel")),
        name="nsa_prep",
    )(proj, proj, proj, proj, proj, proj, proj, cos_f, sin_s,
      tile2(q_norm), tile2(k_norm_s), tile2(k_norm_w))


def _compress_body(ck_ref, cv_ref, pos_ref, w1a_ref, w1b_ref, w2_ref, kw_ref, cos_ref, sin_ref,
                   ko_ref, vo_ref):
    n = ck_ref.shape[1]
    outs = []
    for which, x_ref in enumerate((ck_ref, cv_ref)):
        x = x_ref[0].astype(F32)
        lo = _dot((x + pos_ref[which, 0:1, :]).astype(BF16), w1a_ref[which])
        hi = _dot((x + pos_ref[which, 1:2, :]).astype(BF16), w1b_ref[which])
        h1 = _silu(lo + pltpu.roll(hi, n - 1, 0))
        outs.append(_dot(h1.astype(BF16), w2_ref[which]))
    kc = _norm_rope(outs[0], kw_ref[...], cos_ref[0], sin_ref[0], _seg_ones()).astype(BF16)
    vc_t = _transpose16(outs[1].astype(BF16), _eye16(n))
    for g in range(NSA_GROUPS):
        ko_ref[0, g] = kc[:, g * NSA_DH:(g + 1) * NSA_DH]
        vo_ref[0, g] = vc_t[g * NSA_DH:(g + 1) * NSA_DH]


def _compress(ck, cv, cmp_pos, w_cmp1, w_cmp2, k_norm_c, cos_c, sin_c):
    b, s, _ = ck.shape
    n = s // CMP_STRIDE
    width = CMP_STRIDE * LANES
    per_row = CMP_LEN // CMP_STRIDE
    eye_g = jnp.eye(NSA_GROUPS, dtype=F32)
    w1 = w_cmp1.reshape(2, per_row, CMP_STRIDE, NSA_DH, NSA_DH)
    w1 = jnp.einsum('khldo,gG->khlgdGo', w1, eye_g).reshape(2, per_row, width, LANES).astype(BF16)
    w2 = jnp.einsum('kdo,gG->kgdGo', w_cmp2, eye_g).reshape(2, LANES, LANES).astype(BF16)
    pos = jnp.broadcast_to(cmp_pos.reshape(2, per_row, CMP_STRIDE, 1, NSA_DH),
                           (2, per_row, CMP_STRIDE, NSA_GROUPS, NSA_DH)).reshape(2, per_row, width)
    kw = jnp.tile(k_norm_c.reshape(1, NSA_DH), (1, NSA_GROUPS))
    full = lambda shp: pl.BlockSpec(shp, lambda bi: (0,) * len(shp))
    bspec = pl.BlockSpec((1, n, width), lambda bi: (bi, 0, 0))
    tspec = pl.BlockSpec((1, n, LANES), lambda bi: (bi, 0, 0))
    ospec = pl.BlockSpec((1, NSA_GROUPS, n, NSA_DH), lambda bi: (bi, 0, 0, 0))
    oshape = jax.ShapeDtypeStruct((b, NSA_GROUPS, n, NSA_DH), BF16)
    return pl.pallas_call(
        _compress_body,
        grid=(b,),
        in_specs=[bspec, bspec, full((2, per_row, width)), full((2, width, LANES)),
                  full((2, width, LANES)), full((2, LANES, LANES)), full((1, LANES)), tspec, tspec],
        out_specs=[ospec, pl.BlockSpec((1, NSA_GROUPS, NSA_DH, n), lambda bi: (bi, 0, 0, 0))],
        out_shape=[oshape, jax.ShapeDtypeStruct((b, NSA_GROUPS, NSA_DH, n), BF16)],
        compiler_params=_params(("parallel",)),
        name="nsa_compress",
    )(ck.reshape(b, n, width), cv.reshape(b, n, width), pos, w1[:, 0], w1[:, 1], w2, kw, cos_c, sin_c)


def _nsa_body(qt_ref, kc_ref, vct_ref, ksa_ref, vst_ref, kw_ref, vwt_ref, gt_ref, o_ref,
              qa_ref, sa_ref, sb_ref, os_ref, *, tq, tk, n_sel):
    hp = NSA_HPG
    hs = range(hp)
    t0 = pl.program_id(2) * tq
    q_t = [qt_ref[0, 0, j, 0:NSA_DH, :] for j in hs]
    t_row = t0 + lax.broadcasted_iota(jnp.int32, (1, tq), 1)

    kc = kc_ref[0, 0]
    vc_t = vct_ref[0, 0]
    n_cmp = kc.shape[0]
    cmp_end = lax.broadcasted_iota(jnp.int32, (n_cmp, 1), 0) * CMP_STRIDE + (CMP_LEN - 1)
    bias_c = jnp.where(cmp_end <= t_row, 0.0, NEG)
    valid_c = jnp.where(t_row >= CMP_LEN - 1, 1.0, 0.0)
    cs = lax.broadcasted_iota(jnp.int32, (LANES, n_cmp), 1) * CMP_STRIDE
    bs = lax.broadcasted_iota(jnp.int32, (LANES, n_cmp), 0) * SEL_LEN
    overlap_t = jnp.where((cs < bs + SEL_LEN) & (cs + CMP_LEN > bs), 1.0, 0.0).astype(BF16)
    o_c = []
    imp = None
    s_c = [_dot(kc, q_t[j]) for j in hs]
    wl = WINDOW + tq
    w0 = pl.multiple_of(jnp.maximum(t0 - WINDOW, 0), tq)
    k_w = kw_ref[0, 0, pl.ds(w0, wl), :]
    for j in hs:
        sb_ref[j, 0:wl, :] = _dot(k_w, q_t[j])
    for j in hs:
        sc = s_c[j] + bias_c
        e_c = jnp.exp2(sc - jnp.max(sc, 0, keepdims=True))
        p16 = (e_c * (valid_c / jnp.sum(e_c, 0, keepdims=True))).astype(BF16)
        o_c.append(_dot(vc_t, p16))
        part = _dot(overlap_t, p16)
        imp = part if imp is None else imp + part

    jb = lax.broadcasted_iota(jnp.int32, (LANES, 1), 0)
    cur = t_row >> SEL_SHIFT
    forced = (jb == 0) | (jb == cur) | (jb == cur - 1)
    imp = jnp.where(forced, FORCE, jnp.where(jb * SEL_LEN <= t_row, imp, -FORCE))
    imp = jnp.where(jb < n_sel, imp, -jnp.inf)
    jbf = jb.astype(F32)
    for _ in range(min(SEL_TOP, n_sel)):
        mx = jnp.max(imp, 0, keepdims=True)
        first = jnp.min(jnp.where(imp == mx, jbf, float(LANES)), 0, keepdims=True)
        imp = jnp.where(jbf == first, -jnp.inf, imp)
    selm1_t = jnp.where((imp == -jnp.inf) & (jb < n_sel), 0.0, -1.0).astype(BF16)
    for j in hs:
        qa_ref[j, 0:LANES, :] = selm1_t
        qa_ref[j, LANES:2 * LANES, :] = qt_ref[0, 0, j]

    def scores(kt, s_ref):
        k_aug = ksa_ref[0, 0, pl.ds(pl.multiple_of(kt * tk, tk), tk), :]
        for j in hs:
            s_ref[j] = _dot(k_aug, qa_ref[j])

    def softmax_pv(kt, s_ref, state, diagonal):
        k0 = pl.multiple_of(kt * tk, tk)
        m, acc = list(state[0]), list(state[1])
        v_t = vst_ref[0, 0, :, pl.ds(k0, tk)]
        if diagonal:
            tok = k0 + lax.broadcasted_iota(jnp.int32, (tk, 1), 0)
            bias = jnp.where(tok <= t_row, 0.0, NEG)
        for j in hs:
            sj = s_ref[j] + bias if diagonal else s_ref[j]
            m_new = jnp.maximum(m[j], jnp.max(sj, 0, keepdims=True))
            alpha = jnp.exp2(m[j] - m_new)
            e16 = jnp.exp2((sj - m_new).astype(BF16))
            acc[j] = alpha * acc[j] + _dot(v_t, e16)
            m[j] = m_new
        return tuple(m), tuple(acc)

    def tile_pair(p, state):
        scores(2 * p + 1, sb_ref)
        state = softmax_pv(2 * p, sa_ref, state, False)
        scores(2 * p + 2, sa_ref)
        return softmax_pv(2 * p + 1, sb_ref, state, False)

    def finish(state):
        acc = state[1]
        for j in hs:
            os_ref[j] = acc[j][0:NSA_DH] / jnp.maximum(acc[j][NSA_DH:NSA_DH + 1], 1e-30)

    k_diag = t0 // tk
    state0 = (tuple(jnp.full((1, tq), NEG, F32) for _ in hs),
              tuple(jnp.zeros((V_ROWS, tq), F32) for _ in hs))
    scores(0, sa_ref)

    dist = t_row - (w0 + lax.broadcasted_iota(jnp.int32, (wl, 1), 0))
    bias_w = jnp.where((dist >= 0) & (dist < WINDOW), 0.0, NEG)
    vw_t = vwt_ref[0, 0, :, pl.ds(w0, wl)]
    o_w = []
    for j in hs:
        sw = sb_ref[j, 0:wl, :] + bias_w
        e16 = jnp.exp2((sw - jnp.max(sw, 0, keepdims=True)).astype(BF16))
        oa = _dot(vw_t, e16)
        o_w.append(oa[0:NSA_DH] / oa[NSA_DH:NSA_DH + 1])

    state = lax.fori_loop(0, k_diag // 2, tile_pair, state0)

    @pl.when(k_diag % 2 == 0)
    def _():
        finish(softmax_pv(k_diag, sa_ref, state, True))

    @pl.when(k_diag % 2 == 1)
    def _():
        scores(k_diag, sb_ref)
        finish(softmax_pv(k_diag, sb_ref, softmax_pv(k_diag - 1, sa_ref, state, False), True))

    o_s = [os_ref[j] for j in hs]

    gates = _sigmoid(gt_ref[0, 0])
    for j in range(hp):
        o = (gates[3 * j:3 * j + 1] * o_c[j] + gates[3 * j + 1:3 * j + 2] * o_s[j]
             + gates[3 * j + 2:3 * j + 3] * o_w[j])
        o_ref[0, j * NSA_DH:(j + 1) * NSA_DH, :] = o.astype(o_ref.dtype)


def _nsa_attention(q_t, kc, vc_t, ksa, vs_t, kw, vw_t, gates_t, tq, tk):
    b, g, hp, _, s = q_t.shape
    dh = NSA_DH
    n_cmp = kc.shape[2]
    assert s >= WINDOW + tq and WINDOW % tq == 0 and s % tk == 0 and tk % tq == 0 and tq % LANES == 0
    vt_spec = pl.BlockSpec((1, 1, V_ROWS, s), lambda bi, gi, i: (bi, gi, 0, 0))
    return pl.pallas_call(
        functools.partial(_nsa_body, tq=tq, tk=tk, n_sel=s // SEL_LEN),
        grid=(b, g, s // tq),
        in_specs=[pl.BlockSpec((1, 1, hp, LANES, tq), lambda bi, gi, i: (bi, gi, 0, 0, i)),
                  pl.BlockSpec((1, 1, n_cmp, dh), lambda bi, gi, i: (bi, gi, 0, 0)),
                  pl.BlockSpec((1, 1, dh, n_cmp), lambda bi, gi, i: (bi, gi, 0, 0)),
                  pl.BlockSpec((1, 1, s, 2 * LANES), lambda bi, gi, i: (bi, gi, 0, 0)),
                  vt_spec,
                  pl.BlockSpec((1, 1, s, dh), lambda bi, gi, i: (bi, gi, 0, 0)),
                  vt_spec,
                  pl.BlockSpec((1, 1, 3 * hp, tq), lambda bi, gi, i: (bi, gi, 0, i))],
        out_specs=pl.BlockSpec((1, hp * dh, tq), lambda bi, gi, i: (bi, gi, i)),
        out_shape=jax.ShapeDtypeStruct((b, g * hp * dh, s), BF16),
        scratch_shapes=[pltpu.VMEM((hp, 2 * LANES, tq), BF16), pltpu.VMEM((hp, tk, tq), F32),
                        pltpu.VMEM((hp, tk, tq), F32), pltpu.VMEM((hp, dh, tq), F32)],
        compiler_params=_params(("parallel", "parallel", "arbitrary")),
        name="nsa_attention",
    )(q_t, kc, vc_t, ksa, vs_t, kw, vw_t, gates_t)


def _mix_out_body(x_ref, oa_ref, ob_ref, mg_ref, gm_ref, woa_ref, wob_ref, wout_ref, o_ref):
    d = x_ref.shape[2]
    y_a = _dot(oa_ref[0], woa_ref[...])
    y_b = _dot_tn(ob_ref[0], wob_ref[...])
    merged = (_sigmoid(mg_ref[0, :, 0:d].astype(F32)) * y_a
              + _sigmoid(mg_ref[0, :, d:2 * d].astype(F32)) * y_b)
    o_ref[0] = x_ref[0] + gm_ref[0] * _dot(merged.astype(BF16), wout_ref[...])


def _mix_out(x, o_a, o_b, proj, mg_blk, g_m, w_oa, w_ob, w_out, tm):
    b, s, d = x.shape
    full = lambda a: pl.BlockSpec(a.shape, lambda bi, i: (0, 0))
    return pl.pallas_call(
        _mix_out_body,
        grid=(b, s // tm),
        in_specs=[pl.BlockSpec((1, tm, d), lambda bi, i: (bi, i, 0)),
                  pl.BlockSpec((1, tm, o_a.shape[2]), lambda bi, i: (bi, i, 0)),
                  pl.BlockSpec((1, o_b.shape[1], tm), lambda bi, i: (bi, 0, i)),
                  pl.BlockSpec((1, tm, 2 * d), lambda bi, i: (bi, i, mg_blk)),
                  pl.BlockSpec((1, 1, d), lambda bi, i: (bi, 0, 0)),
                  full(w_oa), full(w_ob), full(w_out)],
        out_specs=pl.BlockSpec((1, tm, d), lambda bi, i: (bi, i, 0)),
        out_shape=jax.ShapeDtypeStruct(x.shape, F32),
        compiler_params=_params(("parallel", "parallel")),
        name="mix_out",
    )(x, o_a, o_b, proj, g_m, w_oa, w_ob, w_out)


def _ffn_body(x_ref, nw_ref, sc_ref, sh_ref, gf_ref, w1_ref, w3_ref, w2_ref, o_ref, h_ref, acc_ref):
    f = pl.program_id(2)

    @pl.when(f == 0)
    def _():
        h_ref[...] = _norm_mod(x_ref[0], nw_ref[...], sc_ref[0], sh_ref[0]).astype(BF16)
        acc_ref[...] = jnp.zeros_like(acc_ref)

    h = h_ref[...]
    t = _silu(_dot(h, w1_ref[...])) * _dot(h, w3_ref[...])
    acc_ref[...] += _dot(t.astype(BF16), w2_ref[...])

    @pl.when(f == pl.num_programs(2) - 1)
    def _():
        o_ref[0] = x_ref[0] + gf_ref[0] * acc_ref[...]


def _dense_ffn(x, nw, sc, sh, g_f, w1, w3, w2, tm, tf):
    b, s, d = x.shape
    ff = w1.shape[1]
    vec = pl.BlockSpec((1, 1, d), lambda bi, i, f: (bi, 0, 0))
    return pl.pallas_call(
        _ffn_body,
        grid=(b, s // tm, ff // tf),
        in_specs=[pl.BlockSpec((1, tm, d), lambda bi, i, f: (bi, i, 0)),
                  pl.BlockSpec((1, d), lambda bi, i, f: (0, 0)), vec, vec, vec,
                  pl.BlockSpec((d, tf), lambda bi, i, f: (0, f)),
                  pl.BlockSpec((d, tf), lambda bi, i, f: (0, f)),
                  pl.BlockSpec((tf, d), lambda bi, i, f: (f, 0))],
        out_specs=pl.BlockSpec((1, tm, d), lambda bi, i, f: (bi, i, 0)),
        out_shape=jax.ShapeDtypeStruct(x.shape, F32),
        scratch_shapes=[pltpu.VMEM((tm, d), BF16), pltpu.VMEM((tm, d), F32)],
        compiler_params=_params(("parallel", "parallel", "arbitrary")),
        name="dense_ffn",
    )(x, nw, sc, sh, g_f, w1, w3, w2)


MOE_ROWS = 64
MOE_GROUP = 4
MOE_SCATTER = 512


def _moe_route_body(x_ref, nw_ref, sc_ref, sh_ref, wr_ref, h_ref, route_ref, meta_ref):
    tm = x_ref.shape[1]
    h16 = _norm_mod(x_ref[0], nw_ref[...], sc_ref[0], sh_ref[0]).astype(BF16)
    h_ref[0] = h16
    logits = _dot_nt(wr_ref[...], h16)
    ef = lax.broadcasted_iota(jnp.int32, (N_EXPERTS, tm), 0).astype(F32)
    m1 = jnp.max(logits, 0, keepdims=True)
    i1 = jnp.min(jnp.where(logits == m1, ef, float(N_EXPERTS)), 0, keepdims=True)
    rest = jnp.where(ef == i1, -jnp.inf, logits)
    m2 = jnp.max(rest, 0, keepdims=True)
    i2 = jnp.min(jnp.where(rest == m2, ef, float(N_EXPERTS)), 0, keepdims=True)
    e2 = jnp.exp(m2 - m1)
    oh1 = jnp.where(ef == i1, 1.0, 0.0)
    oh2 = jnp.where(ef == i2, 1.0, 0.0)
    member = oh1 + oh2
    lane = lax.broadcasted_iota(jnp.int32, (N_EXPERTS, tm), 1)
    csum = member
    sft = 1
    while sft < tm:
        csum = csum + jnp.where(lane >= sft, pltpu.roll(csum, sft, 1), 0.0)
        sft *= 2
    count = jnp.max(csum, 1, keepdims=True)
    nblk = jnp.floor((count + (MOE_ROWS - 1)) * (1.0 / MOE_ROWS))
    nblk_b = jnp.broadcast_to(nblk, (N_EXPERTS, LANES))
    row = lax.broadcasted_iota(jnp.int32, (N_EXPERTS, LANES), 0)
    bsum = nblk_b
    for sft in (1, 2, 4):
        bsum = bsum + jnp.where(row >= sft, pltpu.roll(bsum, sft, 0), 0.0)
    bstart = bsum - nblk_b
    slot = bstart[:, 0:1] * MOE_ROWS + (csum - member)
    rrow = lax.broadcasted_iota(jnp.int32, (8, tm), 0)
    route_ref[0] = jnp.where(
        rrow == 0, jnp.sum(oh1 * slot, 0, keepdims=True),
        jnp.where(rrow == 1, jnp.sum(oh2 * slot, 0, keepdims=True),
                  jnp.where(rrow == 2, 1.0 / (1.0 + e2), jnp.where(rrow == 3, e2 / (1.0 + e2), 0.0))))
    col = lax.broadcasted_iota(jnp.int32, (N_EXPERTS, LANES), 1)
    meta_ref[0] = jnp.where(col == 0, nblk_b, jnp.where(col == 1, bstart, 0.0)).astype(jnp.int32)


def _moe_group_body(nblk_ref, bstart_ref, x_ref, h_ref, route_ref, gf_ref, w1_ref, w3_ref, w2_ref, o_ref,
                    hb_ref, cw_ref, acc_ref, *, n_rows):
    i = pl.program_id(0)
    e = pl.program_id(1)
    f = pl.program_id(2)
    tm = x_ref.shape[1]
    nb = nblk_ref[i * N_EXPERTS + e]
    b0 = bstart_ref[i * N_EXPERTS + e]
    slot1 = route_ref[0, 0:1, :]
    slot2 = route_ref[0, 1:2, :]

    def hits(r0, rows):
        rr = (r0 + lax.broadcasted_iota(jnp.int32, (rows, 1), 0)).astype(F32)
        return rr == slot1, rr == slot2

    def expert_rows(blk, n_blk, first):
        r0 = pl.multiple_of((b0 + blk) * MOE_ROWS, MOE_ROWS)
        rows = pl.ds(r0, n_blk * MOE_ROWS)
        if first:
            hit1, hit2 = hits(r0, n_blk * MOE_ROWS)
            gather = jnp.where(hit1, 1.0, jnp.where(hit2, 1.0, 0.0)).astype(BF16)
            hb_ref[rows, :] = _dot(gather, h_ref[0]).astype(BF16)
            cw_ref[rows, :] = jnp.sum(jnp.where(hit1, route_ref[0, 2:3, :], 0.0)
                                      + jnp.where(hit2, route_ref[0, 3:4, :], 0.0), -1, keepdims=True)
        hb = hb_ref[rows, :]
        t = _silu(_dot(hb, w1_ref[0, 0])) * _dot(hb, w3_ref[0, 0]) * cw_ref[rows, :]
        y = _dot(t.astype(BF16), w2_ref[0])
        if first:
            acc_ref[rows, :] = y
        else:
            acc_ref[rows, :] += y

    def expert_all(first):
        lax.fori_loop(0, nb // MOE_GROUP, lambda k, c: (expert_rows(MOE_GROUP * k, MOE_GROUP, first), c)[1], 0)
        for rem in range(1, MOE_GROUP):
            @pl.when(nb % MOE_GROUP == rem)
            def _(rem=rem):
                expert_rows(nb - rem, rem, first)

    @pl.when(f == 0)
    def _():
        expert_all(True)

    @pl.when(f != 0)
    def _():
        expert_all(False)

    @pl.when((e == pl.num_programs(1) - 1) & (f == pl.num_programs(2) - 1))
    def _():
        def clear(k, c):
            acc_ref[pl.ds(pl.multiple_of(k * MOE_ROWS, MOE_ROWS), MOE_ROWS), :] = jnp.zeros(
                (MOE_ROWS, acc_ref.shape[1]), F32)
            return c

        lax.fori_loop(b0 + nb, n_rows // MOE_ROWS, clear, 0)
        chunk = MOE_SCATTER
        for kc in range(n_rows // chunk):
            hit1, hit2 = hits(kc * chunk, chunk)
            scatter = jnp.where(hit1, 1.0, jnp.where(hit2, 1.0, 0.0)).astype(BF16)
            y = _dot_tn(scatter, acc_ref[kc * chunk:(kc + 1) * chunk, :].astype(BF16))
            if kc == 0:
                o_ref[0] = y
            else:
                o_ref[0] += y
        o_ref[0] = x_ref[0] + gf_ref[0] * o_ref[0]


def _moe_ffn(x, nw, sc, sh, g_f, w_router, w1, w3, w2, tm, tf):
    b, s, d = x.shape
    n_e, _, ff = w1.shape
    assert n_e == N_EXPERTS
    tiles_b = s // tm
    nt = b * tiles_b
    vec = pl.BlockSpec((1, 1, d), lambda bi, i: (bi, 0, 0))
    h16, route, meta = pl.pallas_call(
        _moe_route_body,
        grid=(b, tiles_b),
        in_specs=[pl.BlockSpec((1, tm, d), lambda bi, i: (bi, i, 0)),
                  pl.BlockSpec((1, d), lambda bi, i: (0, 0)), vec, vec,
                  pl.BlockSpec((n_e, d), lambda bi, i: (0, 0))],
        out_specs=[pl.BlockSpec((1, tm, d), lambda bi, i: (bi, i, 0)),
                   pl.BlockSpec((1, 8, tm), lambda bi, i: (bi * tiles_b + i, 0, 0)),
                   pl.BlockSpec((1, n_e, LANES), lambda bi, i: (bi * tiles_b + i, 0, 0))],
        out_shape=[jax.ShapeDtypeStruct((b, s, d), BF16),
                   jax.ShapeDtypeStruct((nt, 8, tm), F32),
                   jax.ShapeDtypeStruct((nt, n_e, LANES), jnp.int32)],
        compiler_params=_params(("parallel", "parallel")),
        name="moe_route",
    )(x, nw, sc, sh, w_router.T.astype(BF16))
    chunked = lambda w: w.reshape(n_e, d, ff // tf, tf).transpose(0, 2, 1, 3).astype(BF16)
    n_rows = -(-(2 * tm + n_e * (MOE_ROWS - 1)) // MOE_SCATTER) * MOE_SCATTER
    grid_spec = pltpu.PrefetchScalarGridSpec(
        num_scalar_prefetch=2,
        grid=(nt, n_e, ff // tf),
        in_specs=[pl.BlockSpec((1, tm, d), lambda i, e, f, nb, bs: (i, 0, 0), pipeline_mode=pl.Buffered(1)),
                  pl.BlockSpec((1, tm, d), lambda i, e, f, nb, bs: (i, 0, 0), pipeline_mode=pl.Buffered(1)),
                  pl.BlockSpec((1, 8, tm), lambda i, e, f, nb, bs: (i, 0, 0)),
                  pl.BlockSpec((1, 1, d), lambda i, e, f, nb, bs: (i // tiles_b, 0, 0)),
                  pl.BlockSpec((1, 1, d, tf), lambda i, e, f, nb, bs: (e, f, 0, 0)),
                  pl.BlockSpec((1, 1, d, tf), lambda i, e, f, nb, bs: (e, f, 0, 0)),
                  pl.BlockSpec((1, tf, d), lambda i, e, f, nb, bs: (e, f, 0))],
        out_specs=pl.BlockSpec((1, tm, d), lambda i, e, f, nb, bs: (i, 0, 0)),
        scratch_shapes=[pltpu.VMEM((n_rows, d), BF16), pltpu.VMEM((n_rows, 1), F32),
                        pltpu.VMEM((n_rows, d), F32)])
    out = pl.pallas_call(
        functools.partial(_moe_group_body, n_rows=n_rows),
        grid_spec=grid_spec,
        out_shape=jax.ShapeDtypeStruct((nt, tm, d), F32),
        compiler_params=_params(("parallel", "arbitrary", "arbitrary")),
        name="moe_group",
    )(meta[:, :, 0].reshape(-1), meta[:, :, 1].reshape(-1),
      x.reshape(nt, tm, d), h16.reshape(nt, tm, d), route, g_f, chunked(w1), chunked(w3), w2)
    return out.reshape(b, s, d)


def _rope_tables(pos):
    inv = 1.0 / (ROPE_THETA ** (jnp.arange(0, NSA_DH, 2, dtype=F32) / NSA_DH))
    ang = pos.astype(F32)[..., None] * inv
    cos, sin = jnp.cos(ang), jnp.sin(ang)
    reps = LANES // NSA_DH
    return (jnp.tile(jnp.concatenate([cos, cos], -1), (1, 1, reps)),
            jnp.tile(jnp.concatenate([-sin, sin], -1), (1, 1, reps)))


_SPLITS = (DN_QKV, DN_HEADS * DN_DV, DN_HEADS, DN_HEADS, NSA_HEADS * NSA_DH) + (NSA_GROUPS * NSA_DH,) * 6
_OFF = np.concatenate([[0], np.cumsum(_SPLITS)])
_OFF_NG = int(_OFF[-1])
_OFF_MG = _OFF_NG + 3 * NSA_HEADS


def kernel(x, c, positions, w_ada, b_ada, norm_mix, norm_ffn, w_in, conv_w, a_log, dt_bias, dn_norm, cmp_pos, w_cmp1, w_cmp2, q_norm, k_norm, w_oa, w_ob, w_out, w1_dense, w3_dense, w2_dense, w_router, w1_moe, w3_moe, w2_moe):
    b, s, d = x.shape
    depth = w_in.shape[0]
    wdn = DN_QKV + DN_HEADS * DN_DV
    n_small = 2 * DN_HEADS + 3 * NSA_HEADS

    cos_f, sin_s = _rope_tables(positions)
    n_cmp_pad = s // CMP_STRIDE
    cmp_end = jnp.minimum(jnp.arange(n_cmp_pad) * CMP_STRIDE + CMP_LEN - 1, s - 1)
    cos_c, sin_c = _rope_tables(positions[:, cmp_end])

    mod = _ada_mod(c, w_ada, b_ada)

    off_nq = int(_OFF[4])
    w_main = jnp.concatenate([w_in[:, :, 0:wdn], w_in[:, :, _OFF_MG:_OFF_MG + 2 * d],
                              w_in[:, :, off_nq:_OFF_NG]], -1).astype(BF16)
    w_small = jnp.concatenate([w_in[:, :, wdn:wdn + 2 * DN_HEADS], w_in[:, :, _OFF_NG:_OFF_MG],
                               jnp.zeros((depth, d, LANES - n_small), F32)], -1).astype(BF16)
    nsa_col0 = wdn + 2 * d
    n_main = w_main.shape[2]

    w_oa16, w_ob16, w_out16 = w_oa.astype(BF16), w_ob.astype(BF16), w_out.astype(BF16)
    w1d, w3d, w2d = w1_dense.astype(BF16), w3_dense.astype(BF16), w2_dense.astype(BF16)
    w1m, w3m, w2m = w1_moe, w3_moe, w2_moe.astype(BF16)

    for l in range(depth):
        sh_m, sc_m, g_m, sh_f, sc_f, g_f = [m.reshape(b, 1, d) for m in jnp.split(mod[l], 6, -1)]
        nw_m = norm_mix[l].reshape(1, d)
        proj, small = _norm_mod_matmul(x, nw_m, sc_m, sh_m, w_main[l], w_small[l], tm=1024, tn=n_main // 3)
        a_t = jnp.swapaxes(small[:, :, DN_HEADS:2 * DN_HEADS], 1, 2)
        o_a = _deltanet(proj, small, a_t, conv_w[l], a_log[l], dt_bias[l], dn_norm[l], ts=512)
        qn, ksn, vs, kwn, vw, ck, cv = _nsa_prep(proj, nsa_col0, cos_f, sin_s, q_norm[l],
                                                 k_norm[l, 1], k_norm[l, 2], ts=512)
        kc, vc = _compress(ck, cv, cmp_pos[l], w_cmp1[l], w_cmp2[l], k_norm[l, 0], cos_c, sin_c)
        gates_t = jnp.swapaxes(small[:, :, 2 * DN_HEADS:n_small], 1, 2).reshape(b, NSA_GROUPS, 3 * NSA_HPG, s)
        o_b = _nsa_attention(qn, kc, vc, ksn, vs, kwn, vw, gates_t, tq=512, tk=1024)
        x = _mix_out(x, o_a, o_b, proj, wdn // (2 * d), g_m, w_oa16[l], w_ob16[l], w_out16[l], tm=512)
        nw_f = norm_ffn[l].reshape(1, d)
        if l % 2 == 0:
            x = _dense_ffn(x, nw_f, sc_f, sh_f, g_f, w1d[l // 2], w3d[l // 2], w2d[l // 2], tm=1024, tf=512)
        else:
            x = _moe_ffn(x, nw_f, sc_f, sh_f, g_f, w_router[l // 2], w1m[l // 2], w3m[l // 2],
                         w2m[l // 2], tm=1024, tf=1792)
    return x
```
